```python
import jax, jax.numpy as jnp
from jax import lax
import numpy as np

D_MODEL = 2048
BATCH = 2
SEQ = 8192
DEPTH = 1

HEAD_DIM = 128
ROPE_DIM = HEAD_DIM // 4
ROPE_THETA = 500000.0
NORM_EPS = 1e-6
Q_BLOCK = 128

A_HEADS = 8
A_Q_RANK = 512
A_KV_RANK = 256
A_ROPE = ROPE_DIM
A_NOPE = HEAD_DIM - ROPE_DIM
A_V_DIM = HEAD_DIM
IDX_HEADS = 8
IDX_DIM = 64
IDX_ROPE = IDX_DIM // 4
IDX_TOPK_MAX = 256

B_HEADS = 8
B_GROUPS = 2
B_HPG = B_HEADS // B_GROUPS
CMP_LEN = 32
CMP_STRIDE = 16
CMP_HIDDEN = 256
SLC_LEN = 64
SLC_TOPN = 16
WINDOW = 512

N_EXPERTS = 32
TOP_K = 4
D_FF = D_MODEL
SWIGLU_ALPHA = 1.702
SWIGLU_LIMIT = 7.0
EXPERT_BLOCK = 256

IN_SIZES = (A_Q_RANK, A_KV_RANK, A_ROPE, IDX_DIM, IDX_HEADS,
            B_HEADS * HEAD_DIM, 6 * B_GROUPS * HEAD_DIM, 3 * B_HEADS, 2 * D_MODEL)
IN_COLS = sum(IN_SIZES)

kernel_name = 'hybrid_dsa_nsa_moe_block'


def rms_norm(x, g):
    xf = x.astype(jnp.float32)
    y = xf * lax.rsqrt(jnp.mean(xf * xf, axis=-1, keepdims=True) + NORM_EPS)
    return (y * g.astype(jnp.float32)).astype(x.dtype)


def layer_norm(x, g, b):
    xf = x.astype(jnp.float32)
    xc = xf - jnp.mean(xf, axis=-1, keepdims=True)
    y = xc * lax.rsqrt(jnp.mean(xc * xc, axis=-1, keepdims=True) + NORM_EPS)
    return (y * g.astype(jnp.float32) + b.astype(jnp.float32)).astype(x.dtype)


def rope_tables(positions, rot_dim):
    inv = jnp.power(ROPE_THETA, -jnp.arange(0, rot_dim, 2, dtype=jnp.float32) / rot_dim)
    ang = positions.astype(jnp.float32)[..., None] * inv
    return jnp.cos(ang), jnp.sin(ang)


def apply_partial_rope(x, cos, sin, rot_dim):
    extra = x.ndim - cos.ndim
    shape = cos.shape[:2] + (1,) * extra + cos.shape[2:]
    c = cos.reshape(shape).astype(x.dtype)
    s = sin.reshape(shape).astype(x.dtype)
    half = rot_dim // 2
    x1, x2, xp = x[..., :half], x[..., half:rot_dim], x[..., rot_dim:]
    return jnp.concatenate([x1 * c - x2 * s, x2 * c + x1 * s, xp], axis=-1)


def masked_softmax(s, mask):
    s = jnp.where(mask, s.astype(jnp.float32), -jnp.inf)
    m = jnp.max(s, axis=-1, keepdims=True)
    m = jnp.where(jnp.isfinite(m), m, 0.0)
    e = jnp.where(mask, jnp.exp(s - m), 0.0)
    return e / jnp.maximum(jnp.sum(e, axis=-1, keepdims=True), 1e-30)


def to_blocks(a):
    b, t = a.shape[:2]
    return jnp.moveaxis(a.reshape((b, t // Q_BLOCK, Q_BLOCK) + a.shape[2:]), 1, 0)


def from_blocks(a):
    a = jnp.moveaxis(a, 0, 1)
    return a.reshape((a.shape[0], a.shape[1] * a.shape[2]) + a.shape[3:])


def split_cols(z, sizes):
    offs = np.cumsum((0,) + tuple(sizes))
    return [z[..., int(offs[i]):int(offs[i + 1])] for i in range(len(sizes))]


def gather_rows(tbl, idx):
    return jax.vmap(lambda t, i: t[i])(tbl, idx)


def dsa_mixer(cq, ckv, krope, kidx, widx, cos_h, sin_h, cos_i, sin_i,
              a_cq_norm, a_ckv_norm, a_w_uq, a_w_uk, a_w_uv,
              idx_w_q, idx_k_norm_g, idx_k_norm_b):
    b, t = cq.shape[:2]
    topk = min(IDX_TOPK_MAX, t // 4)
    cq = rms_norm(cq, a_cq_norm)
    ckv = rms_norm(ckv, a_ckv_norm)
    q = jnp.einsum('btr,rhd->bthd', cq, a_w_uq)
    q_rope = apply_partial_rope(q[..., :A_ROPE], cos_h, sin_h, A_ROPE)
    q_abs = jnp.einsum('bthn,rhn->bthr', q[..., A_ROPE:], a_w_uk)
    k_rope = apply_partial_rope(krope, cos_h, sin_h, A_ROPE)
    q_idx = apply_partial_rope(jnp.einsum('btr,rhd->bthd', cq, idx_w_q), cos_i, sin_i, IDX_ROPE)
    k_idx = apply_partial_rope(layer_norm(kidx, idx_k_norm_g, idx_k_norm_b), cos_i, sin_i, IDX_ROPE)
    w_idx = widx * IDX_HEADS ** -0.5
    key_pos = jnp.arange(t)
    scale = HEAD_DIM ** -0.5

    def block(xs):
        qi, wi, qa, qr, tq = xs
        logits = jnp.einsum('bqhd,bsd->bqhs', qi, k_idx).astype(jnp.float32) * IDX_DIM ** -0.5
        score = jnp.einsum('bqh,bqhs->bqs', wi.astype(jnp.float32), jax.nn.relu(logits))
        causal = key_pos[None, :] <= tq[:, None]
        score = jnp.where(causal[None], score, -jnp.inf)
        _, sel = lax.top_k(score, topk)
        valid = sel <= tq[None, :, None]
        ckv_sel = gather_rows(ckv, sel)
        kr_sel = gather_rows(k_rope, sel)
        s = (jnp.einsum('bqhr,bqkr->bqhk', qa, ckv_sel)
             + jnp.einsum('bqhd,bqkd->bqhk', qr, kr_sel)) * scale
        p = masked_softmax(s, valid[:, :, None, :])
        return jnp.einsum('bqhk,bqkr->bqhr', p.astype(ckv_sel.dtype), ckv_sel)

    xs = (to_blocks(q_idx), to_blocks(w_idx), to_blocks(q_abs), to_blocks(q_rope),
          jnp.arange(t).reshape(t // Q_BLOCK, Q_BLOCK))
    o_lat = from_blocks(lax.map(block, xs))
    o = jnp.einsum('bthr,hrv->bthv', o_lat, a_w_uv)
    return o.reshape(b, t, A_HEADS * A_V_DIM)


def compress_blocks(tok, idx_cmp, pe, w1, w2):
    blocks = tok[:, idx_cmp] + pe[:, None, :]
    b, n, l, g, d = blocks.shape
    flat = jnp.moveaxis(blocks, 2, 3).reshape(b, n, g, l * d)
    return jax.nn.gelu(flat @ w1) @ w2


def nsa_mixer(q, kv, gates, cos_h, sin_h,
              cmp_k_pe, cmp_k_w1, cmp_k_w2, cmp_v_pe, cmp_v_w1, cmp_v_w2):
    b, t = q.shape[:2]
    n_cmp = (t - CMP_LEN) // CMP_STRIDE + 1
    n_slc = t // SLC_LEN
    n_sel = min(SLC_TOPN, n_slc)
    scale = HEAD_DIM ** -0.5
    k_c, v_c, k_s, v_s, k_w, v_w = [kv[:, :, i] for i in range(6)]
    q = apply_partial_rope(q, cos_h, sin_h, ROPE_DIM).reshape(b, t, B_GROUPS, B_HPG, HEAD_DIM)
    k_c = apply_partial_rope(k_c, cos_h, sin_h, ROPE_DIM)
    k_s = apply_partial_rope(k_s, cos_h, sin_h, ROPE_DIM)
    k_w = apply_partial_rope(k_w, cos_h, sin_h, ROPE_DIM)
    cmp_start = jnp.arange(n_cmp) * CMP_STRIDE
    cmp_end = cmp_start + CMP_LEN - 1
    idx_cmp = cmp_start[:, None] + jnp.arange(CMP_LEN)[None, :]
    kc = compress_blocks(k_c, idx_cmp, cmp_k_pe, cmp_k_w1, cmp_k_w2)
    vc = compress_blocks(v_c, idx_cmp, cmp_v_pe, cmp_v_w1, cmp_v_w2)
    slc_start = jnp.arange(n_slc) * SLC_LEN
    overlap = ((cmp_start[:, None] < slc_start[None, :] + SLC_LEN)
               & (cmp_end[:, None] >= slc_start[None, :])).astype(jnp.float32)
    ks_blk = jnp.moveaxis(k_s.reshape(b, n_slc, SLC_LEN, B_GROUPS, HEAD_DIM), 3, 1)
    vs_blk = jnp.moveaxis(v_s.reshape(b, n_slc, SLC_LEN, B_GROUPS, HEAD_DIM), 3, 1)
    pad = ((0, 0), (WINDOW, 0), (0, 0), (0, 0))
    kw_pad = jnp.pad(k_w, pad)
    vw_pad = jnp.pad(v_w, pad)
    blk_ids = jnp.arange(n_slc)
    win_off = jnp.arange(WINDOW + Q_BLOCK) - WINDOW
    gather_blocks = jax.vmap(jax.vmap(lambda tbl, idx: tbl[idx]))

    def block(xs):
        qb, gb, tq, q0 = xs
        sc = jnp.einsum('bqgjd,bngd->bqgjn', qb, kc) * scale
        cmask = cmp_end[None, :] <= tq[:, None]
        pc = masked_softmax(sc, cmask[None, :, None, None, :])
        oc = jnp.einsum('bqgjn,bngd->bqgjd', pc.astype(vc.dtype), vc)
        imp = jnp.einsum('bqgjn,nm->bqgm', pc, overlap)
        cur = tq // SLC_LEN
        forced = ((blk_ids[None, :] == 0) | (blk_ids[None, :] == cur[:, None])
                  | (blk_ids[None, :] == cur[:, None] - 1))
        adm = blk_ids[None, :] * SLC_LEN <= tq[:, None]
        imp = jnp.where(forced[None, :, None, :], jnp.inf, imp)
        imp = jnp.where(adm[None, :, None, :], imp, -jnp.inf)
        _, sel = lax.top_k(imp, n_sel)
        sel_t = jnp.transpose(sel, (0, 2, 1, 3))
        ks_g = gather_blocks(ks_blk, sel_t)
        vs_g = gather_blocks(vs_blk, sel_t)
        kpos = sel_t[..., None] * SLC_LEN + jnp.arange(SLC_LEN)
        smask = jnp.transpose(kpos <= tq[None, None, :, None, None], (0, 2, 1, 3, 4))
        smask = smask.reshape(b, Q_BLOCK, B_GROUPS, 1, n_sel * SLC_LEN)
        ss = jnp.einsum('bqgjd,bgqnsd->bqgjns', qb, ks_g) * scale
        ss = ss.reshape(b, Q_BLOCK, B_GROUPS, B_HPG, n_sel * SLC_LEN)
        ps = masked_softmax(ss, smask)
        os_ = jnp.einsum('bqgjk,bgqkd->bqgjd', ps.astype(vs_g.dtype),
                         vs_g.reshape(b, B_GROUPS, Q_BLOCK, n_sel * SLC_LEN, HEAD_DIM))
        kw_b = lax.dynamic_slice_in_dim(kw_pad, q0, WINDOW + Q_BLOCK, axis=1)
        vw_b = lax.dynamic_slice_in_dim(vw_pad, q0, WINDOW + Q_BLOCK, axis=1)
        wpos = q0 + win_off
        wmask = ((wpos[None, :] <= tq[:, None]) & (wpos[None, :] > tq[:, None] - WINDOW)
                 & (wpos[None, :] >= 0))
        sw = jnp.einsum('bqgjd,bsgd->bqgjs', qb, kw_b) * scale
        pw = masked_softmax(sw, wmask[None, :, None, None, :])
        ow = jnp.einsum('bqgjs,bsgd->bqgjd', pw.astype(vw_b.dtype), vw_b)
        g = gb.reshape(b, Q_BLOCK, B_GROUPS, B_HPG, 3)
        return g[..., 0:1] * oc + g[..., 1:2] * os_ + g[..., 2:3] * ow

    nb = t // Q_BLOCK
    xs = (to_blocks(q), to_blocks(gates), jnp.arange(t).reshape(nb, Q_BLOCK),
          jnp.arange(nb) * Q_BLOCK)
    o = from_blocks(lax.map(block, xs))
    return o.reshape(b, t, B_HEADS * HEAD_DIM)


def moe_ffn(h, w_router, b_router, w_gate_up, b_gate_up, w_down, b_down):
    b, t, d = h.shape
    n = b * t
    hf = h.reshape(n, d)
    logits = (hf @ w_router + b_router).astype(jnp.float32)
    top_val, top_idx = lax.top_k(logits, TOP_K)
    gates = jax.nn.softmax(top_val, axis=-1)
    nk = n * TOP_K
    e_flat = top_idx.reshape(nk)
    tok_flat = jnp.repeat(jnp.arange(n, dtype=jnp.int32), TOP_K)
    w_flat = gates.reshape(nk)
    order = jnp.argsort(e_flat)
    e_s, tok_s, w_s = e_flat[order], tok_flat[order], w_flat[order]
    counts = jnp.bincount(e_flat, length=N_EXPERTS)
    starts = jnp.cumsum(counts) - counts
    padded = (counts + EXPERT_BLOCK - 1) // EXPERT_BLOCK * EXPERT_BLOCK
    pend = jnp.cumsum(padded)
    pstart = pend - padded
    dest = pstart[e_s] + jnp.arange(nk) - starts[e_s]
    n_blocks = -(-nk // EXPERT_BLOCK) + N_EXPERTS
    p_rows = n_blocks * EXPERT_BLOCK
    pad_tok = jnp.full((p_rows,), n, jnp.int32).at[dest].set(tok_s)
    pad_w = jnp.zeros((p_rows,), jnp.float32).at[dest].set(w_s)
    blk_expert = jnp.minimum(
        jnp.searchsorted(pend, jnp.arange(n_blocks) * EXPERT_BLOCK, side='right'), N_EXPERTS - 1)
    x_src = jnp.concatenate([hf, jnp.zeros((1, d), hf.dtype)], axis=0)

    def expert_block(xs):
        tok, wt, e = xs
        xb = x_src[tok]
        gu = xb @ w_gate_up[e] + b_gate_up[e]
        gate, up = jnp.split(gu, 2, axis=-1)
        gate = jnp.minimum(gate, SWIGLU_LIMIT)
        up = jnp.clip(up, -SWIGLU_LIMIT, SWIGLU_LIMIT)
        act = gate * jax.nn.sigmoid(SWIGLU_ALPHA * gate) * (up + 1.0)
        y = act @ w_down[e] + b_down[e]
        return y * wt[:, None].astype(y.dtype)

    ys = lax.map(expert_block, (pad_tok.reshape(n_blocks, EXPERT_BLOCK),
                                pad_w.reshape(n_blocks, EXPERT_BLOCK), blk_expert))
    out = jax.ops.segment_sum(ys.reshape(p_rows, d), pad_tok, num_segments=n + 1)[:n]
    return out.reshape(b, t, d)


def setup_inputs(seed: int = 0) -> dict:
    key = jax.random.key(seed)
    ks = jax.random.split(key, 34)

    def nrm(k, shape, scale):
        return jax.random.normal(k, shape, jnp.float32) * scale

    D = D_MODEL
    positions = (jax.random.randint(ks[2], (BATCH, 1), 0, 4096)
                 + jnp.arange(SEQ)[None, :]).astype(jnp.int32)
    return {
        'x': nrm(ks[0], (BATCH, SEQ, D), 1.0),
        'c': nrm(ks[1], (BATCH, D), 1.0),
        'positions': positions,
        'w_ada': nrm(ks[3], (D, 6 * D), 0.5 * D ** -0.5),
        'b_ada': nrm(ks[4], (6 * D,), 0.02),
        'g_pre_mix': 1.0 + nrm(ks[5], (D,), 0.05),
        'g_post_mix': 1.0 + nrm(ks[6], (D,), 0.05),
        'g_pre_ffn': 1.0 + nrm(ks[7], (D,), 0.05),
        'g_post_ffn': 1.0 + nrm(ks[8], (D,), 0.05),
        'w_in': nrm(ks[9], (D, IN_COLS), D ** -0.5),
        'a_cq_norm': 1.0 + nrm(ks[10], (A_Q_RANK,), 0.05),
        'a_ckv_norm': 1.0 + nrm(ks[11], (A_KV_RANK,), 0.05),
        'a_w_uq': nrm(ks[12], (A_Q_RANK, A_HEADS, HEAD_DIM), A_Q_RANK ** -0.5),
        'a_w_uk': nrm(ks[13], (A_KV_RANK, A_HEADS, A_NOPE), A_KV_RANK ** -0.5),
        'a_w_uv': nrm(ks[14], (A_HEADS, A_KV_RANK, A_V_DIM), A_KV_RANK ** -0.5),
        'idx_w_q': nrm(ks[15], (A_Q_RANK, IDX_HEADS, IDX_DIM), A_Q_RANK ** -0.5),
        'idx_k_norm_g': 1.0 + nrm(ks[16], (IDX_DIM,), 0.05),
        'idx_k_norm_b': nrm(ks[17], (IDX_DIM,), 0.02),
        'cmp_k_pe': nrm(ks[18], (CMP_LEN, HEAD_DIM), 0.1),
        'cmp_k_w1': nrm(ks[19], (CMP_LEN * HEAD_DIM, CMP_HIDDEN), (CMP_LEN * HEAD_DIM) ** -0.5),
        'cmp_k_w2': nrm(ks[20], (CMP_HIDDEN, HEAD_DIM), CMP_HIDDEN ** -0.5),
        'cmp_v_pe': nrm(ks[21], (CMP_LEN, HEAD_DIM), 0.1),
        'cmp_v_w1': nrm(ks[22], (CMP_LEN * HEAD_DIM, CMP_HIDDEN), (CMP_LEN * HEAD_DIM) ** -0.5),
        'cmp_v_w2': nrm(ks[23], (CMP_HIDDEN, HEAD_DIM), CMP_HIDDEN ** -0.5),
        'w_br_a': nrm(ks[24], (A_HEADS * A_V_DIM, D), (A_HEADS * A_V_DIM) ** -0.5),
        'w_br_b': nrm(ks[25], (B_HEADS * HEAD_DIM, D), (B_HEADS * HEAD_DIM) ** -0.5),
        'w_out': nrm(ks[26], (D, D), D ** -0.5),
        'w_router': nrm(ks[27], (D, N_EXPERTS), D ** -0.5),
        'b_router': nrm(ks[28], (N_EXPERTS,), 0.01),
        'w_gate_up': nrm(ks[29], (N_EXPERTS, D, 2 * D_FF), D ** -0.5),
        'b_gate_up': nrm(ks[30], (N_EXPERTS, 2 * D_FF), 0.01),
        'w_down': nrm(ks[31], (N_EXPERTS, D_FF, D), D_FF ** -0.5),
        'b_down': nrm(ks[32], (N_EXPERTS, D), 0.01),
    }


def reference(x, c, positions, w_ada, b_ada, g_pre_mix, g_post_mix, g_pre_ffn, g_post_ffn,
              w_in, a_cq_norm, a_ckv_norm, a_w_uq, a_w_uk, a_w_uv,
              idx_w_q, idx_k_norm_g, idx_k_norm_b,
              cmp_k_pe, cmp_k_w1, cmp_k_w2, cmp_v_pe, cmp_v_w1, cmp_v_w2,
              w_br_a, w_br_b, w_out,
              w_router, b_router, w_gate_up, b_gate_up, w_down, b_down):
    b, t, _ = x.shape
    cos_h, sin_h = rope_tables(positions, ROPE_DIM)
    cos_i, sin_i = rope_tables(positions, IDX_ROPE)
    for _ in range(DEPTH):
        mod = (jax.nn.silu(c) @ w_ada + b_ada)[:, None, :]
        sh1, sc1, gt1, sh2, sc2, gt2 = jnp.split(mod, 6, axis=-1)
        h = rms_norm(x, g_pre_mix) * (1.0 + sc1) + sh1
        z = h @ w_in
        cq, ckv, krope, kidx, widx, bq, bkv, bgate, mgate = split_cols(z, IN_SIZES)
        o_a = dsa_mixer(cq, ckv, krope, kidx, widx, cos_h, sin_h, cos_i, sin_i,
                        a_cq_norm, a_ckv_norm, a_w_uq, a_w_uk, a_w_uv,
                        idx_w_q, idx_k_norm_g, idx_k_norm_b)
        o_b = nsa_mixer(bq.reshape(b, t, B_HEADS, HEAD_DIM),
                        bkv.reshape(b, t, 6, B_GROUPS, HEAD_DIM),
                        jax.nn.sigmoid(bgate).reshape(b, t, B_HEADS, 3),
                        cos_h, sin_h, cmp_k_pe, cmp_k_w1, cmp_k_w2,
                        cmp_v_pe, cmp_v_w1, cmp_v_w2)
        g_a, g_b = jnp.split(jax.nn.sigmoid(mgate), 2, axis=-1)
        merged = g_a * (o_a @ w_br_a) + g_b * (o_b @ w_br_b)
        x = x + gt1 * rms_norm(merged @ w_out, g_post_mix)
        h = rms_norm(x, g_pre_ffn) * (1.0 + sc2) + sh2
        y = moe_ffn(h, w_router, b_router, w_gate_up, b_gate_up, w_down, b_down)
        x = x + gt2 * rms_norm(y, g_post_ffn)
    return x
```

```python
import functools

import numpy as np
import jax
import jax.numpy as jnp
from jax import lax
from jax.experimental import pallas as pl
from jax.experimental.pallas import tpu as pltpu

F32 = jnp.float32
BF16 = jnp.bfloat16
I32 = jnp.int32

HEAD_DIM = 128
ROPE_DIM = 32
ROPE_THETA = 500000.0
NORM_EPS = 1e-6
Q_BLOCK = 128
A_HEADS = 8
A_Q_RANK = 512
A_KV_RANK = 256
IDX_HEADS = 8
IDX_DIM = 64
IDX_ROPE = 16
IDX_TOPK_MAX = 256
B_HEADS = 8
B_GROUPS = 2
B_HPG = 4
CMP_LEN = 32
CMP_STRIDE = 16
CMP_HIDDEN = 256
SLC_LEN = 64
SLC_TOPN = 16
WINDOW = 512
N_EXPERTS = 32
TOP_K = 4
SWIGLU_ALPHA = 1.702
SWIGLU_LIMIT = 7.0

LANES = 128
VMEM_LIMIT = 52 * 1024 * 1024

KEY_TILE = 512
QA_DIM = A_KV_RANK + LANES
ROW_BLK = 512
FF_TILE = 512
DMA_ROWS = 512
CMB_TOK = 128

Z_MGATE, Z_BQ, Z_CQ, Z_CKV, Z_SMALL, Z_BKV, Z_COLS = 0, 4096, 5120, 5632, 5888, 6144, 7680

NEG_BIG = -1e30
INT_MIN = np.int32(-2147483648)


def _cparams(sem):
    return pltpu.CompilerParams(dimension_semantics=sem, vmem_limit_bytes=VMEM_LIMIT)


def _dot(a, b):
    return jnp.dot(a, b, preferred_element_type=F32)


def _dot_nt(a, b):
    return lax.dot_general(a, b, (((1,), (1,)), ((), ())), preferred_element_type=F32)


def _split3(a):
    hi = a.astype(BF16)
    r1 = a - hi.astype(F32)
    mid = r1.astype(BF16)
    lo = (r1 - mid.astype(F32)).astype(BF16)
    return hi, mid, lo


def _lane_tile(a, n):
    return jnp.concatenate([a] * n, axis=1) if n > 1 else a


def _row_tile(a, n):
    return jnp.concatenate([a] * n, axis=0) if n > 1 else a


def _f32_key(s):
    bits = pltpu.bitcast(s, I32)
    return bits ^ (jnp.right_shift(bits, 31) & np.int32(0x7FFFFFFF))


def _count(key_ref, nt, pred):
    rows, width = key_ref.shape[1], key_ref.shape[2]

    def body(kt, acc):
        ind = jnp.where(pred(key_ref[kt], kt), 1.0, 0.0)
        part = ind[:, 0:LANES]
        for c in range(1, width // LANES):
            part = part + ind[:, c * LANES:(c + 1) * LANES]
        return acc + part

    acc = lax.fori_loop(0, nt, body, jnp.zeros((rows, LANES), F32))
    return jnp.sum(acc, axis=1, keepdims=True)


def _kth_key(key_ref, nt, k):
    rows = key_ref.shape[1]

    def bit_body(b, ans):
        cand = ans | jnp.left_shift(jnp.int32(1), 31 - b)
        cs = cand ^ INT_MIN
        cnt = _count(key_ref, nt, lambda kk, kt: kk >= cs)
        return jnp.where(cnt >= k, cand, ans)

    ans = lax.fori_loop(0, 32, bit_body, jnp.zeros((rows, 1), I32))
    return ans ^ INT_MIN


def _break_ties(key_ref, nt, t, k, idx_bits):
    rows, width = key_ref.shape[1], key_ref.shape[2]
    cnt_gt = _count(key_ref, nt, lambda kk, kt: kk > t)
    cnt_ge = _count(key_ref, nt, lambda kk, kt: kk >= t)
    need = k - cnt_gt

    @pl.when(jnp.max(cnt_ge) > k)
    def _():
        lane = lax.broadcasted_iota(I32, (rows, width), 1)

        def bit_body(b, c0):
            cand = c0 | jnp.left_shift(jnp.int32(1), idx_bits - 1 - b)
            f = _count(key_ref, nt, lambda kk, kt: jnp.where(kk == t, kt * width + lane, cand) < cand)
            return jnp.where(f < need, cand, c0)

        c0 = lax.fori_loop(0, idx_bits, bit_body, jnp.zeros((rows, 1), I32))

        def fix(kt, carry):
            kk = key_ref[kt]
            idx = kt * width + lane
            key_ref[kt] = jnp.where(jnp.where(kk == t, idx, c0) > c0, t - 1, kk)
            return carry

        lax.fori_loop(0, nt, fix, 0)


def _ada_body(c_ref, w_ref, b_ref, o_ref):
    c = c_ref[...]
    s = c * jax.nn.sigmoid(c)
    s_hi, s_mid, s_lo = _split3(s)
    w_hi, w_mid, w_lo = _split3(w_ref[...])
    acc = _dot(s_hi, w_hi) + (_dot(s_hi, w_mid) + _dot(s_mid, w_hi))
    acc = acc + (_dot(s_hi, w_lo) + _dot(s_mid, w_mid) + _dot(s_lo, w_hi))
    o_ref[...] = acc + b_ref[...]


def _adaln(c, w_ada, b_ada):
    nb, d = c.shape
    n = w_ada.shape[1]
    tn = 1024
    cp = jnp.zeros((8, d), F32).at[:nb].set(c)
    out = pl.pallas_call(
        _ada_body,
        grid=(n // tn,),
        in_specs=[pl.BlockSpec((8, d), lambda j: (0, 0)),
                  pl.BlockSpec((d, tn), lambda j: (0, j)),
                  pl.BlockSpec((1, tn), lambda j: (0, j))],
        out_specs=pl.BlockSpec((8, tn), lambda j: (0, j)),
        out_shape=jax.ShapeDtypeStruct((8, n), F32),
        compiler_params=_cparams(("parallel",)),
    )(cp, w_ada, b_ada[None])
    return out[:nb]


def _inproj_body(x_ref, g_ref, sc_ref, sh_ref, w_ref, z_ref, h_ref):
    @pl.when(pl.program_id(2) == 0)
    def _():
        x = x_ref[...]
        y = x * lax.rsqrt(jnp.mean(x * x, axis=-1, keepdims=True) + NORM_EPS) * g_ref[...]
        h_ref[...] = (y * (1.0 + sc_ref[...]) + sh_ref[...]).astype(BF16)

    z_ref[...] = _dot(h_ref[...], w_ref[...])


def _inproj(x, g, sc, sh, w):
    nb, t, d = x.shape
    tm, tn = 512, 1536
    return pl.pallas_call(
        _inproj_body,
        grid=(nb, t // tm, Z_COLS // tn),
        in_specs=[pl.BlockSpec((None, tm, d), lambda b, i, j: (b, i, 0)),
                  pl.BlockSpec((1, d), lambda b, i, j: (0, 0)),
                  pl.BlockSpec((None, 1, d), lambda b, i, j: (b, 0, 0)),
                  pl.BlockSpec((None, 1, d), lambda b, i, j: (b, 0, 0)),
                  pl.BlockSpec((d, tn), lambda b, i, j: (0, j))],
        out_specs=pl.BlockSpec((None, tm, tn), lambda b, i, j: (b, i, j)),
        out_shape=jax.ShapeDtypeStruct((nb, t, Z_COLS), F32),
        scratch_shapes=[pltpu.VMEM((tm, d), BF16)],
        compiler_params=_cparams(("parallel", "parallel", "arbitrary")),
    )(x, g[None], sc, sh, w)


def _rms(x, g):
    return x * lax.rsqrt(jnp.mean(x * x, axis=-1, keepdims=True) + NORM_EPS) * g


def _dsa_prep_body(cq_ref, ckv_ref, sm_ref, tabh_ref, tabi_ref, gq_ref, gkv_ref,
                   wuq_ref, wuqr_ref, wuk_ref, widx_ref, widxr_ref, lng_ref, lnb_ref,
                   qa_ref, ka_ref, qi_ref, ki_ref, wi_ref, sg_ref):
    tm = cq_ref.shape[0]
    lane = lax.broadcasted_iota(I32, (tm, LANES), 1)
    cqn = _rms(cq_ref[...], gq_ref[...]).astype(BF16)
    ckvn = _rms(ckv_ref[...], gkv_ref[...])
    s0 = sm_ref[:, 0:LANES]
    s1 = sm_ref[:, LANES:2 * LANES]
    ch, sh = tabh_ref[:, 0:LANES], tabh_ref[:, LANES:2 * LANES]
    ci, si = tabi_ref[:, 0:LANES], tabi_ref[:, LANES:2 * LANES]

    k_rope = s0 * ch + pltpu.roll(s0, 32, 1) * sh
    ka_ref[:, 0:A_KV_RANK] = ckvn.astype(BF16)
    ka_ref[:, A_KV_RANK:QA_DIM] = jnp.where(lane < ROPE_DIM, k_rope, 1.0).astype(BF16)

    inside = jnp.where(lane >= 32, jnp.where(lane < 96, 1.0, 0.0), 0.0)
    mu = jnp.sum(s0 * inside, axis=-1, keepdims=True) * (1.0 / IDX_DIM)
    xc = (s0 - mu) * inside
    var = jnp.sum(xc * xc, axis=-1, keepdims=True) * (1.0 / IDX_DIM)
    y = xc * lax.rsqrt(var + NORM_EPS) * lng_ref[...] + lnb_ref[...]
    rot = jnp.where(lane < 40, pltpu.roll(y, LANES - 8, 1), pltpu.roll(y, 8, 1))
    ki_ref[...] = (y * ci + rot * si).astype(BF16)

    scale = HEAD_DIM ** -0.5
    w_scale = IDX_HEADS ** -0.5 * IDX_DIM ** -0.5
    for h in range(A_HEADS):
        qh = _dot(cqn, wuq_ref[h])
        qr = _dot(cqn, wuqr_ref[h])
        roped = qh * ch + qr * sh
        qabs = _dot(roped.astype(BF16), wuk_ref[h])
        qa_ref[h, :, 0:A_KV_RANK] = (qabs * scale).astype(BF16)
        qa_ref[h, :, A_KV_RANK:QA_DIM] = (jnp.where(lane < ROPE_DIM, roped, 0.0) * scale).astype(BF16)
        qih = _dot(cqn, widx_ref[h])
        qir = _dot(cqn, widxr_ref[h])
        qi_ref[h] = (qih * ci + qir * si).astype(BF16)
        wi_ref[h] = jnp.broadcast_to(s1[:, h:h + 1], (tm, LANES)) * w_scale
    sg_ref[...] = jax.nn.sigmoid(s1)


def _dsa_prep(z, tabh, tabi, gq, gkv, wuq, wuqr, wuk, widx, widxr, lng, lnb):
    nb, t, _ = z.shape
    tm = 256
    full = lambda a: pl.BlockSpec(a.shape, lambda b, i: (0,) * a.ndim)
    return pl.pallas_call(
        _dsa_prep_body,
        grid=(nb, t // tm),
        in_specs=[pl.BlockSpec((None, tm, A_Q_RANK), lambda b, i: (b, i, Z_CQ // A_Q_RANK)),
                  pl.BlockSpec((None, tm, A_KV_RANK), lambda b, i: (b, i, Z_CKV // A_KV_RANK)),
                  pl.BlockSpec((None, tm, 256), lambda b, i: (b, i, Z_SMALL // 256)),
                  pl.BlockSpec((None, tm, 256), lambda b, i: (b, i, 0)),
                  pl.BlockSpec((None, tm, 256), lambda b, i: (b, i, 0)),
                  full(gq), full(gkv), full(wuq), full(wuqr), full(wuk), full(widx), full(widxr),
                  full(lng), full(lnb)],
        out_specs=[pl.BlockSpec((None, A_HEADS, tm, QA_DIM), lambda b, i: (b, 0, i, 0)),
                   pl.BlockSpec((None, tm, QA_DIM), lambda b, i: (b, i, 0)),
                   pl.BlockSpec((None, IDX_HEADS, tm, LANES), lambda b, i: (b, 0, i, 0)),
                   pl.BlockSpec((None, tm, LANES), lambda b, i: (b, i, 0)),
                   pl.BlockSpec((None, IDX_HEADS, tm, LANES), lambda b, i: (b, 0, i, 0)),
                   pl.BlockSpec((None, tm, LANES), lambda b, i: (b, i, 0))],
        out_shape=[jax.ShapeDtypeStruct((nb, A_HEADS, t, QA_DIM), BF16),
                   jax.ShapeDtypeStruct((nb, t, QA_DIM), BF16),
                   jax.ShapeDtypeStruct((nb, IDX_HEADS, t, LANES), BF16),
                   jax.ShapeDtypeStruct((nb, t, LANES), BF16),
                   jax.ShapeDtypeStruct((nb, IDX_HEADS, t, LANES), F32),
                   jax.ShapeDtypeStruct((nb, t, LANES), F32)],
        compiler_params=_cparams(("parallel", "parallel")),
    )(z, z, z, tabh, tabi, gq, gkv, wuq, wuqr, wuk, widx, widxr, lng, lnb)


def _dsa_attn_body(qi_ref, wi_ref, ki_ref, qa_ref, ka_ref, wuv_ref, o_ref,
                   key_ref, acc_ref, m_ref, *, topk, idx_bits):
    qb = Q_BLOCK
    rows = A_HEADS * qb
    q0 = pl.program_id(1) * qb
    nt = (q0 + qb + KEY_TILE - 1) // KEY_TILE
    tq = q0 + lax.broadcasted_iota(I32, (qb, KEY_TILE), 0)
    lane = lax.broadcasted_iota(I32, (qb, KEY_TILE), 1)
    qi = qi_ref[...].reshape(rows, LANES)
    wi = wi_ref[...].reshape(rows, LANES)

    def score_body(kt, carry):
        k0 = pl.multiple_of(kt * KEY_TILE, KEY_TILE)
        logits = _dot_nt(qi, ki_ref[pl.ds(k0, KEY_TILE), :])
        r = jnp.maximum(logits, 0.0) * _lane_tile(wi, KEY_TILE // LANES)
        s = r[0:qb]
        for h in range(1, IDX_HEADS):
            s = s + r[h * qb:(h + 1) * qb]
        s = s + 0.0
        s = jnp.where(k0 + lane <= tq, s, -jnp.inf)
        key_ref[kt] = _f32_key(s)
        return carry

    lax.fori_loop(0, nt, score_body, 0)
    thr = _kth_key(key_ref, nt, topk)
    _break_ties(key_ref, nt, thr, topk, idx_bits)

    qa = qa_ref[...].reshape(rows, QA_DIM)
    m_ref[...] = jnp.full((rows, LANES), NEG_BIG, F32)
    acc_ref[...] = jnp.zeros((rows, QA_DIM), F32)

    def attn_body(kt, carry):
        k0 = pl.multiple_of(kt * KEY_TILE, KEY_TILE)
        ka = ka_ref[pl.ds(k0, KEY_TILE), :]
        sel = jnp.where(key_ref[kt] >= thr, k0 + lane, tq + 1) <= tq
        bias = jnp.where(sel, 0.0, NEG_BIG)
        s = _dot_nt(qa, ka) + _row_tile(bias, A_HEADS)
        m_old = m_ref[...]
        m_new = jnp.maximum(m_old, jnp.max(s, axis=1, keepdims=True))
        p = jnp.exp(s - _lane_tile(m_new, KEY_TILE // LANES))
        alpha = jnp.exp(m_old - m_new)
        acc_ref[...] = acc_ref[...] * _lane_tile(alpha, QA_DIM // LANES) + _dot(p.astype(BF16), ka)
        m_ref[...] = m_new
        return carry

    lax.fori_loop(0, nt, attn_body, 0)

    tail = acc_ref[:, A_KV_RANK:QA_DIM]
    lane_r = lax.broadcasted_iota(I32, (rows, LANES), 1)
    denom = jnp.where(lane_r >= ROPE_DIM, tail, pltpu.roll(tail, 64, 1))
    inv = 1.0 / jnp.maximum(denom, 1e-30)
    o_lat = (acc_ref[:, 0:A_KV_RANK] * _lane_tile(inv, A_KV_RANK // LANES)).astype(BF16)
    for h in range(A_HEADS):
        o_ref[:, h * HEAD_DIM:(h + 1) * HEAD_DIM] = _dot(o_lat[h * qb:(h + 1) * qb], wuv_ref[h]).astype(BF16)


def _dsa_attn(qi, wi, ki, qa, ka, wuv):
    nb, _, t, _ = qa.shape
    topk = min(IDX_TOPK_MAX, t // 4)
    idx_bits = int(np.ceil(np.log2(t)))
    qb = Q_BLOCK
    body = functools.partial(_dsa_attn_body, topk=topk, idx_bits=idx_bits)
    return pl.pallas_call(
        body,
        grid=(nb, t // qb),
        in_specs=[pl.BlockSpec((None, IDX_HEADS, qb, LANES), lambda b, i: (b, 0, i, 0)),
                  pl.BlockSpec((None, IDX_HEADS, qb, LANES), lambda b, i: (b, 0, i, 0)),
                  pl.BlockSpec((None, t, LANES), lambda b, i: (b, 0, 0)),
                  pl.BlockSpec((None, A_HEADS, qb, QA_DIM), lambda b, i: (b, 0, i, 0)),
                  pl.BlockSpec((None, t, QA_DIM), lambda b, i: (b, 0, 0)),
                  pl.BlockSpec(wuv.shape, lambda b, i: (0, 0, 0))],
        out_specs=pl.BlockSpec((None, qb, A_HEADS * HEAD_DIM), lambda b, i: (b, i, 0)),
        out_shape=jax.ShapeDtypeStruct((nb, t, A_HEADS * HEAD_DIM), BF16),
        scratch_shapes=[pltpu.VMEM((t // KEY_TILE, qb, KEY_TILE), I32),
                        pltpu.VMEM((A_HEADS * qb, QA_DIM), F32),
                        pltpu.VMEM((A_HEADS * qb, LANES), F32)],
        compiler_params=_cparams(("parallel", "parallel")),
    )(qi, wi, ki, qa, ka, wuv)


def _nsa_prep_body(bq_ref, bkv_ref, tabh_ref, qb_ref, kc_ref, vc_ref, ks_ref, vs_ref, kw_ref, vw_ref):
    tm = bq_ref.shape[0]
    lane = lax.broadcasted_iota(I32, (tm, LANES), 1)
    ch, sh = tabh_ref[:, 0:LANES], tabh_ref[:, LANES:2 * LANES]

    def rope(x):
        rot = jnp.where(lane < ROPE_DIM // 2, pltpu.roll(x, LANES - ROPE_DIM // 2, 1),
                        pltpu.roll(x, ROPE_DIM // 2, 1))
        return x * ch + rot * sh

    scale = HEAD_DIM ** -0.5
    for h in range(B_HEADS):
        sl = slice(h * HEAD_DIM, (h + 1) * HEAD_DIM)
        qb_ref[:, sl] = (rope(bq_ref[:, sl]) * scale).astype(BF16)
    outs = (kc_ref, vc_ref, ks_ref, vs_ref, kw_ref, vw_ref)
    for kind in range(6):
        for g in range(B_GROUPS):
            c0 = (kind * B_GROUPS + g) * HEAD_DIM
            v = bkv_ref[:, c0:c0 + HEAD_DIM]
            if kind % 2 == 0:
                v = rope(v)
            outs[kind][g] = v.astype(BF16)


def _nsa_prep(z, tabh):
    nb, t, _ = z.shape
    tm = 256
    kv_spec = pl.BlockSpec((None, B_GROUPS, tm, HEAD_DIM), lambda b, i: (b, 0, i, 0))
    kv_shape = jax.ShapeDtypeStruct((nb, B_GROUPS, t, HEAD_DIM), BF16)
    return pl.pallas_call(
        _nsa_prep_body,
        grid=(nb, t // tm),
        in_specs=[pl.BlockSpec((None, tm, 1024), lambda b, i: (b, i, Z_BQ // 1024)),
                  pl.BlockSpec((None, tm, 1536), lambda b, i: (b, i, Z_BKV // 1536)),
                  pl.BlockSpec((None, tm, 256), lambda b, i: (b, i, 0))],
        out_specs=[pl.BlockSpec((None, tm, 1024), lambda b, i: (b, i, 0))] + [kv_spec] * 6,
        out_shape=[jax.ShapeDtypeStruct((nb, t, 1024), BF16)] + [kv_shape] * 6,
        compiler_params=_cparams(("parallel", "parallel")),
    )(z, z, tabh)


def _compress_body(ch_ref, pe_ref, w1_ref, w2_ref, o_ref):
    nc = ch_ref.shape[0]
    half = CMP_STRIDE * HEAD_DIM
    x = ch_ref[...]
    ha = _dot(x, w1_ref[0:half, :])
    hb = _dot(x, w1_ref[half:2 * half, :])
    bias = _dot(pe_ref[...], w1_ref[...])[0:1]
    pre = ha + pltpu.roll(hb, nc - 1, 0) + bias
    hid = jax.nn.gelu(pre, approximate=True)
    o_ref[...] = _dot(hid.astype(BF16), w2_ref[...]).astype(BF16)


def _compress(tok, pe, w1, w2):
    nb, ng, t, d = tok.shape
    nc = t // CMP_STRIDE
    chunks = tok.reshape(nb, ng, nc, CMP_STRIDE * d)
    pe8 = jnp.broadcast_to(pe.reshape(1, CMP_LEN * d), (8, CMP_LEN * d)).astype(BF16)
    return pl.pallas_call(
        _compress_body,
        grid=(nb, ng),
        in_specs=[pl.BlockSpec((None, None, nc, CMP_STRIDE * d), lambda b, g: (b, g, 0, 0)),
                  pl.BlockSpec(pe8.shape, lambda b, g: (0, 0)),
                  pl.BlockSpec(w1.shape, lambda b, g: (0, 0)),
                  pl.BlockSpec(w2.shape, lambda b, g: (0, 0))],
        out_specs=pl.BlockSpec((None, None, nc, d), lambda b, g: (b, g, 0, 0)),
        out_shape=jax.ShapeDtypeStruct((nb, ng, nc, d), BF16),
        compiler_params=_cparams(("parallel", "parallel")),
    )(chunks, pe8, w1.astype(BF16), w2.astype(BF16))


def _nsa_attn_body(q_ref, kc_ref, vc_ref, ks_ref, vs_ref, kw_ref, vw_ref, gate_ref, ov_ref, ex_ref,
                   o_ref, key_ref, acc_ref, m_ref, l_ref, *, n_cmp, n_sel):
    qb = Q_BLOCK
    rows = B_HPG * qb
    nc = kc_ref.shape[0]
    q0 = pl.program_id(2) * qb
    qs = jnp.concatenate([q_ref[:, j * HEAD_DIM:(j + 1) * HEAD_DIM] for j in range(B_HPG)], axis=0)

    tq_c = q0 + lax.broadcasted_iota(I32, (qb, nc), 0)
    n_id = lax.broadcasted_iota(I32, (qb, nc), 1)
    c_ok = jnp.where(n_id < n_cmp, n_id * CMP_STRIDE + (CMP_LEN - 1), tq_c + 1) <= tq_c
    c_bias = _row_tile(jnp.where(c_ok, 0.0, NEG_BIG), B_HPG)
    c_keep = _row_tile(jnp.where(c_ok, 1.0, 0.0), B_HPG)
    sc = _dot_nt(qs, kc_ref[...]) + c_bias
    e = jnp.exp(sc - jnp.max(sc, axis=1, keepdims=True)) * c_keep
    pc = e / jnp.maximum(jnp.sum(e, axis=1, keepdims=True), 1e-30)
    oc = _dot(pc.astype(BF16), vc_ref[...])

    pc4 = pc[0:qb]
    for j in range(1, B_HPG):
        pc4 = pc4 + pc[j * qb:(j + 1) * qb]
    p_hi, p_mid, p_lo = _split3(pc4)
    ov = ov_ref[...]
    imp = _dot(p_hi, ov) + _dot(p_mid, ov) + _dot(p_lo, ov)
    tq_b = q0 + lax.broadcasted_iota(I32, (qb, LANES), 0)
    blk = lax.broadcasted_iota(I32, (qb, LANES), 1)
    cur = tq_b // SLC_LEN
    d_cur = cur - blk
    forced = jnp.where(blk == 0, 1.0, jnp.where(d_cur == 0, 1.0, jnp.where(d_cur == 1, 1.0, 0.0)))
    imp = jnp.where(forced > 0.5, jnp.inf, imp + 0.0)
    imp = jnp.where(blk * SLC_LEN <= tq_b, imp, -jnp.inf)
    key_ref[0] = _f32_key(imp)
    thr = _kth_key(key_ref, 1, n_sel)
    _break_ties(key_ref, 1, thr, n_sel, 7)
    selm = jnp.where(key_ref[0] >= thr, 1.0, 0.0).astype(BF16)

    nt = (q0 + qb + KEY_TILE - 1) // KEY_TILE
    tq_k = q0 + lax.broadcasted_iota(I32, (qb, KEY_TILE), 0)
    lane_k = lax.broadcasted_iota(I32, (qb, KEY_TILE), 1)
    m_ref[...] = jnp.full((rows, LANES), NEG_BIG, F32)
    l_ref[...] = jnp.zeros((rows, LANES), F32)
    acc_ref[...] = jnp.zeros((rows, HEAD_DIM), F32)

    def sel_body(kt, carry):
        k0 = pl.multiple_of(kt * KEY_TILE, KEY_TILE)
        chosen = _dot(selm, ex_ref[kt])
        ok = jnp.where(chosen > 0.5, k0 + lane_k, tq_k + 1) <= tq_k
        bias = _row_tile(jnp.where(ok, 0.0, NEG_BIG), B_HPG)
        s = _dot_nt(qs, ks_ref[pl.ds(k0, KEY_TILE), :]) + bias
        m_old = m_ref[...]
        m_new = jnp.maximum(m_old, jnp.max(s, axis=1, keepdims=True))
        p = jnp.exp(s - _lane_tile(m_new, KEY_TILE // LANES))
        alpha = jnp.exp(m_old - m_new)
        l_ref[...] = l_ref[...] * alpha + jnp.sum(p, axis=1, keepdims=True)
        acc_ref[...] = acc_ref[...] * alpha + _dot(p.astype(BF16), vs_ref[pl.ds(k0, KEY_TILE), :])
        m_ref[...] = m_new
        return carry

    lax.fori_loop(0, nt, sel_body, 0)
    os_ = acc_ref[...] / jnp.maximum(l_ref[...], 1e-30)

    span = WINDOW + qb
    start = pl.multiple_of(jnp.maximum(q0 - WINDOW, 0), qb)
    tq_w = q0 + lax.broadcasted_iota(I32, (qb, span), 0)
    wpos = start + lax.broadcasted_iota(I32, (qb, span), 1)
    w_ok = jnp.where(wpos > tq_w - WINDOW, wpos, tq_w + 1) <= tq_w
    w_bias = _row_tile(jnp.where(w_ok, 0.0, NEG_BIG), B_HPG)
    sw = _dot_nt(qs, kw_ref[pl.ds(start, span), :]) + w_bias
    ew = jnp.exp(sw - jnp.max(sw, axis=1, keepdims=True))
    pw = ew / jnp.maximum(jnp.sum(ew, axis=1, keepdims=True), 1e-30)
    ow = _dot(pw.astype(BF16), vw_ref[pl.ds(start, span), :])

    gate = gate_ref[...]
    for j in range(B_HPG):
        r = slice(j * qb, (j + 1) * qb)
        res = (gate[:, 3 * j:3 * j + 1] * oc[r] + gate[:, 3 * j + 1:3 * j + 2] * os_[r]
               + gate[:, 3 * j + 2:3 * j + 3] * ow[r])
        o_ref[:, j * HEAD_DIM:(j + 1) * HEAD_DIM] = res.astype(BF16)


def _nsa_attn(qb_all, kc, vc, ks, vs, kw, vw, gates, ov, ex):
    nb, t, _ = qb_all.shape
    nc = kc.shape[2]
    n_cmp = (t - CMP_LEN) // CMP_STRIDE + 1
    n_sel = min(SLC_TOPN, t // SLC_LEN)
    qb = Q_BLOCK
    body = functools.partial(_nsa_attn_body, n_cmp=n_cmp, n_sel=n_sel)
    res_c = pl.BlockSpec((None, None, nc, HEAD_DIM), lambda b, g, i: (b, g, 0, 0))
    res_t = pl.BlockSpec((None, None, t, HEAD_DIM), lambda b, g, i: (b, g, 0, 0))
    return pl.pallas_call(
        body,
        grid=(nb, B_GROUPS, t // qb),
        in_specs=[pl.BlockSpec((None, qb, B_HPG * HEAD_DIM), lambda b, g, i: (b, i, g)),
                  res_c, res_c, res_t, res_t, res_t, res_t,
                  pl.BlockSpec((None, None, qb, 3 * B_HPG), lambda b, g, i: (b, g, i, 0)),
                  pl.BlockSpec(ov.shape, lambda b, g, i: (0, 0)),
                  pl.BlockSpec(ex.shape, lambda b, g, i: (0, 0, 0))],
        out_specs=pl.BlockSpec((None, qb, B_HPG * HEAD_DIM), lambda b, g, i: (b, i, g)),
        out_shape=jax.ShapeDtypeStruct((nb, t, B_HEADS * HEAD_DIM), BF16),
        scratch_shapes=[pltpu.VMEM((1, qb, LANES), I32),
                        pltpu.VMEM((B_HPG * qb, HEAD_DIM), F32),
                        pltpu.VMEM((B_HPG * qb, LANES), F32),
                        pltpu.VMEM((B_HPG * qb, LANES), F32)],
        compiler_params=_cparams(("parallel", "parallel", "parallel")),
    )(qb_all, kc, vc, ks, vs, kw, vw, gates, ov, ex)


def _merge_body(oa_ref, ob_ref, wa_ref, wb_ref, ga_ref, gb_ref, o_ref):
    a = _dot(oa_ref[...], wa_ref[...])
    b = _dot(ob_ref[...], wb_ref[...])
    o_ref[...] = (jax.nn.sigmoid(ga_ref[...]) * a + jax.nn.sigmoid(gb_ref[...]) * b).astype(BF16)


def _merge(o_a, o_b, w_a, w_b, z):
    nb, t, da = o_a.shape
    d = w_a.shape[1]
    tm, tn = 512, 1024
    return pl.pallas_call(
        _merge_body,
        grid=(nb, t // tm, d // tn),
        in_specs=[pl.BlockSpec((None, tm, da), lambda b, i, j: (b, i, 0)),
                  pl.BlockSpec((None, tm, da), lambda b, i, j: (b, i, 0)),
                  pl.BlockSpec((da, tn), lambda b, i, j: (0, j)),
                  pl.BlockSpec((da, tn), lambda b, i, j: (0, j)),
                  pl.BlockSpec((None, tm, tn), lambda b, i, j: (b, i, j)),
                  pl.BlockSpec((None, tm, tn), lambda b, i, j: (b, i, d // tn + j))],
        out_specs=pl.BlockSpec((None, tm, tn), lambda b, i, j: (b, i, j)),
        out_shape=jax.ShapeDtypeStruct((nb, t, d), BF16),
        compiler_params=_cparams(("parallel", "parallel", "parallel")),
    )(o_a, o_b, w_a, w_b, z, z)


def _outproj_body(mg_ref, wo_ref, x_ref, gpost_ref, gt_ref, gpre_ref, sc_ref, sh_ref,
                  wr_ref, br_ref, x1_ref, h_ref, idx_ref, gate_ref):
    tm = x_ref.shape[0]
    y = _dot(mg_ref[...], wo_ref[...])
    x1 = x_ref[...] + gt_ref[...] * _rms(y, gpost_ref[...])
    x1_ref[...] = x1
    h = _rms(x1, gpre_ref[...]) * (1.0 + sc_ref[...]) + sh_ref[...]
    h_hi = h.astype(BF16)
    h_ref[...] = h_hi
    h_lo = (h - h_hi.astype(F32)).astype(BF16)
    logits = (_dot(h_hi, wr_ref[0]) + (_dot(h_hi, wr_ref[1]) + _dot(h_lo, wr_ref[0]))) + br_ref[...]
    lane = lax.broadcasted_iota(I32, (tm, N_EXPERTS), 1).astype(F32)
    vals, idxs = [], []
    cur = logits
    for _ in range(TOP_K):
        m = jnp.max(cur, axis=1, keepdims=True)
        am = jnp.min(jnp.where(cur == m, lane, float(N_EXPERTS)), axis=1, keepdims=True)
        vals.append(m)
        idxs.append(am)
        cur = jnp.where(lane == am, -jnp.inf, cur)
    es = [jnp.exp(v - vals[0]) for v in vals]
    tot = es[0] + es[1] + es[2] + es[3]
    for k in range(TOP_K):
        idx_ref[:, k:k + 1] = idxs[k].astype(I32)
        gate_ref[:, k:k + 1] = es[k] / tot


def _outproj(merged, w_out, x, g_post, gt1, g_pre, sc2, sh2, wr, br):
    nb, t, d = x.shape
    tm = 256
    vec = lambda: pl.BlockSpec((1, d), lambda b, i: (0, 0))
    mod = lambda: pl.BlockSpec((None, 1, d), lambda b, i: (b, 0, 0))
    row = lambda w: pl.BlockSpec((None, tm, w), lambda b, i: (b, i, 0))
    return pl.pallas_call(
        _outproj_body,
        grid=(nb, t // tm),
        in_specs=[row(d), pl.BlockSpec((d, d), lambda b, i: (0, 0)), row(d), vec(), mod(), vec(), mod(), mod(),
                  pl.BlockSpec(wr.shape, lambda b, i: (0, 0, 0)),
                  pl.BlockSpec((1, N_EXPERTS), lambda b, i: (0, 0))],
        out_specs=[row(d), row(d), row(TOP_K), row(TOP_K)],
        out_shape=[jax.ShapeDtypeStruct((nb, t, d), F32),
                   jax.ShapeDtypeStruct((nb, t, d), BF16),
                   jax.ShapeDtypeStruct((nb, t, TOP_K), I32),
                   jax.ShapeDtypeStruct((nb, t, TOP_K), F32)],
        compiler_params=_cparams(("parallel", "parallel")),
    )(merged, w_out, x, g_post[None], gt1, g_pre[None], sc2, sh2, wr, br[None])


def _gather_body(tok_ref, h_ref, o_ref, sem):
    base = pl.program_id(0) * DMA_ROWS

    def copy(r):
        return pltpu.make_async_copy(h_ref.at[tok_ref[0, 0, r]], o_ref.at[base + r], sem)

    def issue(r, carry):
        copy(r).start()
        return carry

    def wait(r, carry):
        copy(r).wait()
        return carry

    lax.fori_loop(0, DMA_ROWS, issue, 0)
    lax.fori_loop(0, DMA_ROWS, wait, 0)


def _gather_rows(h3, row_tok):
    p = row_tok.shape[0]
    steps = p // DMA_ROWS
    return pl.pallas_call(
        _gather_body,
        grid=(steps,),
        in_specs=[pl.BlockSpec((1, 1, DMA_ROWS), lambda i: (i, 0, 0), memory_space=pltpu.SMEM),
                  pl.BlockSpec(memory_space=pl.ANY)],
        out_specs=pl.BlockSpec(memory_space=pl.ANY),
        out_shape=jax.ShapeDtypeStruct((p,) + h3.shape[1:], h3.dtype),
        scratch_shapes=[pltpu.SemaphoreType.DMA(())],
        compiler_params=_cparams(("arbitrary",)),
    )(row_tok.reshape(steps, 1, DMA_ROWS), h3)


def _gmm_body(be_ref, x_ref, wg_ref, wu_ref, bg_ref, bu_ref, wd_ref, bd_ref, rw_ref, y_ref, acc_ref):
    j = pl.program_id(1)
    x = x_ref[...]
    g = jnp.minimum(_dot(x, wg_ref[...]) + bg_ref[...], SWIGLU_LIMIT)
    u = jnp.clip(_dot(x, wu_ref[...]) + bu_ref[...], -SWIGLU_LIMIT, SWIGLU_LIMIT)
    act = g * jax.nn.sigmoid(SWIGLU_ALPHA * g) * (u + 1.0)
    part = _dot(act.astype(BF16), wd_ref[...])

    @pl.when(j == 0)
    def _():
        acc_ref[...] = part

    @pl.when(j > 0)
    def _():
        acc_ref[...] = acc_ref[...] + part

    @pl.when(j == pl.num_programs(1) - 1)
    def _():
        y = acc_ref[...] + bd_ref[...]
        y_ref[...] = (y * _lane_tile(rw_ref[...], y.shape[1] // LANES)).astype(BF16)


def _gmm(blk_e, xs, wgu, bgu, wd, bd, row_w):
    p, d = xs.shape
    ff = wd.shape[1]
    nblk = p // ROW_BLK
    nft = ff // FF_TILE
    grid_spec = pltpu.PrefetchScalarGridSpec(
        num_scalar_prefetch=1,
        grid=(nblk, nft),
        in_specs=[pl.BlockSpec((ROW_BLK, d), lambda i, j, be: (i, 0)),
                  pl.BlockSpec((None, d, FF_TILE), lambda i, j, be: (be[i], 0, j)),
                  pl.BlockSpec((None, d, FF_TILE), lambda i, j, be: (be[i], 0, nft + j)),
                  pl.BlockSpec((None, 1, FF_TILE), lambda i, j, be: (be[i], 0, j)),
                  pl.BlockSpec((None, 1, FF_TILE), lambda i, j, be: (be[i], 0, nft + j)),
                  pl.BlockSpec((None, FF_TILE, d), lambda i, j, be: (be[i], j, 0)),
                  pl.BlockSpec((None, 1, d), lambda i, j, be: (be[i], 0, 0)),
                  pl.BlockSpec((ROW_BLK, LANES), lambda i, j, be: (i, 0))],
        out_specs=pl.BlockSpec((ROW_BLK, d), lambda i, j, be: (i, 0)),
        scratch_shapes=[pltpu.VMEM((ROW_BLK, d), F32)],
    )
    return pl.pallas_call(
        _gmm_body,
        grid_spec=grid_spec,
        out_shape=jax.ShapeDtypeStruct((p, d), BF16),
        compiler_params=_cparams(("parallel", "arbitrary")),
    )(blk_e, xs, wgu, wgu, bgu, bgu, wd, bd, row_w)


def _combine_body(dest_ref, ys_ref, x1_ref, g_ref, gt_ref, o_ref, buf, sem):
    tm = CMB_TOK

    def copy(r):
        return pltpu.make_async_copy(ys_ref.at[dest_ref[0, 0, r]], buf.at[r], sem)

    def issue(r, carry):
        copy(r).start()
        return carry

    def wait(r, carry):
        copy(r).wait()
        return carry

    lax.fori_loop(0, TOP_K * tm, issue, 0)
    lax.fori_loop(0, TOP_K * tm, wait, 0)
    y = buf[0:tm].astype(F32)
    for k in range(1, TOP_K):
        y = y + buf[k * tm:(k + 1) * tm].astype(F32)
    d = y.shape[1] * y.shape[2]
    ms = jnp.sum(jnp.sum(y * y, axis=2, keepdims=True), axis=1, keepdims=True) * (1.0 / d)
    o_ref[...] = x1_ref[...] + gt_ref[...] * (y * lax.rsqrt(ms + NORM_EPS) * g_ref[...])


def _combine(dest, ys3, x1, g_post, gt2):
    nb, t, d = x1.shape
    n = nb * t
    sub = d // LANES
    tm = CMB_TOK
    steps = n // tm
    dest_t = dest.reshape(steps, tm, TOP_K).transpose(0, 2, 1).reshape(steps, 1, TOP_K * tm)
    out = pl.pallas_call(
        _combine_body,
        grid=(steps,),
        in_specs=[pl.BlockSpec((1, 1, TOP_K * tm), lambda i: (i, 0, 0), memory_space=pltpu.SMEM),
                  pl.BlockSpec(memory_space=pl.ANY),
                  pl.BlockSpec((tm, sub, LANES), lambda i: (i, 0, 0)),
                  pl.BlockSpec((sub, LANES), lambda i: (0, 0)),
                  pl.BlockSpec((None, sub, LANES), lambda i: ((i * tm) // t, 0, 0))],
        out_specs=pl.BlockSpec((tm, sub, LANES), lambda i: (i, 0, 0)),
        out_shape=jax.ShapeDtypeStruct((n, sub, LANES), F32),
        scratch_shapes=[pltpu.VMEM((TOP_K * tm, sub, LANES), BF16), pltpu.SemaphoreType.DMA(())],
        compiler_params=_cparams(("arbitrary",)),
    )(dest_t, ys3, x1.reshape(n, sub, LANES), g_post.reshape(sub, LANES), gt2.reshape(nb, sub, LANES))
    return out.reshape(nb, t, d)


def _route(top_idx, gates):
    n = top_idx.shape[0]
    nk = n * TOP_K
    e = top_idx.reshape(nk)
    onehot = (e[:, None] == jnp.arange(N_EXPERTS, dtype=I32)[None, :]).astype(I32)
    csum = jnp.cumsum(onehot, axis=0)
    rank = jnp.sum((csum - onehot) * onehot, axis=1)
    counts = csum[-1]
    padded = (counts + ROW_BLK - 1) // ROW_BLK * ROW_BLK
    pend = jnp.cumsum(padded)
    pstart = pend - padded
    dest = (pstart[e] + rank).astype(I32)
    nblk = nk // ROW_BLK + N_EXPERTS
    p = nblk * ROW_BLK
    tok = jnp.arange(nk, dtype=I32) // TOP_K
    row_tok = jnp.zeros((p,), I32).at[dest].set(tok)
    row_w = jnp.zeros((p,), F32).at[dest].set(gates.reshape(nk))
    blk_e = jnp.minimum(jnp.searchsorted(pend, jnp.arange(nblk, dtype=I32) * ROW_BLK, side="right"),
                        N_EXPERTS - 1).astype(I32)
    return dest.reshape(n, TOP_K), row_tok, row_w, blk_e


def _rot_cols(w, half):
    z = jnp.zeros_like(w)
    z = z.at[..., 0:half].set(w[..., half:2 * half])
    return z.at[..., half:2 * half].set(w[..., 0:half])


def _prep_w_in(w_in):
    d = w_in.shape[0]
    offs = np.cumsum([0, A_Q_RANK, A_KV_RANK, ROPE_DIM, IDX_DIM, IDX_HEADS, B_HEADS * HEAD_DIM,
                      6 * B_GROUPS * HEAD_DIM, 3 * B_HEADS, 2 * d])
    cq, ckv, krope, kidx, widx, bq, bkv, bgate, mgate = [w_in[:, int(offs[i]):int(offs[i + 1])] for i in range(9)]
    krope_rot = jnp.concatenate([krope[:, ROPE_DIM // 2:], krope[:, :ROPE_DIM // 2]], axis=1)
    small = jnp.concatenate([krope, kidx, krope_rot, widx, bgate,
                             jnp.zeros((d, 256 - 160), w_in.dtype)], axis=1)
    w = jnp.concatenate([mgate, bq, cq, ckv, small, bkv], axis=1)
    assert w.shape[1] == Z_COLS
    return w.astype(BF16)


def _rope_tables(positions):
    def tab(rot_dim, lo, width):
        inv = jnp.power(ROPE_THETA, -jnp.arange(0, rot_dim, 2, dtype=F32) / rot_dim)
        ang = positions.astype(F32)[..., None] * inv
        c, s = jnp.cos(ang), jnp.sin(ang)
        shape = positions.shape
        cos_t = jnp.concatenate([jnp.zeros(shape + (lo,), F32), c, c,
                                 jnp.ones(shape + (width - rot_dim,), F32),
                                 jnp.zeros(shape + (LANES - lo - width,), F32)], axis=-1)
        sin_t = jnp.concatenate([jnp.zeros(shape + (lo,), F32), -s, s,
                                 jnp.zeros(shape + (LANES - lo - rot_dim,), F32)], axis=-1)
        return jnp.concatenate([cos_t, sin_t], axis=-1)

    return tab(ROPE_DIM, 0, LANES), tab(IDX_ROPE, 32, IDX_DIM)


def _select_tables(t):
    n_cmp = (t - CMP_LEN) // CMP_STRIDE + 1
    nc = t // CMP_STRIDE
    n = np.arange(nc)[:, None]
    m = np.arange(LANES)[None, :]
    ov = (n * CMP_STRIDE < m * SLC_LEN + SLC_LEN) & (n * CMP_STRIDE + CMP_LEN - 1 >= m * SLC_LEN) & (n < n_cmp)
    key_blk = (np.arange(t) // SLC_LEN).reshape(t // KEY_TILE, 1, KEY_TILE)
    ex = key_blk == np.arange(LANES).reshape(1, LANES, 1)
    return jnp.asarray(ov, BF16), jnp.asarray(ex, BF16)


def kernel(x, c, positions, w_ada, b_ada, g_pre_mix, g_post_mix, g_pre_ffn, g_post_ffn, w_in, a_cq_norm, a_ckv_norm, a_w_uq, a_w_uk, a_w_uv, idx_w_q, idx_k_norm_g, idx_k_norm_b, cmp_k_pe, cmp_k_w1, cmp_k_w2, cmp_v_pe, cmp_v_w1, cmp_v_w2, w_br_a, w_br_b, w_out, w_router, b_router, w_gate_up, b_gate_up, w_down, b_down):
    nb, t, d = x.shape
    n = nb * t
    assert t % KEY_TILE == 0 and t // SLC_LEN <= LANES and d % LANES == 0

    mod = _adaln(c, w_ada, b_ada)
    sh1, sc1, gt1, sh2, sc2, gt2 = [m[:, None, :] for m in jnp.split(mod, 6, axis=-1)]

    z = _inproj(x, g_pre_mix, sc1, sh1, _prep_w_in(w_in))
    tabh, tabi = _rope_tables(positions)

    wuq = jnp.transpose(a_w_uq, (1, 0, 2))
    wuk = jnp.transpose(a_w_uk, (1, 2, 0))
    wuk = jnp.concatenate([jnp.zeros((A_HEADS, ROPE_DIM, A_KV_RANK), wuk.dtype), wuk], axis=1)
    widx = jnp.transpose(idx_w_q, (1, 0, 2))
    pad_idx = lambda w: jnp.pad(w, ((0, 0), (0, 0), (32, LANES - 32 - IDX_DIM)))
    pad_vec = lambda v: jnp.pad(v, (32, LANES - 32 - IDX_DIM))[None]
    qa, ka, qi, ki, wi, sg = _dsa_prep(
        z, tabh, tabi, a_cq_norm[None], a_ckv_norm[None],
        wuq.astype(BF16), _rot_cols(wuq, ROPE_DIM // 2).astype(BF16), wuk.astype(BF16),
        pad_idx(widx).astype(BF16), pad_idx(_rot_cols(widx, IDX_ROPE // 2)).astype(BF16),
        pad_vec(idx_k_norm_g), pad_vec(idx_k_norm_b))
    o_a = _dsa_attn(qi, wi, ki, qa, ka, a_w_uv.astype(BF16))

    qb_all, kc_tok, vc_tok, ks, vs, kw, vw = _nsa_prep(z, tabh)
    kc = _compress(kc_tok, cmp_k_pe, cmp_k_w1, cmp_k_w2)
    vc = _compress(vc_tok, cmp_v_pe, cmp_v_w1, cmp_v_w2)
    gates_b = sg[:, :, IDX_HEADS:IDX_HEADS + 3 * B_HEADS].reshape(nb, t, B_GROUPS, 3 * B_HPG).transpose(0, 2, 1, 3)
    ov, ex = _select_tables(t)
    o_b = _nsa_attn(qb_all, kc, vc, ks, vs, kw, vw, gates_b, ov, ex)

    merged = _merge(o_a, o_b, w_br_a.astype(BF16), w_br_b.astype(BF16), z)

    wr_hi = w_router.astype(BF16)
    wr = jnp.stack([wr_hi, (w_router - wr_hi.astype(F32)).astype(BF16)])
    x1, h2, top_idx, gates = _outproj(merged, w_out.astype(BF16), x, g_post_mix, gt1, g_pre_ffn, sc2, sh2,
                                      wr, b_router)
    dest, row_tok, row_w, blk_e = _route(top_idx.reshape(n, TOP_K), gates.reshape(n, TOP_K))
    sub = d // LANES
    xs = _gather_rows(h2.reshape(n, sub, LANES), row_tok)
    p = xs.shape[0]
    ys = _gmm(blk_e, xs.reshape(p, d), w_gate_up.astype(BF16), b_gate_up[:, None, :],
              w_down.astype(BF16), b_down[:, None, :],
              jnp.broadcast_to(row_w[:, None], (p, LANES)))
    return _combine(dest, ys.reshape(p, sub, LANES), x1, g_post_ffn, gt2)
```

```python
import functools

import numpy as np
import jax
import jax.numpy as jnp
from jax import lax
from jax.experimental import pallas as pl
from jax.experimental.pallas import tpu as pltpu

F32 = jnp.float32
BF16 = jnp.bfloat16
I32 = jnp.int32

HEAD_DIM = 128
ROPE_DIM = 32
ROPE_THETA = 500000.0
NORM_EPS = 1e-6
Q_BLOCK = 128
A_HEADS = 8
A_Q_RANK = 512
A_KV_RANK = 256
IDX_HEADS = 8
IDX_DIM = 64
IDX_ROPE = 16
IDX_TOPK_MAX = 256
B_HEADS = 8
B_GROUPS = 2
B_HPG = 4
CMP_LEN = 32
CMP_STRIDE = 16
CMP_HIDDEN = 256
SLC_LEN = 64
SLC_TOPN = 16
WINDOW = 512
N_EXPERTS = 32
TOP_K = 4
SWIGLU_ALPHA = 1.702
SWIGLU_LIMIT = 7.0

LANES = 128
VMEM_LIMIT = 52 * 1024 * 1024

KEY_TILE = 512
QA_DIM = A_KV_RANK + LANES
ROW_BLK = 512
FF_TILE = 512
CMB_TOK = 128

Z_MGATE, Z_BQ, Z_CQ, Z_CKV, Z_SMALL, Z_BKV, Z_COLS = 0, 4096, 5120, 5632, 5888, 6144, 7680

NEG_BIG = -1e30
INT_MIN = np.int32(-2147483648)
KEY_NEG_INF = np.int32(-2139095041)
BISECT_MAX_ITER = 24


def _cparams(sem):
    return pltpu.CompilerParams(dimension_semantics=sem, vmem_limit_bytes=VMEM_LIMIT)


def _dot(a, b):
    return jnp.dot(a, b, preferred_element_type=F32)


def _dot_nt(a, b):
    return lax.dot_general(a, b, (((1,), (1,)), ((), ())), preferred_element_type=F32)


def _split3(a):
    hi = a.astype(BF16)
    r1 = a - hi.astype(F32)
    mid = r1.astype(BF16)
    lo = (r1 - mid.astype(F32)).astype(BF16)
    return hi, mid, lo


def _lane_tile(a, n):
    return jnp.concatenate([a] * n, axis=1) if n > 1 else a


def _row_tile(a, n):
    return jnp.concatenate([a] * n, axis=0) if n > 1 else a


def _f32_key(s):
    bits = pltpu.bitcast(s, I32)
    return bits ^ (jnp.right_shift(bits, 31) & np.int32(0x7FFFFFFF))


def _count(key_ref, nt, pred):
    rows, width = key_ref.shape[1], key_ref.shape[2]

    def body(kt, acc):
        ind = jnp.where(pred(key_ref[kt], kt), 1.0, 0.0)
        part = ind[:, 0:LANES]
        for c in range(1, width // LANES):
            part = part + ind[:, c * LANES:(c + 1) * LANES]
        return acc + part

    acc = lax.fori_loop(0, nt, body, jnp.zeros((rows, LANES), F32))
    return jnp.sum(acc, axis=1, keepdims=True)


def _kth_key(key_ref, nt, k):
    rows = key_ref.shape[1]

    def bit_body(b, ans):
        cand = ans | jnp.left_shift(jnp.int32(1), 31 - b)
        cs = cand ^ INT_MIN
        cnt = _count(key_ref, nt, lambda kk, kt: kk >= cs)
        return jnp.where(cnt >= k, cand, ans)

    ans = lax.fori_loop(0, 32, bit_body, jnp.zeros((rows, 1), I32))
    return ans ^ INT_MIN


def _break_ties(key_ref, nt, t, k, idx_bits):
    rows, width = key_ref.shape[1], key_ref.shape[2]
    cnt_gt = _count(key_ref, nt, lambda kk, kt: kk > t)
    cnt_ge = _count(key_ref, nt, lambda kk, kt: kk >= t)
    need = k - cnt_gt

    @pl.when(jnp.max(cnt_ge) > k)
    def _():
        lane = lax.broadcasted_iota(I32, (rows, width), 1)

        def bit_body(b, c0):
            cand = c0 | jnp.left_shift(jnp.int32(1), idx_bits - 1 - b)
            f = _count(key_ref, nt, lambda kk, kt: jnp.where(kk == t, kt * width + lane, cand) < cand)
            return jnp.where(f < need, cand, c0)

        c0 = lax.fori_loop(0, idx_bits, bit_body, jnp.zeros((rows, 1), I32))

        def fix(kt, carry):
            kk = key_ref[kt]
            idx = kt * width + lane
            key_ref[kt] = jnp.where(jnp.where(kk == t, idx, c0) > c0, t - 1, kk)
            return carry

        lax.fori_loop(0, nt, fix, 0)


def _topk_select(key_ref, thr_ref, nt, k, idx_bits, lo0, hi0):
    rows = key_ref.shape[1]
    n_fin = _count(key_ref, nt, lambda kk, kt: kk > KEY_NEG_INF)
    done0 = jnp.where(n_fin <= k, 1.0, 0.0)
    thr0 = jnp.full((rows, 1), KEY_NEG_INF, I32)

    def cond(st):
        it, _, _, _, done, stuck = st
        return jnp.logical_and(it < BISECT_MAX_ITER, jnp.min(jnp.maximum(done, stuck)) < 0.5)

    def body(st):
        it, lo, hi, thr, done, stuck = st
        mid = 0.5 * lo + 0.5 * hi
        midk = _f32_key(mid)
        cnt = _count(key_ref, nt, lambda kk, kt: kk > midk)
        hit = jnp.where(cnt == k, 1.0 - done, 0.0)
        thr = jnp.where(hit > 0.5, midk, thr)
        done = jnp.maximum(done, hit)
        stuck = jnp.maximum(stuck, jnp.where(mid <= lo, 1.0, jnp.where(mid >= hi, 1.0, 0.0)))
        above = cnt >= k
        return it + 1, jnp.where(above, mid, lo), jnp.where(above, hi, mid), thr, done, stuck

    init = (jnp.int32(0), lo0, hi0, thr0, done0, jnp.zeros((rows, 1), F32))
    _, _, _, thr, done, _ = lax.while_loop(cond, body, init)
    thr_ref[...] = jnp.broadcast_to(thr, thr_ref.shape)

    @pl.when(jnp.min(done) < 0.5)
    def _():
        t = _kth_key(key_ref, nt, k)
        _break_ties(key_ref, nt, t, k, idx_bits)
        thr_ref[...] = jnp.broadcast_to(t - 1, thr_ref.shape)


def _ada_body(c_ref, w_ref, b_ref, o_ref):
    c = c_ref[...]
    s = c * jax.nn.sigmoid(c)
    s_hi, s_mid, s_lo = _split3(s)
    w_hi, w_mid, w_lo = _split3(w_ref[...])
    acc = _dot(s_hi, w_hi) + (_dot(s_hi, w_mid) + _dot(s_mid, w_hi))
    acc = acc + (_dot(s_hi, w_lo) + _dot(s_mid, w_mid) + _dot(s_lo, w_hi))
    o_ref[...] = acc + b_ref[...]


def _adaln(c, w_ada, b_ada):
    nb, d = c.shape
    n = w_ada.shape[1]
    tn = 1024
    cp = jnp.zeros((8, d), F32).at[:nb].set(c)
    out = pl.pallas_call(
        _ada_body,
        grid=(n // tn,),
        in_specs=[pl.BlockSpec((8, d), lambda j: (0, 0)),
                  pl.BlockSpec((d, tn), lambda j: (0, j)),
                  pl.BlockSpec((1, tn), lambda j: (0, j))],
        out_specs=pl.BlockSpec((8, tn), lambda j: (0, j)),
        out_shape=jax.ShapeDtypeStruct((8, n), F32),
        compiler_params=_cparams(("parallel",)),
    )(cp, w_ada, b_ada[None])
    return out[:nb]


def _inproj_body(x_ref, g_ref, sc_ref, sh_ref, w_ref, z_ref, h_ref):
    @pl.when(pl.program_id(2) == 0)
    def _():
        x = x_ref[...]
        y = x * lax.rsqrt(jnp.mean(x * x, axis=-1, keepdims=True) + NORM_EPS) * g_ref[...]
        h_ref[...] = (y * (1.0 + sc_ref[...]) + sh_ref[...]).astype(BF16)

    z_ref[...] = _dot(h_ref[...], w_ref[...])


def _inproj(x, g, sc, sh, w):
    nb, t, d = x.shape
    tm, tn = 512, 1536
    return pl.pallas_call(
        _inproj_body,
        grid=(nb, t // tm, Z_COLS // tn),
        in_specs=[pl.BlockSpec((None, tm, d), lambda b, i, j: (b, i, 0)),
                  pl.BlockSpec((1, d), lambda b, i, j: (0, 0)),
                  pl.BlockSpec((None, 1, d), lambda b, i, j: (b, 0, 0)),
                  pl.BlockSpec((None, 1, d), lambda b, i, j: (b, 0, 0)),
                  pl.BlockSpec((d, tn), lambda b, i, j: (0, j))],
        out_specs=pl.BlockSpec((None, tm, tn), lambda b, i, j: (b, i, j)),
        out_shape=jax.ShapeDtypeStruct((nb, t, Z_COLS), F32),
        scratch_shapes=[pltpu.VMEM((tm, d), BF16)],
        compiler_params=_cparams(("parallel", "parallel", "arbitrary")),
    )(x, g[None], sc, sh, w)


def _rms(x, g):
    return x * lax.rsqrt(jnp.mean(x * x, axis=-1, keepdims=True) + NORM_EPS) * g


def _dsa_prep_body(cq_ref, ckv_ref, sm_ref, tabh_ref, tabi_ref, gq_ref, gkv_ref,
                   wuq_ref, wuqr_ref, wuk_ref, widx_ref, widxr_ref, lng_ref, lnb_ref,
                   qa_ref, ka_ref, qi_ref, ki_ref, wi_ref, sg_ref):
    tm = cq_ref.shape[0]
    lane = lax.broadcasted_iota(I32, (tm, LANES), 1)
    cqn = _rms(cq_ref[...], gq_ref[...]).astype(BF16)
    ckvn = _rms(ckv_ref[...], gkv_ref[...])
    s0 = sm_ref[:, 0:LANES]
    s1 = sm_ref[:, LANES:2 * LANES]
    ch, sh = tabh_ref[:, 0:LANES], tabh_ref[:, LANES:2 * LANES]
    ci, si = tabi_ref[:, 0:LANES], tabi_ref[:, LANES:2 * LANES]

    k_rope = s0 * ch + pltpu.roll(s0, 32, 1) * sh
    ka_ref[:, 0:A_KV_RANK] = ckvn.astype(BF16)
    ka_ref[:, A_KV_RANK:QA_DIM] = jnp.where(lane < ROPE_DIM, k_rope, 0.0).astype(BF16)

    inside = jnp.where(lane >= 32, jnp.where(lane < 96, 1.0, 0.0), 0.0)
    mu = jnp.sum(s0 * inside, axis=-1, keepdims=True) * (1.0 / IDX_DIM)
    xc = (s0 - mu) * inside
    var = jnp.sum(xc * xc, axis=-1, keepdims=True) * (1.0 / IDX_DIM)
    y = xc * lax.rsqrt(var + NORM_EPS) * lng_ref[...] + lnb_ref[...]
    rot = jnp.where(lane < 40, pltpu.roll(y, LANES - 8, 1), pltpu.roll(y, 8, 1))
    ki_ref[...] = (y * ci + rot * si).astype(BF16)

    scale = HEAD_DIM ** -0.5
    w_scale = IDX_HEADS ** -0.5 * IDX_DIM ** -0.5
    for h in range(A_HEADS):
        qh = _dot(cqn, wuq_ref[h])
        qr = _dot(cqn, wuqr_ref[h])
        roped = qh * ch + qr * sh
        qabs = _dot(roped.astype(BF16), wuk_ref[h])
        qa_ref[h, :, 0:A_KV_RANK] = (qabs * scale).astype(BF16)
        qa_ref[h, :, A_KV_RANK:QA_DIM] = (jnp.where(lane < ROPE_DIM, roped, 0.0) * scale).astype(BF16)
        qih = _dot(cqn, widx_ref[h])
        qir = _dot(cqn, widxr_ref[h])
        qi_ref[h] = (qih * ci + qir * si).astype(BF16)
        wi_ref[h] = jnp.broadcast_to(s1[:, h:h + 1], (tm, LANES)) * w_scale
    sg_ref[...] = jax.nn.sigmoid(s1)


def _dsa_prep(z, tabh, tabi, gq, gkv, wuq, wuqr, wuk, widx, widxr, lng, lnb):
    nb, t, _ = z.shape
    tm = 256
    full = lambda a: pl.BlockSpec(a.shape, lambda b, i: (0,) * a.ndim)
    return pl.pallas_call(
        _dsa_prep_body,
        grid=(nb, t // tm),
        in_specs=[pl.BlockSpec((None, tm, A_Q_RANK), lambda b, i: (b, i, Z_CQ // A_Q_RANK)),
                  pl.BlockSpec((None, tm, A_KV_RANK), lambda b, i: (b, i, Z_CKV // A_KV_RANK)),
                  pl.BlockSpec((None, tm, 256), lambda b, i: (b, i, Z_SMALL // 256)),
                  pl.BlockSpec((None, tm, 256), lambda b, i: (b, i, 0)),
                  pl.BlockSpec((None, tm, 256), lambda b, i: (b, i, 0)),
                  full(gq), full(gkv), full(wuq), full(wuqr), full(wuk), full(widx), full(widxr),
                  full(lng), full(lnb)],
        out_specs=[pl.BlockSpec((None, A_HEADS, tm, QA_DIM), lambda b, i: (b, 0, i, 0)),
                   pl.BlockSpec((None, tm, QA_DIM), lambda b, i: (b, i, 0)),
                   pl.BlockSpec((None, IDX_HEADS, tm, LANES), lambda b, i: (b, 0, i, 0)),
                   pl.BlockSpec((None, tm, LANES), lambda b, i: (b, i, 0)),
                   pl.BlockSpec((None, IDX_HEADS, tm, LANES), lambda b, i: (b, 0, i, 0)),
                   pl.BlockSpec((None, tm, LANES), lambda b, i: (b, i, 0))],
        out_shape=[jax.ShapeDtypeStruct((nb, A_HEADS, t, QA_DIM), BF16),
                   jax.ShapeDtypeStruct((nb, t, QA_DIM), BF16),
                   jax.ShapeDtypeStruct((nb, IDX_HEADS, t, LANES), BF16),
                   jax.ShapeDtypeStruct((nb, t, LANES), BF16),
                   jax.ShapeDtypeStruct((nb, IDX_HEADS, t, LANES), F32),
                   jax.ShapeDtypeStruct((nb, t, LANES), F32)],
        compiler_params=_cparams(("parallel", "parallel")),
    )(z, z, z, tabh, tabi, gq, gkv, wuq, wuqr, wuk, widx, widxr, lng, lnb)


def _fold_lanes(x, op):
    out = x[:, 0:LANES]
    for c in range(1, x.shape[1] // LANES):
        out = op(out, x[:, c * LANES:(c + 1) * LANES])
    return out


def _dsa_attn_body(qi_ref, wi_ref, ki_ref, qa_ref, ka_ref, wuv_ref, o_ref,
                   key_ref, thr_ref, acc_ref, m_ref, l_ref, *, topk, idx_bits):
    qb = Q_BLOCK
    rows = A_HEADS * qb
    q0 = pl.program_id(1) * qb
    nt = (q0 + qb + KEY_TILE - 1) // KEY_TILE
    tq = q0 + lax.broadcasted_iota(I32, (qb, KEY_TILE), 0)
    lane = lax.broadcasted_iota(I32, (qb, KEY_TILE), 1)
    qi = qi_ref[...].reshape(rows, LANES)
    wi = wi_ref[...].reshape(rows, LANES)

    def score_body(kt, carry):
        smin, smax = carry
        k0 = pl.multiple_of(kt * KEY_TILE, KEY_TILE)
        logits = _dot_nt(qi, ki_ref[pl.ds(k0, KEY_TILE), :])
        r = jnp.maximum(logits, 0.0) * _lane_tile(wi, KEY_TILE // LANES)
        s = r[0:qb]
        for h in range(1, IDX_HEADS):
            s = s + r[h * qb:(h + 1) * qb]
        s = s + 0.0
        causal = k0 + lane <= tq
        key_ref[kt] = _f32_key(jnp.where(causal, s, -jnp.inf))
        smin = jnp.minimum(smin, _fold_lanes(jnp.where(causal, s, jnp.inf), jnp.minimum))
        smax = jnp.maximum(smax, _fold_lanes(jnp.where(causal, s, -jnp.inf), jnp.maximum))
        return smin, smax

    smin, smax = lax.fori_loop(0, nt, score_body,
                               (jnp.full((qb, LANES), jnp.inf, F32), jnp.full((qb, LANES), -jnp.inf, F32)))
    _topk_select(key_ref, thr_ref, nt, topk, idx_bits,
                 jnp.min(smin, axis=1, keepdims=True), jnp.max(smax, axis=1, keepdims=True))
    thr = thr_ref[:, 0:1]

    qa = qa_ref[...].reshape(rows, QA_DIM)
    m_ref[...] = jnp.full((rows, LANES), NEG_BIG, F32)
    l_ref[...] = jnp.zeros((rows, LANES), F32)
    acc_ref[...] = jnp.zeros((rows, A_KV_RANK), F32)

    def attn_body(kt, carry):
        k0 = pl.multiple_of(kt * KEY_TILE, KEY_TILE)
        ka = ka_ref[pl.ds(k0, KEY_TILE), :]
        sel = jnp.where(key_ref[kt] > thr, k0 + lane, tq + 1) <= tq
        bias = jnp.where(sel, 0.0, NEG_BIG)
        s = _dot_nt(qa, ka) + _row_tile(bias, A_HEADS)
        m_old = m_ref[...]
        m_new = jnp.maximum(m_old, jnp.max(s, axis=1, keepdims=True))
        p = jnp.exp(s - _lane_tile(m_new, KEY_TILE // LANES))
        alpha = jnp.exp(m_old - m_new)
        l_ref[...] = l_ref[...] * alpha + jnp.sum(p, axis=1, keepdims=True)
        acc_ref[...] = (acc_ref[...] * _lane_tile(alpha, A_KV_RANK // LANES)
                        + _dot(p.astype(BF16), ka[:, 0:A_KV_RANK]))
        m_ref[...] = m_new
        return carry

    lax.fori_loop(0, nt, attn_body, 0)

    inv = 1.0 / jnp.maximum(l_ref[...], 1e-30)
    o_lat = (acc_ref[...] * _lane_tile(inv, A_KV_RANK // LANES)).astype(BF16)
    for h in range(A_HEADS):
        o_ref[:, h * HEAD_DIM:(h + 1) * HEAD_DIM] = _dot(o_lat[h * qb:(h + 1) * qb], wuv_ref[h]).astype(BF16)


def _dsa_attn(qi, wi, ki, qa, ka, wuv):
    nb, _, t, _ = qa.shape
    topk = min(IDX_TOPK_MAX, t // 4)
    idx_bits = int(np.ceil(np.log2(t)))
    qb = Q_BLOCK
    body = functools.partial(_dsa_attn_body, topk=topk, idx_bits=idx_bits)
    return pl.pallas_call(
        body,
        grid=(nb, t // qb),
        in_specs=[pl.BlockSpec((None, IDX_HEADS, qb, LANES), lambda b, i: (b, 0, i, 0)),
                  pl.BlockSpec((None, IDX_HEADS, qb, LANES), lambda b, i: (b, 0, i, 0)),
                  pl.BlockSpec((None, t, LANES), lambda b, i: (b, 0, 0)),
                  pl.BlockSpec((None, A_HEADS, qb, QA_DIM), lambda b, i: (b, 0, i, 0)),
                  pl.BlockSpec((None, t, QA_DIM), lambda b, i: (b, 0, 0)),
                  pl.BlockSpec(wuv.shape, lambda b, i: (0, 0, 0))],
        out_specs=pl.BlockSpec((None, qb, A_HEADS * HEAD_DIM), lambda b, i: (b, i, 0)),
        out_shape=jax.ShapeDtypeStruct((nb, t, A_HEADS * HEAD_DIM), BF16),
        scratch_shapes=[pltpu.VMEM((t // KEY_TILE, qb, KEY_TILE), I32),
                        pltpu.VMEM((qb, LANES), I32),
                        pltpu.VMEM((A_HEADS * qb, A_KV_RANK), F32),
                        pltpu.VMEM((A_HEADS * qb, LANES), F32),
                        pltpu.VMEM((A_HEADS * qb, LANES), F32)],
        compiler_params=_cparams(("parallel", "parallel")),
    )(qi, wi, ki, qa, ka, wuv)


def _nsa_prep_body(bq_ref, bkv_ref, tabh_ref, qb_ref, kc_ref, vc_ref, ks_ref, vs_ref, kw_ref, vw_ref):
    tm = bq_ref.shape[0]
    lane = lax.broadcasted_iota(I32, (tm, LANES), 1)
    ch, sh = tabh_ref[:, 0:LANES], tabh_ref[:, LANES:2 * LANES]

    def rope(x):
        rot = jnp.where(lane < ROPE_DIM // 2, pltpu.roll(x, LANES - ROPE_DIM // 2, 1),
                        pltpu.roll(x, ROPE_DIM // 2, 1))
        return x * ch + rot * sh

    scale = HEAD_DIM ** -0.5
    for h in range(B_HEADS):
        sl = slice(h * HEAD_DIM, (h + 1) * HEAD_DIM)
        qb_ref[:, sl] = (rope(bq_ref[:, sl]) * scale).astype(BF16)
    outs = (kc_ref, vc_ref, ks_ref, vs_ref, kw_ref, vw_ref)
    for kind in range(6):
        for g in range(B_GROUPS):
            c0 = (kind * B_GROUPS + g) * HEAD_DIM
            v = bkv_ref[:, c0:c0 + HEAD_DIM]
            if kind % 2 == 0:
                v = rope(v)
            outs[kind][g] = v.astype(BF16)


def _nsa_prep(z, tabh):
    nb, t, _ = z.shape
    tm = 256
    kv_spec = pl.BlockSpec((None, B_GROUPS, tm, HEAD_DIM), lambda b, i: (b, 0, i, 0))
    kv_shape = jax.ShapeDtypeStruct((nb, B_GROUPS, t, HEAD_DIM), BF16)
    return pl.pallas_call(
        _nsa_prep_body,
        grid=(nb, t // tm),
        in_specs=[pl.BlockSpec((None, tm, 1024), lambda b, i: (b, i, Z_BQ // 1024)),
                  pl.BlockSpec((None, tm, 1536), lambda b, i: (b, i, Z_BKV // 1536)),
                  pl.BlockSpec((None, tm, 256), lambda b, i: (b, i, 0))],
        out_specs=[pl.BlockSpec((None, tm, 1024), lambda b, i: (b, i, 0))] + [kv_spec] * 6,
        out_shape=[jax.ShapeDtypeStruct((nb, t, 1024), BF16)] + [kv_shape] * 6,
        compiler_params=_cparams(("parallel", "parallel")),
    )(z, z, tabh)


def _compress_body(ch_ref, pe_ref, w1_ref, w2_ref, o_ref):
    nc = ch_ref.shape[0]
    half = CMP_STRIDE * HEAD_DIM
    x = ch_ref[...]
    ha = _dot(x, w1_ref[0:half, :])
    hb = _dot(x, w1_ref[half:2 * half, :])
    bias = _dot(pe_ref[...], w1_ref[...])[0:1]
    pre = ha + pltpu.roll(hb, nc - 1, 0) + bias
    hid = jax.nn.gelu(pre, approximate=True)
    o_ref[...] = _dot(hid.astype(BF16), w2_ref[...]).astype(BF16)


def _compress(tok, pe, w1, w2):
    nb, ng, t, d = tok.shape
    nc = t // CMP_STRIDE
    chunks = tok.reshape(nb, ng, nc, CMP_STRIDE * d)
    pe8 = jnp.broadcast_to(pe.reshape(1, CMP_LEN * d), (8, CMP_LEN * d)).astype(BF16)
    return pl.pallas_call(
        _compress_body,
        grid=(nb, ng),
        in_specs=[pl.BlockSpec((None, None, nc, CMP_STRIDE * d), lambda b, g: (b, g, 0, 0)),
                  pl.BlockSpec(pe8.shape, lambda b, g: (0, 0)),
                  pl.BlockSpec(w1.shape, lambda b, g: (0, 0)),
                  pl.BlockSpec(w2.shape, lambda b, g: (0, 0))],
        out_specs=pl.BlockSpec((None, None, nc, d), lambda b, g: (b, g, 0, 0)),
        out_shape=jax.ShapeDtypeStruct((nb, ng, nc, d), BF16),
        compiler_params=_cparams(("parallel", "parallel")),
    )(chunks, pe8, w1.astype(BF16), w2.astype(BF16))


def _nsa_attn_body(q_ref, kc_ref, vc_ref, ks_ref, vs_ref, kw_ref, vw_ref, gate_ref, ov_ref, ex_ref,
                   o_ref, key_ref, thr_ref, acc_ref, m_ref, l_ref, *, n_cmp, n_sel):
    qb = Q_BLOCK
    rows = B_HPG * qb
    nc = kc_ref.shape[0]
    q0 = pl.program_id(2) * qb
    qs = jnp.concatenate([q_ref[:, j * HEAD_DIM:(j + 1) * HEAD_DIM] for j in range(B_HPG)], axis=0)

    tq_c = q0 + lax.broadcasted_iota(I32, (qb, nc), 0)
    n_id = lax.broadcasted_iota(I32, (qb, nc), 1)
    c_ok = jnp.where(n_id < n_cmp, n_id * CMP_STRIDE + (CMP_LEN - 1), tq_c + 1) <= tq_c
    c_bias = _row_tile(jnp.where(c_ok, 0.0, NEG_BIG), B_HPG)
    c_keep = _row_tile(jnp.where(c_ok, 1.0, 0.0), B_HPG)
    sc = _dot_nt(qs, kc_ref[...]) + c_bias
    e = jnp.exp(sc - jnp.max(sc, axis=1, keepdims=True)) * c_keep
    pc = e / jnp.maximum(jnp.sum(e, axis=1, keepdims=True), 1e-30)
    oc = _dot(pc.astype(BF16), vc_ref[...])

    pc4 = pc[0:qb]
    for j in range(1, B_HPG):
        pc4 = pc4 + pc[j * qb:(j + 1) * qb]
    p_hi, p_mid, p_lo = _split3(pc4)
    ov = ov_ref[...]
    imp = _dot(p_hi, ov) + _dot(p_mid, ov) + _dot(p_lo, ov)
    tq_b = q0 + lax.broadcasted_iota(I32, (qb, LANES), 0)
    blk = lax.broadcasted_iota(I32, (qb, LANES), 1)
    cur = tq_b // SLC_LEN
    d_cur = cur - blk
    forced = jnp.where(blk == 0, 1.0, jnp.where(d_cur == 0, 1.0, jnp.where(d_cur == 1, 1.0, 0.0)))
    admissible = blk * SLC_LEN <= tq_b
    free = jnp.where(admissible, jnp.where(forced > 0.5, -1.0, imp), -1.0)
    imp = jnp.where(forced > 0.5, jnp.inf, imp + 0.0)
    key_ref[0] = _f32_key(jnp.where(admissible, imp, -jnp.inf))
    _topk_select(key_ref, thr_ref, 1, n_sel, 7,
                 jnp.full((qb, 1), -1.0, F32), jnp.max(free, axis=1, keepdims=True))
    selm = jnp.where(key_ref[0] > thr_ref[:, 0:1], 1.0, 0.0).astype(BF16)

    nt = (q0 + qb + KEY_TILE - 1) // KEY_TILE
    tq_k = q0 + lax.broadcasted_iota(I32, (qb, KEY_TILE), 0)
    lane_k = lax.broadcasted_iota(I32, (qb, KEY_TILE), 1)
    m_ref[...] = jnp.full((rows, LANES), NEG_BIG, F32)
    l_ref[...] = jnp.zeros((rows, LANES), F32)
    acc_ref[...] = jnp.zeros((rows, HEAD_DIM), F32)

    def sel_body(kt, carry):
        k0 = pl.multiple_of(kt * KEY_TILE, KEY_TILE)
        chosen = _dot(selm, ex_ref[kt])
        ok = jnp.where(chosen > 0.5, k0 + lane_k, tq_k + 1) <= tq_k
        bias = _row_tile(jnp.where(ok, 0.0, NEG_BIG), B_HPG)
        s = _dot_nt(qs, ks_ref[pl.ds(k0, KEY_TILE), :]) + bias
        m_old = m_ref[...]
        m_new = jnp.maximum(m_old, jnp.max(s, axis=1, keepdims=True))
        p = jnp.exp(s - _lane_tile(m_new, KEY_TILE // LANES))
        alpha = jnp.exp(m_old - m_new)
        l_ref[...] = l_ref[...] * alpha + jnp.sum(p, axis=1, keepdims=True)
        acc_ref[...] = acc_ref[...] * alpha + _dot(p.astype(BF16), vs_ref[pl.ds(k0, KEY_TILE), :])
        m_ref[...] = m_new
        return carry

    lax.fori_loop(0, nt, sel_body, 0)
    os_ = acc_ref[...] / jnp.maximum(l_ref[...], 1e-30)

    span = WINDOW + qb
    start = pl.multiple_of(jnp.maximum(q0 - WINDOW, 0), qb)
    tq_w = q0 + lax.broadcasted_iota(I32, (qb, span), 0)
    wpos = start + lax.broadcasted_iota(I32, (qb, span), 1)
    w_ok = jnp.where(wpos > tq_w - WINDOW, wpos, tq_w + 1) <= tq_w
    w_bias = _row_tile(jnp.where(w_ok, 0.0, NEG_BIG), B_HPG)
    sw = _dot_nt(qs, kw_ref[pl.ds(start, span), :]) + w_bias
    ew = jnp.exp(sw - jnp.max(sw, axis=1, keepdims=True))
    pw = ew / jnp.maximum(jnp.sum(ew, axis=1, keepdims=True), 1e-30)
    ow = _dot(pw.astype(BF16), vw_ref[pl.ds(start, span), :])

    gate = gate_ref[...]
    for j in range(B_HPG):
        r = slice(j * qb, (j + 1) * qb)
        res = (gate[:, 3 * j:3 * j + 1] * oc[r] + gate[:, 3 * j + 1:3 * j + 2] * os_[r]
               + gate[:, 3 * j + 2:3 * j + 3] * ow[r])
        o_ref[:, j * HEAD_DIM:(j + 1) * HEAD_DIM] = res.astype(BF16)


def _nsa_attn(qb_all, kc, vc, ks, vs, kw, vw, gates, ov, ex):
    nb, t, _ = qb_all.shape
    nc = kc.shape[2]
    n_cmp = (t - CMP_LEN) // CMP_STRIDE + 1
    n_sel = min(SLC_TOPN, t // SLC_LEN)
    qb = Q_BLOCK
    body = functools.partial(_nsa_attn_body, n_cmp=n_cmp, n_sel=n_sel)
    res_c = pl.BlockSpec((None, None, nc, HEAD_DIM), lambda b, g, i: (b, g, 0, 0))
    res_t = pl.BlockSpec((None, None, t, HEAD_DIM), lambda b, g, i: (b, g, 0, 0))
    return pl.pallas_call(
        body,
        grid=(nb, B_GROUPS, t // qb),
        in_specs=[pl.BlockSpec((None, qb, B_HPG * HEAD_DIM), lambda b, g, i: (b, i, g)),
                  res_c, res_c, res_t, res_t, res_t, res_t,
                  pl.BlockSpec((None, None, qb, 3 * B_HPG), lambda b, g, i: (b, g, i, 0)),
                  pl.BlockSpec(ov.shape, lambda b, g, i: (0, 0)),
                  pl.BlockSpec(ex.shape, lambda b, g, i: (0, 0, 0))],
        out_specs=pl.BlockSpec((None, qb, B_HPG * HEAD_DIM), lambda b, g, i: (b, i, g)),
        out_shape=jax.ShapeDtypeStruct((nb, t, B_HEADS * HEAD_DIM), BF16),
        scratch_shapes=[pltpu.VMEM((1, qb, LANES), I32),
                        pltpu.VMEM((qb, LANES), I32),
                        pltpu.VMEM((B_HPG * qb, HEAD_DIM), F32),
                        pltpu.VMEM((B_HPG * qb, LANES), F32),
                        pltpu.VMEM((B_HPG * qb, LANES), F32)],
        compiler_params=_cparams(("parallel", "parallel", "parallel")),
    )(qb_all, kc, vc, ks, vs, kw, vw, gates, ov, ex)


def _merge_body(oa_ref, ob_ref, wa_ref, wb_ref, ga_ref, gb_ref, o_ref):
    a = _dot(oa_ref[...], wa_ref[...])
    b = _dot(ob_ref[...], wb_ref[...])
    o_ref[...] = (jax.nn.sigmoid(ga_ref[...]) * a + jax.nn.sigmoid(gb_ref[...]) * b).astype(BF16)


def _merge(o_a, o_b, w_a, w_b, z):
    nb, t, da = o_a.shape
    d = w_a.shape[1]
    tm, tn = 512, 1024
    return pl.pallas_call(
        _merge_body,
        grid=(nb, t // tm, d // tn),
        in_specs=[pl.BlockSpec((None, tm, da), lambda b, i, j: (b, i, 0)),
                  pl.BlockSpec((None, tm, da), lambda b, i, j: (b, i, 0)),
                  pl.BlockSpec((da, tn), lambda b, i, j: (0, j)),
                  pl.BlockSpec((da, tn), lambda b, i, j: (0, j)),
                  pl.BlockSpec((None, tm, tn), lambda b, i, j: (b, i, j)),
                  pl.BlockSpec((None, tm, tn), lambda b, i, j: (b, i, d // tn + j))],
        out_specs=pl.BlockSpec((None, tm, tn), lambda b, i, j: (b, i, j)),
        out_shape=jax.ShapeDtypeStruct((nb, t, d), BF16),
        compiler_params=_cparams(("parallel", "parallel", "parallel")),
    )(o_a, o_b, w_a, w_b, z, z)


def _outproj_body(mg_ref, wo_ref, x_ref, gpost_ref, gt_ref, gpre_ref, sc_ref, sh_ref,
                  wr_ref, br_ref, x1_ref, h_ref, idx_ref, gate_ref):
    tm = x_ref.shape[0]
    y = _dot(mg_ref[...], wo_ref[...])
    x1 = x_ref[...] + gt_ref[...] * _rms(y, gpost_ref[...])
    x1_ref[...] = x1
    h = _rms(x1, gpre_ref[...]) * (1.0 + sc_ref[...]) + sh_ref[...]
    h_ref[...] = h
    h_hi = h.astype(BF16)
    h_lo = (h - h_hi.astype(F32)).astype(BF16)
    logits = (_dot(h_hi, wr_ref[0]) + (_dot(h_hi, wr_ref[1]) + _dot(h_lo, wr_ref[0]))) + br_ref[...]
    lane = lax.broadcasted_iota(I32, (tm, N_EXPERTS), 1).astype(F32)
    vals, idxs = [], []
    cur = logits
    for _ in range(TOP_K):
        m = jnp.max(cur, axis=1, keepdims=True)
        am = jnp.min(jnp.where(cur == m, lane, float(N_EXPERTS)), axis=1, keepdims=True)
        vals.append(m)
        idxs.append(am)
        cur = jnp.where(lane == am, -jnp.inf, cur)
    es = [jnp.exp(v - vals[0]) for v in vals]
    tot = es[0] + es[1] + es[2] + es[3]
    for k in range(TOP_K):
        idx_ref[:, k:k + 1] = idxs[k].astype(I32)
        gate_ref[:, k:k + 1] = es[k] / tot


def _outproj(merged, w_out, x, g_post, gt1, g_pre, sc2, sh2, wr, br):
    nb, t, d = x.shape
    tm = 256
    vec = lambda: pl.BlockSpec((1, d), lambda b, i: (0, 0))
    mod = lambda: pl.BlockSpec((None, 1, d), lambda b, i: (b, 0, 0))
    row = lambda w: pl.BlockSpec((None, tm, w), lambda b, i: (b, i, 0))
    return pl.pallas_call(
        _outproj_body,
        grid=(nb, t // tm),
        in_specs=[row(d), pl.BlockSpec((d, d), lambda b, i: (0, 0)), row(d), vec(), mod(), vec(), mod(), mod(),
                  pl.BlockSpec(wr.shape, lambda b, i: (0, 0, 0)),
                  pl.BlockSpec((1, N_EXPERTS), lambda b, i: (0, 0))],
        out_specs=[row(d), row(d), row(TOP_K), row(TOP_K)],
        out_shape=[jax.ShapeDtypeStruct((nb, t, d), F32),
                   jax.ShapeDtypeStruct((nb, t, d), F32),
                   jax.ShapeDtypeStruct((nb, t, TOP_K), I32),
                   jax.ShapeDtypeStruct((nb, t, TOP_K), F32)],
        compiler_params=_cparams(("parallel", "parallel")),
    )(merged, w_out, x, g_post[None], gt1, g_pre[None], sc2, sh2, wr, br[None])


def _gmm_body(be_ref, nu_ref, tokc_ref, tokn_ref, h_ref, wg_ref, wu_ref, bg_ref, bu_ref, wd_ref, bd_ref,
              y_ref, xf_ref, xb_ref, acc_ref, sem):
    i, j = pl.program_id(0), pl.program_id(1)
    n_used = nu_ref[0]
    slot = i % 2

    def issue(tok_ref, s):
        def body(r, carry):
            pltpu.make_async_copy(h_ref.at[pl.ds(tok_ref[0, 0, r], 1), :],
                                  xf_ref.at[s, pl.ds(r, 1), :], sem.at[s]).start()
            return carry

        lax.fori_loop(0, ROW_BLK, body, 0)

    @pl.when(j == 0)
    def _():
        @pl.when(i == 0)
        def _():
            issue(tokc_ref, 0)

        @pl.when(i < n_used)
        def _():
            pltpu.make_async_copy(h_ref.at[pl.ds(0, ROW_BLK), :], xf_ref.at[slot], sem.at[slot]).wait()
            xb_ref[...] = xf_ref[slot].astype(BF16)

        @pl.when(i + 1 < n_used)
        def _():
            issue(tokn_ref, 1 - slot)

    last = pl.num_programs(1) - 1

    @pl.when(i < n_used)
    def _():
        x = xb_ref[...]
        g = jnp.minimum(_dot(x, wg_ref[...]) + bg_ref[...], SWIGLU_LIMIT)
        u = jnp.clip(_dot(x, wu_ref[...]) + bu_ref[...], -SWIGLU_LIMIT, SWIGLU_LIMIT)
        act = g * jax.nn.sigmoid(SWIGLU_ALPHA * g) * (u + 1.0)
        part = _dot(act.astype(BF16), wd_ref[...])

        @pl.when(j == 0)
        def _():
            acc_ref[...] = part

        @pl.when(j > 0)
        def _():
            acc_ref[...] = acc_ref[...] + part

        @pl.when(j == last)
        def _():
            y_ref[...] = (acc_ref[...] + bd_ref[...]).astype(BF16)

    @pl.when(jnp.logical_and(i >= n_used, j == last))
    def _():
        y_ref[...] = jnp.zeros(y_ref.shape, BF16)


def _gmm(blk_e, n_used, row_tok, h, wgu, bgu, wd, bd):
    d = h.shape[1]
    ff = wd.shape[1]
    p = row_tok.shape[0]
    nblk = p // ROW_BLK
    nft = ff // FF_TILE
    tok3 = row_tok.reshape(nblk, 1, ROW_BLK)

    def expert(i, be, nu):
        return be[jnp.minimum(i, nu[0] - 1)]

    def tile(i, j, nu):
        return jnp.where(i < nu[0], j, nft - 1)

    grid_spec = pltpu.PrefetchScalarGridSpec(
        num_scalar_prefetch=2,
        grid=(nblk, nft),
        in_specs=[pl.BlockSpec((1, 1, ROW_BLK), lambda i, j, be, nu: (i, 0, 0), memory_space=pltpu.SMEM),
                  pl.BlockSpec((1, 1, ROW_BLK), lambda i, j, be, nu: (jnp.minimum(i + 1, nblk - 1), 0, 0),
                               memory_space=pltpu.SMEM),
                  pl.BlockSpec(memory_space=pl.ANY),
                  pl.BlockSpec((None, d, FF_TILE), lambda i, j, be, nu: (expert(i, be, nu), 0, tile(i, j, nu))),
                  pl.BlockSpec((None, d, FF_TILE),
                               lambda i, j, be, nu: (expert(i, be, nu), 0, nft + tile(i, j, nu))),
                  pl.BlockSpec((None, 1, FF_TILE), lambda i, j, be, nu: (expert(i, be, nu), 0, tile(i, j, nu))),
                  pl.BlockSpec((None, 1, FF_TILE),
                               lambda i, j, be, nu: (expert(i, be, nu), 0, nft + tile(i, j, nu))),
                  pl.BlockSpec((None, FF_TILE, d), lambda i, j, be, nu: (expert(i, be, nu), tile(i, j, nu), 0)),
                  pl.BlockSpec((None, 1, d), lambda i, j, be, nu: (expert(i, be, nu), 0, 0))],
        out_specs=pl.BlockSpec((ROW_BLK, d), lambda i, j, be, nu: (i, 0)),
        scratch_shapes=[pltpu.VMEM((2, ROW_BLK, d), F32),
                        pltpu.VMEM((ROW_BLK, d), BF16),
                        pltpu.VMEM((ROW_BLK, d), F32),
                        pltpu.SemaphoreType.DMA((2,))],
    )
    return pl.pallas_call(
        _gmm_body,
        grid_spec=grid_spec,
        out_shape=jax.ShapeDtypeStruct((p, d), BF16),
        compiler_params=_cparams(("arbitrary", "arbitrary")),
    )(blk_e, n_used, tok3, tok3, h, wgu, wgu, bgu, bgu, wd, bd)


def _combine_body(destc_ref, destn_ref, ys_ref, gate_ref, x1_ref, g_ref, gt_ref, o_ref, buf, sem):
    tm = CMB_TOK
    i = pl.program_id(0)
    slot = i % 2

    def issue(dest_ref, s):
        def body(r, carry):
            pltpu.make_async_copy(ys_ref.at[dest_ref[0, 0, r]], buf.at[s, r], sem.at[s]).start()
            return carry

        lax.fori_loop(0, TOP_K * tm, body, 0)

    @pl.when(i == 0)
    def _():
        issue(destc_ref, 0)

    pltpu.make_async_copy(ys_ref.at[pl.ds(0, TOP_K * tm)], buf.at[slot], sem.at[slot]).wait()

    @pl.when(i + 1 < pl.num_programs(0))
    def _():
        issue(destn_ref, 1 - slot)

    y = buf[slot, 0:tm].astype(F32) * gate_ref[:, 0:1, :]
    for k in range(1, TOP_K):
        y = y + buf[slot, k * tm:(k + 1) * tm].astype(F32) * gate_ref[:, k:k + 1, :]
    d = y.shape[1] * y.shape[2]
    ms = jnp.sum(jnp.sum(y * y, axis=2, keepdims=True), axis=1, keepdims=True) * (1.0 / d)
    o_ref[...] = x1_ref[...] + gt_ref[...] * (y * lax.rsqrt(ms + NORM_EPS) * g_ref[...])


def _combine(dest, ys3, gates, x1, g_post, gt2):
    nb, t, d = x1.shape
    n = nb * t
    sub = d // LANES
    tm = CMB_TOK
    steps = n // tm
    dest_t = dest.reshape(steps, tm, TOP_K).transpose(0, 2, 1).reshape(steps, 1, TOP_K * tm)
    gates_b = jnp.broadcast_to(gates[:, :, None], (n, TOP_K, LANES))
    out = pl.pallas_call(
        _combine_body,
        grid=(steps,),
        in_specs=[pl.BlockSpec((1, 1, TOP_K * tm), lambda i: (i, 0, 0), memory_space=pltpu.SMEM),
                  pl.BlockSpec((1, 1, TOP_K * tm), lambda i: (jnp.minimum(i + 1, steps - 1), 0, 0),
                               memory_space=pltpu.SMEM),
                  pl.BlockSpec(memory_space=pl.ANY),
                  pl.BlockSpec((tm, TOP_K, LANES), lambda i: (i, 0, 0)),
                  pl.BlockSpec((tm, sub, LANES), lambda i: (i, 0, 0)),
                  pl.BlockSpec((sub, LANES), lambda i: (0, 0)),
                  pl.BlockSpec((None, sub, LANES), lambda i: ((i * tm) // t, 0, 0))],
        out_specs=pl.BlockSpec((tm, sub, LANES), lambda i: (i, 0, 0)),
        out_shape=jax.ShapeDtypeStruct((n, sub, LANES), F32),
        scratch_shapes=[pltpu.VMEM((2, TOP_K * tm, sub, LANES), BF16), pltpu.SemaphoreType.DMA((2,))],
        compiler_params=_cparams(("arbitrary",)),
    )(dest_t, dest_t, ys3, gates_b, x1.reshape(n, sub, LANES), g_post.reshape(sub, LANES),
      gt2.reshape(nb, sub, LANES))
    return out.reshape(nb, t, d)


def _route(top_idx):
    n = top_idx.shape[0]
    nk = n * TOP_K
    e = top_idx.reshape(nk)
    onehot = (e[:, None] == jnp.arange(N_EXPERTS, dtype=I32)[None, :]).astype(I32)
    csum = jnp.cumsum(onehot, axis=0)
    rank = jnp.sum((csum - onehot) * onehot, axis=1)
    counts = csum[-1]
    padded = (counts + ROW_BLK - 1) // ROW_BLK * ROW_BLK
    pend = jnp.cumsum(padded)
    pstart = pend - padded
    dest = (pstart[e] + rank).astype(I32)
    nblk = nk // ROW_BLK + N_EXPERTS
    tok = jnp.arange(nk, dtype=I32) // TOP_K
    row_tok = jnp.zeros((nblk * ROW_BLK,), I32).at[dest].set(tok)
    blk_start = jnp.arange(nblk, dtype=I32) * ROW_BLK
    blk_e = jnp.minimum(jnp.sum((pend[None, :] <= blk_start[:, None]).astype(I32), axis=1), N_EXPERTS - 1)
    n_used = (pend[-1:] // ROW_BLK).astype(I32)
    return dest.reshape(n, TOP_K), row_tok, blk_e.astype(I32), n_used


def _rot_cols(w, half):
    z = jnp.zeros_like(w)
    z = z.at[..., 0:half].set(w[..., half:2 * half])
    return z.at[..., half:2 * half].set(w[..., 0:half])


def _prep_w_in(w_in):
    d = w_in.shape[0]
    offs = np.cumsum([0, A_Q_RANK, A_KV_RANK, ROPE_DIM, IDX_DIM, IDX_HEADS, B_HEADS * HEAD_DIM,
                      6 * B_GROUPS * HEAD_DIM, 3 * B_HEADS, 2 * d])
    cq, ckv, krope, kidx, widx, bq, bkv, bgate, mgate = [w_in[:, int(offs[i]):int(offs[i + 1])] for i in range(9)]
    krope_rot = jnp.concatenate([krope[:, ROPE_DIM // 2:], krope[:, :ROPE_DIM // 2]], axis=1)
    small = jnp.concatenate([krope, kidx, krope_rot, widx, bgate,
                             jnp.zeros((d, 256 - 160), w_in.dtype)], axis=1)
    w = jnp.concatenate([mgate, bq, cq, ckv, small, bkv], axis=1)
    assert w.shape[1] == Z_COLS
    return w.astype(BF16)


def _rope_tables(positions):
    def tab(rot_dim, lo, width):
        inv = jnp.power(ROPE_THETA, -jnp.arange(0, rot_dim, 2, dtype=F32) / rot_dim)
        ang = positions.astype(F32)[..., None] * inv
        c, s = jnp.cos(ang), jnp.sin(ang)
        shape = positions.shape
        cos_t = jnp.concatenate([jnp.zeros(shape + (lo,), F32), c, c,
                                 jnp.ones(shape + (width - rot_dim,), F32),
                                 jnp.zeros(shape + (LANES - lo - width,), F32)], axis=-1)
        sin_t = jnp.concatenate([jnp.zeros(shape + (lo,), F32), -s, s,
                                 jnp.zeros(shape + (LANES - lo - rot_dim,), F32)], axis=-1)
        return jnp.concatenate([cos_t, sin_t], axis=-1)

    return tab(ROPE_DIM, 0, LANES), tab(IDX_ROPE, 32, IDX_DIM)


def _select_tables(t):
    n_cmp = (t - CMP_LEN) // CMP_STRIDE + 1
    nc = t // CMP_STRIDE
    n = np.arange(nc)[:, None]
    m = np.arange(LANES)[None, :]
    ov = (n * CMP_STRIDE < m * SLC_LEN + SLC_LEN) & (n * CMP_STRIDE + CMP_LEN - 1 >= m * SLC_LEN) & (n < n_cmp)
    key_blk = (np.arange(t) // SLC_LEN).reshape(t // KEY_TILE, 1, KEY_TILE)
    ex = key_blk == np.arange(LANES).reshape(1, LANES, 1)
    return jnp.asarray(ov, BF16), jnp.asarray(ex, BF16)


def kernel(x, c, positions, w_ada, b_ada, g_pre_mix, g_post_mix, g_pre_ffn, g_post_ffn, w_in, a_cq_norm, a_ckv_norm, a_w_uq, a_w_uk, a_w_uv, idx_w_q, idx_k_norm_g, idx_k_norm_b, cmp_k_pe, cmp_k_w1, cmp_k_w2, cmp_v_pe, cmp_v_w1, cmp_v_w2, w_br_a, w_br_b, w_out, w_router, b_router, w_gate_up, b_gate_up, w_down, b_down):
    nb, t, d = x.shape
    n = nb * t
    assert t % KEY_TILE == 0 and t // SLC_LEN <= LANES and d % LANES == 0

    mod = _adaln(c, w_ada, b_ada)
    sh1, sc1, gt1, sh2, sc2, gt2 = [m[:, None, :] for m in jnp.split(mod, 6, axis=-1)]

    z = _inproj(x, g_pre_mix, sc1, sh1, _prep_w_in(w_in))
    tabh, tabi = _rope_tables(positions)

    wuq = jnp.transpose(a_w_uq, (1, 0, 2))
    wuk = jnp.transpose(a_w_uk, (1, 2, 0))
    wuk = jnp.concatenate([jnp.zeros((A_HEADS, ROPE_DIM, A_KV_RANK), wuk.dtype), wuk], axis=1)
    widx = jnp.transpose(idx_w_q, (1, 0, 2))
    pad_idx = lambda w: jnp.pad(w, ((0, 0), (0, 0), (32, LANES - 32 - IDX_DIM)))
    pad_vec = lambda v: jnp.pad(v, (32, LANES - 32 - IDX_DIM))[None]
    qa, ka, qi, ki, wi, sg = _dsa_prep(
        z, tabh, tabi, a_cq_norm[None], a_ckv_norm[None],
        wuq.astype(BF16), _rot_cols(wuq, ROPE_DIM // 2).astype(BF16), wuk.astype(BF16),
        pad_idx(widx).astype(BF16), pad_idx(_rot_cols(widx, IDX_ROPE // 2)).astype(BF16),
        pad_vec(idx_k_norm_g), pad_vec(idx_k_norm_b))
    o_a = _dsa_attn(qi, wi, ki, qa, ka, a_w_uv.astype(BF16))

    qb_all, kc_tok, vc_tok, ks, vs, kw, vw = _nsa_prep(z, tabh)
    kc = _compress(kc_tok, cmp_k_pe, cmp_k_w1, cmp_k_w2)
    vc = _compress(vc_tok, cmp_v_pe, cmp_v_w1, cmp_v_w2)
    gates_b = sg[:, :, IDX_HEADS:IDX_HEADS + 3 * B_HEADS].reshape(nb, t, B_GROUPS, 3 * B_HPG).transpose(0, 2, 1, 3)
    ov, ex = _select_tables(t)
    o_b = _nsa_attn(qb_all, kc, vc, ks, vs, kw, vw, gates_b, ov, ex)

    merged = _merge(o_a, o_b, w_br_a.astype(BF16), w_br_b.astype(BF16), z)

    wr_hi = w_router.astype(BF16)
    wr = jnp.stack([wr_hi, (w_router - wr_hi.astype(F32)).astype(BF16)])
    x1, h2, top_idx, gates = _outproj(merged, w_out.astype(BF16), x, g_post_mix, gt1, g_pre_ffn, sc2, sh2,
                                      wr, b_router)
    dest, row_tok, blk_e, n_used = _route(top_idx.reshape(n, TOP_K))
    ys = _gmm(blk_e, n_used, row_tok, h2.reshape(n, d), w_gate_up.astype(BF16), b_gate_up[:, None, :],
              w_down.astype(BF16), b_down[:, None, :])
    return _combine(dest, ys.reshape(ys.shape[0], d // LANES, LANES), gates.reshape(n, TOP_K),
                    x1, g_post_ffn, gt2)
```

```python
import functools

import numpy as np
import jax
import jax.numpy as jnp
from jax import lax
from jax.experimental import pallas as pl
from jax.experimental.pallas import tpu as pltpu

F32 = jnp.float32
BF16 = jnp.bfloat16
I32 = jnp.int32

HEAD_DIM = 128
ROPE_DIM = 32
ROPE_THETA = 500000.0
NORM_EPS = 1e-6
Q_BLOCK = 128
A_HEADS = 8
A_Q_RANK = 512
A_KV_RANK = 256
IDX_HEADS = 8
IDX_DIM = 64
IDX_ROPE = 16
IDX_TOPK_MAX = 256
B_HEADS = 8
B_GROUPS = 2
B_HPG = 4
CMP_LEN = 32
CMP_STRIDE = 16
CMP_HIDDEN = 256
SLC_LEN = 64
SLC_TOPN = 16
WINDOW = 512
N_EXPERTS = 32
TOP_K = 4
SWIGLU_ALPHA = 1.702
SWIGLU_LIMIT = 7.0

LANES = 128
VMEM_LIMIT = 52 * 1024 * 1024

KEY_TILE = 512
QA_DIM = A_KV_RANK + LANES
ROW_BLK = 512
FF_TILE = 512
CMB_TOK = 128
ATTN_CHUNKS = 4

Z_MGATE, Z_BQ, Z_CQ, Z_CKV, Z_SMALL, Z_BKV, Z_COLS = 0, 4096, 5120, 5632, 5888, 6144, 7680

NEG_BIG = -1e30
INT_MIN = np.int32(-2147483648)
KEY_NEG_INF = np.int32(-2139095041)
BISECT_MAX_ITER = 24


def _cparams(sem):
    return pltpu.CompilerParams(dimension_semantics=sem, vmem_limit_bytes=VMEM_LIMIT)


def _dot(a, b):
    return jnp.dot(a, b, preferred_element_type=F32)


def _dot_nt(a, b):
    return lax.dot_general(a, b, (((1,), (1,)), ((), ())), preferred_element_type=F32)


def _split3(a):
    hi = a.astype(BF16)
    r1 = a - hi.astype(F32)
    mid = r1.astype(BF16)
    lo = (r1 - mid.astype(F32)).astype(BF16)
    return hi, mid, lo


def _lane_tile(a, n):
    return jnp.concatenate([a] * n, axis=1) if n > 1 else a


def _row_tile(a, n):
    return jnp.concatenate([a] * n, axis=0) if n > 1 else a


def _f32_key(s):
    bits = pltpu.bitcast(s, I32)
    return bits ^ (jnp.right_shift(bits, 31) & np.int32(0x7FFFFFFF))


def _count(key_ref, nt, pred):
    rows, width = key_ref.shape[1], key_ref.shape[2]

    def body(kt, acc):
        ind = jnp.where(pred(key_ref[kt], kt), 1.0, 0.0)
        part = ind[:, 0:LANES]
        for c in range(1, width // LANES):
            part = part + ind[:, c * LANES:(c + 1) * LANES]
        return acc + part

    acc = lax.fori_loop(0, nt, body, jnp.zeros((rows, LANES), F32))
    return jnp.sum(acc, axis=1, keepdims=True)


def _kth_key(key_ref, nt, k):
    rows = key_ref.shape[1]

    def bit_body(b, ans):
        cand = ans | jnp.left_shift(jnp.int32(1), 31 - b)
        cs = cand ^ INT_MIN
        cnt = _count(key_ref, nt, lambda kk, kt: kk >= cs)
        return jnp.where(cnt >= k, cand, ans)

    ans = lax.fori_loop(0, 32, bit_body, jnp.zeros((rows, 1), I32))
    return ans ^ INT_MIN


def _break_ties(key_ref, nt, t, k, idx_bits):
    rows, width = key_ref.shape[1], key_ref.shape[2]
    cnt_gt = _count(key_ref, nt, lambda kk, kt: kk > t)
    cnt_ge = _count(key_ref, nt, lambda kk, kt: kk >= t)
    need = k - cnt_gt

    @pl.when(jnp.max(cnt_ge) > k)
    def _():
        lane = lax.broadcasted_iota(I32, (rows, width), 1)

        def bit_body(b, c0):
            cand = c0 | jnp.left_shift(jnp.int32(1), idx_bits - 1 - b)
            f = _count(key_ref, nt, lambda kk, kt: jnp.where(kk == t, kt * width + lane, cand) < cand)
            return jnp.where(f < need, cand, c0)

        c0 = lax.fori_loop(0, idx_bits, bit_body, jnp.zeros((rows, 1), I32))

        def fix(kt, carry):
            kk = key_ref[kt]
            idx = kt * width + lane
            key_ref[kt] = jnp.where(jnp.where(kk == t, idx, c0) > c0, t - 1, kk)
            return carry

        lax.fori_loop(0, nt, fix, 0)


def _topk_select(key_ref, thr_ref, nt, k, idx_bits, lo0, hi0):
    rows = key_ref.shape[1]
    n_fin = _count(key_ref, nt, lambda kk, kt: kk > KEY_NEG_INF)
    done0 = jnp.where(n_fin <= k, 1.0, 0.0)
    thr0 = jnp.full((rows, 1), KEY_NEG_INF, I32)

    def cond(st):
        it, _, _, _, done, stuck = st
        return jnp.logical_and(it < BISECT_MAX_ITER, jnp.min(jnp.maximum(done, stuck)) < 0.5)

    def body(st):
        it, lo, hi, thr, done, stuck = st
        mid = 0.5 * lo + 0.5 * hi
        midk = _f32_key(mid)
        cnt = _count(key_ref, nt, lambda kk, kt: kk > midk)
        hit = jnp.where(cnt == k, 1.0 - done, 0.0)
        thr = jnp.where(hit > 0.5, midk, thr)
        done = jnp.maximum(done, hit)
        stuck = jnp.maximum(stuck, jnp.where(mid <= lo, 1.0, jnp.where(mid >= hi, 1.0, 0.0)))
        above = cnt >= k
        return it + 1, jnp.where(above, mid, lo), jnp.where(above, hi, mid), thr, done, stuck

    init = (jnp.int32(0), lo0, hi0, thr0, done0, jnp.zeros((rows, 1), F32))
    _, _, _, thr, done, _ = lax.while_loop(cond, body, init)
    thr_ref[...] = jnp.broadcast_to(thr, thr_ref.shape)

    @pl.when(jnp.min(done) < 0.5)
    def _():
        t = _kth_key(key_ref, nt, k)
        _break_ties(key_ref, nt, t, k, idx_bits)
        thr_ref[...] = jnp.broadcast_to(t - 1, thr_ref.shape)


def _ada_body(c_ref, w_ref, b_ref, o_ref):
    c = c_ref[...]
    s = c * jax.nn.sigmoid(c)
    s_hi, s_mid, s_lo = _split3(s)
    w_hi, w_mid, w_lo = _split3(w_ref[...])
    acc = _dot(s_hi, w_hi) + (_dot(s_hi, w_mid) + _dot(s_mid, w_hi))
    acc = acc + (_dot(s_hi, w_lo) + _dot(s_mid, w_mid) + _dot(s_lo, w_hi))
    o_ref[...] = acc + b_ref[...]


def _adaln(c, w_ada, b_ada):
    nb, d = c.shape
    n = w_ada.shape[1]
    tn = 1024
    cp = jnp.zeros((8, d), F32).at[:nb].set(c)
    out = pl.pallas_call(
        _ada_body,
        grid=(n // tn,),
        in_specs=[pl.BlockSpec((8, d), lambda j: (0, 0)),
                  pl.BlockSpec((d, tn), lambda j: (0, j)),
                  pl.BlockSpec((1, tn), lambda j: (0, j))],
        out_specs=pl.BlockSpec((8, tn), lambda j: (0, j)),
        out_shape=jax.ShapeDtypeStruct((8, n), F32),
        compiler_params=_cparams(("parallel",)),
    )(cp, w_ada, b_ada[None])
    return out[:nb]


def _inproj_body(x_ref, g_ref, sc_ref, sh_ref, w_ref, z_ref, h_ref):
    @pl.when(pl.program_id(2) == 0)
    def _():
        x = x_ref[...]
        y = x * lax.rsqrt(jnp.mean(x * x, axis=-1, keepdims=True) + NORM_EPS) * g_ref[...]
        h_ref[...] = (y * (1.0 + sc_ref[...]) + sh_ref[...]).astype(BF16)

    z_ref[...] = _dot(h_ref[...], w_ref[...])


def _inproj(x, g, sc, sh, w):
    nb, t, d = x.shape
    tm, tn = 512, 1536
    return pl.pallas_call(
        _inproj_body,
        grid=(nb, t // tm, Z_COLS // tn),
        in_specs=[pl.BlockSpec((None, tm, d), lambda b, i, j: (b, i, 0)),
                  pl.BlockSpec((1, d), lambda b, i, j: (0, 0)),
                  pl.BlockSpec((None, 1, d), lambda b, i, j: (b, 0, 0)),
                  pl.BlockSpec((None, 1, d), lambda b, i, j: (b, 0, 0)),
                  pl.BlockSpec((d, tn), lambda b, i, j: (0, j))],
        out_specs=pl.BlockSpec((None, tm, tn), lambda b, i, j: (b, i, j)),
        out_shape=jax.ShapeDtypeStruct((nb, t, Z_COLS), F32),
        scratch_shapes=[pltpu.VMEM((tm, d), BF16)],
        compiler_params=_cparams(("parallel", "parallel", "arbitrary")),
    )(x, g[None], sc, sh, w)


def _rms(x, g):
    return x * lax.rsqrt(jnp.mean(x * x, axis=-1, keepdims=True) + NORM_EPS) * g


def _dsa_prep_body(cq_ref, ckv_ref, sm_ref, tabh_ref, tabi_ref, gq_ref, gkv_ref,
                   wuq_ref, wuqr_ref, wuk_ref, widx_ref, widxr_ref, lng_ref, lnb_ref,
                   qa_ref, ka_ref, qi_ref, ki_ref, wi_ref, sg_ref):
    tm = cq_ref.shape[0]
    lane = lax.broadcasted_iota(I32, (tm, LANES), 1)
    cqn = _rms(cq_ref[...], gq_ref[...]).astype(BF16)
    ckvn = _rms(ckv_ref[...], gkv_ref[...])
    s0 = sm_ref[:, 0:LANES]
    s1 = sm_ref[:, LANES:2 * LANES]
    ch, sh = tabh_ref[:, 0:LANES], tabh_ref[:, LANES:2 * LANES]
    ci, si = tabi_ref[:, 0:LANES], tabi_ref[:, LANES:2 * LANES]

    k_rope = s0 * ch + pltpu.roll(s0, 32, 1) * sh
    ka_ref[:, 0:A_KV_RANK] = ckvn.astype(BF16)
    ka_ref[:, A_KV_RANK:QA_DIM] = jnp.where(lane < ROPE_DIM, k_rope, 0.0).astype(BF16)

    inside = jnp.where(lane >= 32, jnp.where(lane < 96, 1.0, 0.0), 0.0)
    mu = jnp.sum(s0 * inside, axis=-1, keepdims=True) * (1.0 / IDX_DIM)
    xc = (s0 - mu) * inside
    var = jnp.sum(xc * xc, axis=-1, keepdims=True) * (1.0 / IDX_DIM)
    y = xc * lax.rsqrt(var + NORM_EPS) * lng_ref[...] + lnb_ref[...]
    rot = jnp.where(lane < 40, pltpu.roll(y, LANES - 8, 1), pltpu.roll(y, 8, 1))
    ki_ref[...] = (y * ci + rot * si).astype(BF16)

    scale = HEAD_DIM ** -0.5
    w_scale = IDX_HEADS ** -0.5 * IDX_DIM ** -0.5
    for h in range(A_HEADS):
        qh = _dot(cqn, wuq_ref[h])
        qr = _dot(cqn, wuqr_ref[h])
        roped = qh * ch + qr * sh
        qabs = _dot(roped.astype(BF16), wuk_ref[h])
        qa_ref[h, :, 0:A_KV_RANK] = (qabs * scale).astype(BF16)
        qa_ref[h, :, A_KV_RANK:QA_DIM] = (jnp.where(lane < ROPE_DIM, roped, 0.0) * scale).astype(BF16)
        qih = _dot(cqn, widx_ref[h])
        qir = _dot(cqn, widxr_ref[h])
        qi_ref[h] = (qih * ci + qir * si).astype(BF16)
        wi_ref[h] = jnp.broadcast_to(s1[:, h:h + 1], (tm, LANES)) * w_scale
    sg_ref[...] = jax.nn.sigmoid(s1)


def _dsa_prep(z, tabh, tabi, gq, gkv, wuq, wuqr, wuk, widx, widxr, lng, lnb):
    nb, t, _ = z.shape
    tm = 256
    full = lambda a: pl.BlockSpec(a.shape, lambda b, i: (0,) * a.ndim)
    return pl.pallas_call(
        _dsa_prep_body,
        grid=(nb, t // tm),
        in_specs=[pl.BlockSpec((None, tm, A_Q_RANK), lambda b, i: (b, i, Z_CQ // A_Q_RANK)),
                  pl.BlockSpec((None, tm, A_KV_RANK), lambda b, i: (b, i, Z_CKV // A_KV_RANK)),
                  pl.BlockSpec((None, tm, 256), lambda b, i: (b, i, Z_SMALL // 256)),
                  pl.BlockSpec((None, tm, 256), lambda b, i: (b, i, 0)),
                  pl.BlockSpec((None, tm, 256), lambda b, i: (b, i, 0)),
                  full(gq), full(gkv), full(wuq), full(wuqr), full(wuk), full(widx), full(widxr),
                  full(lng), full(lnb)],
        out_specs=[pl.BlockSpec((None, A_HEADS, tm, QA_DIM), lambda b, i: (b, 0, i, 0)),
                   pl.BlockSpec((None, tm, QA_DIM), lambda b, i: (b, i, 0)),
                   pl.BlockSpec((None, IDX_HEADS, tm, LANES), lambda b, i: (b, 0, i, 0)),
                   pl.BlockSpec((None, tm, LANES), lambda b, i: (b, i, 0)),
                   pl.BlockSpec((None, IDX_HEADS, tm, LANES), lambda b, i: (b, 0, i, 0)),
                   pl.BlockSpec((None, tm, LANES), lambda b, i: (b, i, 0))],
        out_shape=[jax.ShapeDtypeStruct((nb, A_HEADS, t, QA_DIM), BF16),
                   jax.ShapeDtypeStruct((nb, t, QA_DIM), BF16),
                   jax.ShapeDtypeStruct((nb, IDX_HEADS, t, LANES), BF16),
                   jax.ShapeDtypeStruct((nb, t, LANES), BF16),
                   jax.ShapeDtypeStruct((nb, IDX_HEADS, t, LANES), F32),
                   jax.ShapeDtypeStruct((nb, t, LANES), F32)],
        compiler_params=_cparams(("parallel", "parallel")),
    )(z, z, z, tabh, tabi, gq, gkv, wuq, wuqr, wuk, widx, widxr, lng, lnb)


def _fold_lanes(x, op):
    out = x[:, 0:LANES]
    for c in range(1, x.shape[1] // LANES):
        out = op(out, x[:, c * LANES:(c + 1) * LANES])
    return out


def _dsa_attn_body(qi_ref, wi_ref, ki_ref, qa_ref, ka_ref, wuv_ref, o_ref,
                   key_ref, thr_ref, acc_ref, m_ref, l_ref, *, topk, idx_bits):
    qb = Q_BLOCK
    rows = A_HEADS * qb
    q0 = pl.program_id(1) * qb
    nt = (q0 + qb + KEY_TILE - 1) // KEY_TILE
    tq = q0 + lax.broadcasted_iota(I32, (qb, KEY_TILE), 0)
    lane = lax.broadcasted_iota(I32, (qb, KEY_TILE), 1)

    def score_body(kt, carry):
        smin, smax = carry
        k0 = pl.multiple_of(kt * KEY_TILE, KEY_TILE)
        ki = ki_ref[pl.ds(k0, KEY_TILE), :]
        s = None
        for h in range(IDX_HEADS):
            r = jnp.maximum(_dot_nt(qi_ref[h], ki), 0.0) * _lane_tile(wi_ref[h], KEY_TILE // LANES)
            s = r if s is None else s + r
        s = s + 0.0
        causal = k0 + lane <= tq
        key_ref[kt] = _f32_key(jnp.where(causal, s, -jnp.inf))
        smin = jnp.minimum(smin, _fold_lanes(jnp.where(causal, s, jnp.inf), jnp.minimum))
        smax = jnp.maximum(smax, _fold_lanes(jnp.where(causal, s, -jnp.inf), jnp.maximum))
        return smin, smax

    smin, smax = lax.fori_loop(0, nt, score_body,
                               (jnp.full((qb, LANES), jnp.inf, F32), jnp.full((qb, LANES), -jnp.inf, F32)))
    _topk_select(key_ref, thr_ref, nt, topk, idx_bits,
                 jnp.min(smin, axis=1, keepdims=True), jnp.max(smax, axis=1, keepdims=True))
    thr = thr_ref[:, 0:1]

    m_ref[...] = jnp.full((rows, LANES), NEG_BIG, F32)
    l_ref[...] = jnp.zeros((rows, LANES), F32)
    acc_ref[...] = jnp.zeros((rows, A_KV_RANK), F32)

    def attn_body(kt, carry):
        k0 = pl.multiple_of(kt * KEY_TILE, KEY_TILE)
        ka = ka_ref[pl.ds(k0, KEY_TILE), :]
        val = ka[:, 0:A_KV_RANK]
        sel = jnp.where(key_ref[kt] > thr, k0 + lane, tq + 1) <= tq
        bias = jnp.where(sel, 0.0, NEG_BIG)
        s = _dot_nt(qa_ref[...].reshape(rows, QA_DIM), ka) + _row_tile(bias, A_HEADS)
        m_old = m_ref[...]
        m_new = jnp.maximum(m_old, jnp.max(s, axis=1, keepdims=True))
        p = jnp.exp(s - _lane_tile(m_new, KEY_TILE // LANES))
        alpha = jnp.exp(m_old - m_new)
        l_ref[...] = l_ref[...] * alpha + jnp.sum(p, axis=1, keepdims=True)
        pb = p.astype(BF16)
        half = rows // ATTN_CHUNKS
        for c in range(ATTN_CHUNKS):
            r = slice(c * half, (c + 1) * half)
            acc_ref[r] = acc_ref[r] * _lane_tile(alpha[r], A_KV_RANK // LANES) + _dot(pb[r], val)
        m_ref[...] = m_new
        return carry

    lax.fori_loop(0, nt, attn_body, 0)

    inv = 1.0 / jnp.maximum(l_ref[...], 1e-30)
    o_lat = (acc_ref[...] * _lane_tile(inv, A_KV_RANK // LANES)).astype(BF16)
    for h in range(A_HEADS):
        o_ref[:, h * HEAD_DIM:(h + 1) * HEAD_DIM] = _dot(o_lat[h * qb:(h + 1) * qb], wuv_ref[h]).astype(BF16)


def _dsa_attn(qi, wi, ki, qa, ka, wuv):
    nb, _, t, _ = qa.shape
    topk = min(IDX_TOPK_MAX, t // 4)
    idx_bits = int(np.ceil(np.log2(t)))
    qb = Q_BLOCK
    body = functools.partial(_dsa_attn_body, topk=topk, idx_bits=idx_bits)
    return pl.pallas_call(
        body,
        grid=(nb, t // qb),
        in_specs=[pl.BlockSpec((None, IDX_HEADS, qb, LANES), lambda b, i: (b, 0, i, 0)),
                  pl.BlockSpec((None, IDX_HEADS, qb, LANES), lambda b, i: (b, 0, i, 0)),
                  pl.BlockSpec((None, t, LANES), lambda b, i: (b, 0, 0)),
                  pl.BlockSpec((None, A_HEADS, qb, QA_DIM), lambda b, i: (b, 0, i, 0)),
                  pl.BlockSpec((None, t, QA_DIM), lambda b, i: (b, 0, 0)),
                  pl.BlockSpec(wuv.shape, lambda b, i: (0, 0, 0))],
        out_specs=pl.BlockSpec((None, qb, A_HEADS * HEAD_DIM), lambda b, i: (b, i, 0)),
        out_shape=jax.ShapeDtypeStruct((nb, t, A_HEADS * HEAD_DIM), BF16),
        scratch_shapes=[pltpu.VMEM((t // KEY_TILE, qb, KEY_TILE), I32),
                        pltpu.VMEM((qb, LANES), I32),
                        pltpu.VMEM((A_HEADS * qb, A_KV_RANK), F32),
                        pltpu.VMEM((A_HEADS * qb, LANES), F32),
                        pltpu.VMEM((A_HEADS * qb, LANES), F32)],
        compiler_params=_cparams(("parallel", "parallel")),
    )(qi, wi, ki, qa, ka, wuv)


def _nsa_prep_body(bq_ref, bkv_ref, tabh_ref, qb_ref, kc_ref, vc_ref, ks_ref, vs_ref, kw_ref, vw_ref):
    tm = bq_ref.shape[0]
    lane = lax.broadcasted_iota(I32, (tm, LANES), 1)
    ch, sh = tabh_ref[:, 0:LANES], tabh_ref[:, LANES:2 * LANES]

    def rope(x):
        rot = jnp.where(lane < ROPE_DIM // 2, pltpu.roll(x, LANES - ROPE_DIM // 2, 1),
                        pltpu.roll(x, ROPE_DIM // 2, 1))
        return x * ch + rot * sh

    scale = HEAD_DIM ** -0.5
    for h in range(B_HEADS):
        sl = slice(h * HEAD_DIM, (h + 1) * HEAD_DIM)
        qb_ref[:, sl] = (rope(bq_ref[:, sl]) * scale).astype(BF16)
    outs = (kc_ref, vc_ref, ks_ref, vs_ref, kw_ref, vw_ref)
    for kind in range(6):
        for g in range(B_GROUPS):
            c0 = (kind * B_GROUPS + g) * HEAD_DIM
            v = bkv_ref[:, c0:c0 + HEAD_DIM]
            if kind % 2 == 0:
                v = rope(v)
            outs[kind][g] = v.astype(BF16)


def _nsa_prep(z, tabh):
    nb, t, _ = z.shape
    tm = 256
    kv_spec = pl.BlockSpec((None, B_GROUPS, tm, HEAD_DIM), lambda b, i: (b, 0, i, 0))
    kv_shape = jax.ShapeDtypeStruct((nb, B_GROUPS, t, HEAD_DIM), BF16)
    return pl.pallas_call(
        _nsa_prep_body,
        grid=(nb, t // tm),
        in_specs=[pl.BlockSpec((None, tm, 1024), lambda b, i: (b, i, Z_BQ // 1024)),
                  pl.BlockSpec((None, tm, 1536), lambda b, i: (b, i, Z_BKV // 1536)),
                  pl.BlockSpec((None, tm, 256), lambda b, i: (b, i, 0))],
        out_specs=[pl.BlockSpec((None, tm, 1024), lambda b, i: (b, i, 0))] + [kv_spec] * 6,
        out_shape=[jax.ShapeDtypeStruct((nb, t, 1024), BF16)] + [kv_shape] * 6,
        compiler_params=_cparams(("parallel", "parallel")),
    )(z, z, tabh)


def _compress_body(ch_ref, pe_ref, w1_ref, w2_ref, o_ref):
    nc = ch_ref.shape[0]
    half = CMP_STRIDE * HEAD_DIM
    x = ch_ref[...]
    ha = _dot(x, w1_ref[0:half, :])
    hb = _dot(x, w1_ref[half:2 * half, :])
    bias = _dot(pe_ref[...], w1_ref[...])[0:1]
    pre = ha + pltpu.roll(hb, nc - 1, 0) + bias
    hid = jax.nn.gelu(pre, approximate=True)
    o_ref[...] = _dot(hid.astype(BF16), w2_ref[...]).astype(BF16)


def _compress(tok, pe, w1, w2):
    nb, ng, t, d = tok.shape
    nc = t // CMP_STRIDE
    chunks = tok.reshape(nb, ng, nc, CMP_STRIDE * d)
    pe8 = jnp.broadcast_to(pe.reshape(1, CMP_LEN * d), (8, CMP_LEN * d)).astype(BF16)
    return pl.pallas_call(
        _compress_body,
        grid=(nb, ng),
        in_specs=[pl.BlockSpec((None, None, nc, CMP_STRIDE * d), lambda b, g: (b, g, 0, 0)),
                  pl.BlockSpec(pe8.shape, lambda b, g: (0, 0)),
                  pl.BlockSpec(w1.shape, lambda b, g: (0, 0)),
                  pl.BlockSpec(w2.shape, lambda b, g: (0, 0))],
        out_specs=pl.BlockSpec((None, None, nc, d), lambda b, g: (b, g, 0, 0)),
        out_shape=jax.ShapeDtypeStruct((nb, ng, nc, d), BF16),
        compiler_params=_cparams(("parallel", "parallel")),
    )(chunks, pe8, w1.astype(BF16), w2.astype(BF16))


def _nsa_attn_body(q_ref, kc_ref, vc_ref, ks_ref, vs_ref, kw_ref, vw_ref, gate_ref, ov_ref, ex_ref,
                   o_ref, key_ref, thr_ref, acc_ref, m_ref, l_ref, *, n_cmp, n_sel):
    qb = Q_BLOCK
    rows = B_HPG * qb
    nc = kc_ref.shape[0]
    q0 = pl.program_id(2) * qb
    qs = jnp.concatenate([q_ref[:, j * HEAD_DIM:(j + 1) * HEAD_DIM] for j in range(B_HPG)], axis=0)

    tq_c = q0 + lax.broadcasted_iota(I32, (qb, nc), 0)
    n_id = lax.broadcasted_iota(I32, (qb, nc), 1)
    c_ok = jnp.where(n_id < n_cmp, n_id * CMP_STRIDE + (CMP_LEN - 1), tq_c + 1) <= tq_c
    c_bias = _row_tile(jnp.where(c_ok, 0.0, NEG_BIG), B_HPG)
    c_keep = _row_tile(jnp.where(c_ok, 1.0, 0.0), B_HPG)
    sc = _dot_nt(qs, kc_ref[...]) + c_bias
    e = jnp.exp(sc - jnp.max(sc, axis=1, keepdims=True)) * c_keep
    pc = e / jnp.maximum(jnp.sum(e, axis=1, keepdims=True), 1e-30)
    oc = _dot(pc.astype(BF16), vc_ref[...])

    pc4 = pc[0:qb]
    for j in range(1, B_HPG):
        pc4 = pc4 + pc[j * qb:(j + 1) * qb]
    p_hi, p_mid, p_lo = _split3(pc4)
    ov = ov_ref[...]
    imp = _dot(p_hi, ov) + _dot(p_mid, ov) + _dot(p_lo, ov)
    tq_b = q0 + lax.broadcasted_iota(I32, (qb, LANES), 0)
    blk = lax.broadcasted_iota(I32, (qb, LANES), 1)
    cur = tq_b // SLC_LEN
    d_cur = cur - blk
    forced = jnp.where(blk == 0, 1.0, jnp.where(d_cur == 0, 1.0, jnp.where(d_cur == 1, 1.0, 0.0)))
    admissible = blk * SLC_LEN <= tq_b
    free = jnp.where(admissible, jnp.where(forced > 0.5, -1.0, imp), -1.0)
    imp = jnp.where(forced > 0.5, jnp.inf, imp + 0.0)
    key_ref[0] = _f32_key(jnp.where(admissible, imp, -jnp.inf))
    _topk_select(key_ref, thr_ref, 1, n_sel, 7,
                 jnp.full((qb, 1), -1.0, F32), jnp.max(free, axis=1, keepdims=True))
    selm = jnp.where(key_ref[0] > thr_ref[:, 0:1], 1.0, 0.0).astype(BF16)

    nt = (q0 + qb + KEY_TILE - 1) // KEY_TILE
    tq_k = q0 + lax.broadcasted_iota(I32, (qb, KEY_TILE), 0)
    lane_k = lax.broadcasted_iota(I32, (qb, KEY_TILE), 1)
    m_ref[...] = jnp.full((rows, LANES), NEG_BIG, F32)
    l_ref[...] = jnp.zeros((rows, LANES), F32)
    acc_ref[...] = jnp.zeros((rows, HEAD_DIM), F32)

    def sel_body(kt, carry):
        k0 = pl.multiple_of(kt * KEY_TILE, KEY_TILE)
        chosen = _dot(selm, ex_ref[kt])
        ok = jnp.where(chosen > 0.5, k0 + lane_k, tq_k + 1) <= tq_k
        bias = jnp.where(ok, 0.0, NEG_BIG)
        ks = ks_ref[pl.ds(k0, KEY_TILE), :]
        vs = vs_ref[pl.ds(k0, KEY_TILE), :]
        s = _dot_nt(qs, ks) + _row_tile(bias, B_HPG)
        m_old = m_ref[...]
        m_new = jnp.maximum(m_old, jnp.max(s, axis=1, keepdims=True))
        p = jnp.exp(s - _lane_tile(m_new, KEY_TILE // LANES))
        alpha = jnp.exp(m_old - m_new)
        l_ref[...] = l_ref[...] * alpha + jnp.sum(p, axis=1, keepdims=True)
        acc_ref[...] = acc_ref[...] * alpha + _dot(p.astype(BF16), vs)
        m_ref[...] = m_new
        return carry

    lax.fori_loop(0, nt, sel_body, 0)
    os_ = acc_ref[...] / jnp.maximum(l_ref[...], 1e-30)

    span = WINDOW + qb
    start = pl.multiple_of(jnp.maximum(q0 - WINDOW, 0), qb)
    tq_w = q0 + lax.broadcasted_iota(I32, (qb, span), 0)
    wpos = start + lax.broadcasted_iota(I32, (qb, span), 1)
    w_ok = jnp.where(wpos > tq_w - WINDOW, wpos, tq_w + 1) <= tq_w
    w_bias = _row_tile(jnp.where(w_ok, 0.0, NEG_BIG), B_HPG)
    sw = _dot_nt(qs, kw_ref[pl.ds(start, span), :]) + w_bias
    ew = jnp.exp(sw - jnp.max(sw, axis=1, keepdims=True))
    pw = ew / jnp.maximum(jnp.sum(ew, axis=1, keepdims=True), 1e-30)
    ow = _dot(pw.astype(BF16), vw_ref[pl.ds(start, span), :])

    gate = gate_ref[...]
    for j in range(B_HPG):
        r = slice(j * qb, (j + 1) * qb)
        res = (gate[:, 3 * j:3 * j + 1] * oc[r] + gate[:, 3 * j + 1:3 * j + 2] * os_[r]
               + gate[:, 3 * j + 2:3 * j + 3] * ow[r])
        o_ref[:, j * HEAD_DIM:(j + 1) * HEAD_DIM] = res.astype(BF16)


def _nsa_attn(qb_all, kc, vc, ks, vs, kw, vw, gates, ov, ex):
    nb, t, _ = qb_all.shape
    nc = kc.shape[2]
    n_cmp = (t - CMP_LEN) // CMP_STRIDE + 1
    n_sel = min(SLC_TOPN, t // SLC_LEN)
    qb = Q_BLOCK
    body = functools.partial(_nsa_attn_body, n_cmp=n_cmp, n_sel=n_sel)
    res_c = pl.BlockSpec((None, None, nc, HEAD_DIM), lambda b, g, i: (b, g, 0, 0))
    res_t = pl.BlockSpec((None, None, t, HEAD_DIM), lambda b, g, i: (b, g, 0, 0))
    return pl.pallas_call(
        body,
        grid=(nb, B_GROUPS, t // qb),
        in_specs=[pl.BlockSpec((None, qb, B_HPG * HEAD_DIM), lambda b, g, i: (b, i, g)),
                  res_c, res_c, res_t, res_t, res_t, res_t,
                  pl.BlockSpec((None, None, qb, 3 * B_HPG), lambda b, g, i: (b, g, i, 0)),
                  pl.BlockSpec(ov.shape, lambda b, g, i: (0, 0)),
                  pl.BlockSpec(ex.shape, lambda b, g, i: (0, 0, 0))],
        out_specs=pl.BlockSpec((None, qb, B_HPG * HEAD_DIM), lambda b, g, i: (b, i, g)),
        out_shape=jax.ShapeDtypeStruct((nb, t, B_HEADS * HEAD_DIM), BF16),
        scratch_shapes=[pltpu.VMEM((1, qb, LANES), I32),
                        pltpu.VMEM((qb, LANES), I32),
                        pltpu.VMEM((B_HPG * qb, HEAD_DIM), F32),
                        pltpu.VMEM((B_HPG * qb, LANES), F32),
                        pltpu.VMEM((B_HPG * qb, LANES), F32)],
        compiler_params=_cparams(("parallel", "parallel", "parallel")),
    )(qb_all, kc, vc, ks, vs, kw, vw, gates, ov, ex)


def _merge_body(oa_ref, ob_ref, wa_ref, wb_ref, ga_ref, gb_ref, o_ref):
    a = _dot(oa_ref[...], wa_ref[...])
    b = _dot(ob_ref[...], wb_ref[...])
    o_ref[...] = (jax.nn.sigmoid(ga_ref[...]) * a + jax.nn.sigmoid(gb_ref[...]) * b).astype(BF16)


def _merge(o_a, o_b, w_a, w_b, z):
    nb, t, da = o_a.shape
    d = w_a.shape[1]
    tm, tn = 512, 1024
    return pl.pallas_call(
        _merge_body,
        grid=(nb, t // tm, d // tn),
        in_specs=[pl.BlockSpec((None, tm, da), lambda b, i, j: (b, i, 0)),
                  pl.BlockSpec((None, tm, da), lambda b, i, j: (b, i, 0)),
                  pl.BlockSpec((da, tn), lambda b, i, j: (0, j)),
                  pl.BlockSpec((da, tn), lambda b, i, j: (0, j)),
                  pl.BlockSpec((None, tm, tn), lambda b, i, j: (b, i, j)),
                  pl.BlockSpec((None, tm, tn), lambda b, i, j: (b, i, d // tn + j))],
        out_specs=pl.BlockSpec((None, tm, tn), lambda b, i, j: (b, i, j)),
        out_shape=jax.ShapeDtypeStruct((nb, t, d), BF16),
        compiler_params=_cparams(("parallel", "parallel", "parallel")),
    )(o_a, o_b, w_a, w_b, z, z)


def _outproj_body(mg_ref, wo_ref, x_ref, gpost_ref, gt_ref, gpre_ref, sc_ref, sh_ref,
                  wr_ref, br_ref, x1_ref, h_ref, idx_ref, gate_ref):
    tm = x_ref.shape[0]
    y = _dot(mg_ref[...], wo_ref[...])
    x1 = x_ref[...] + gt_ref[...] * _rms(y, gpost_ref[...])
    x1_ref[...] = x1
    h = _rms(x1, gpre_ref[...]) * (1.0 + sc_ref[...]) + sh_ref[...]
    h_ref[...] = h
    h_hi = h.astype(BF16)
    h_lo = (h - h_hi.astype(F32)).astype(BF16)
    logits = (_dot(h_hi, wr_ref[0]) + (_dot(h_hi, wr_ref[1]) + _dot(h_lo, wr_ref[0]))) + br_ref[...]
    lane = lax.broadcasted_iota(I32, (tm, N_EXPERTS), 1).astype(F32)
    vals, idxs = [], []
    cur = logits
    for _ in range(TOP_K):
        m = jnp.max(cur, axis=1, keepdims=True)
        am = jnp.min(jnp.where(cur == m, lane, float(N_EXPERTS)), axis=1, keepdims=True)
        vals.append(m)
        idxs.append(am)
        cur = jnp.where(lane == am, -jnp.inf, cur)
    es = [jnp.exp(v - vals[0]) for v in vals]
    tot = es[0] + es[1] + es[2] + es[3]
    for k in range(TOP_K):
        idx_ref[:, k:k + 1] = idxs[k].astype(I32)
        gate_ref[:, k:k + 1] = es[k] / tot


def _outproj(merged, w_out, x, g_post, gt1, g_pre, sc2, sh2, wr, br):
    nb, t, d = x.shape
    tm = 256
    vec = lambda: pl.BlockSpec((1, d), lambda b, i: (0, 0))
    mod = lambda: pl.BlockSpec((None, 1, d), lambda b, i: (b, 0, 0))
    row = lambda w: pl.BlockSpec((None, tm, w), lambda b, i: (b, i, 0))
    return pl.pallas_call(
        _outproj_body,
        grid=(nb, t // tm),
        in_specs=[row(d), pl.BlockSpec((d, d), lambda b, i: (0, 0)), row(d), vec(), mod(), vec(), mod(), mod(),
                  pl.BlockSpec(wr.shape, lambda b, i: (0, 0, 0)),
                  pl.BlockSpec((1, N_EXPERTS), lambda b, i: (0, 0))],
        out_specs=[row(d), row(d), row(TOP_K), row(TOP_K)],
        out_shape=[jax.ShapeDtypeStruct((nb, t, d), F32),
                   jax.ShapeDtypeStruct((nb, t, d), F32),
                   jax.ShapeDtypeStruct((nb, t, TOP_K), I32),
                   jax.ShapeDtypeStruct((nb, t, TOP_K), F32)],
        compiler_params=_cparams(("parallel", "parallel")),
    )(merged, w_out, x, g_post[None], gt1, g_pre[None], sc2, sh2, wr, br[None])


def _gmm_body(be_ref, nu_ref, tokc_ref, tokn_ref, h_ref, wg_ref, wu_ref, bg_ref, bu_ref, wd_ref, bd_ref,
              y_ref, xf_ref, xb_ref, sem, *, ff_steps):
    i, j = pl.program_id(0), pl.program_id(1)
    n_used = nu_ref[0]
    slot = i % 2
    last = ff_steps - 1
    rows_per_step = ROW_BLK // ff_steps

    def row_copy(tok_ref, r, s):
        return pltpu.make_async_copy(h_ref.at[pl.ds(tok_ref[0, 0, r], 1), :],
                                     xf_ref.at[s, pl.ds(r, 1), :], sem.at[s])

    def wait_slot(s):
        pltpu.make_async_copy(h_ref.at[pl.ds(0, ROW_BLK), :], xf_ref.at[s], sem.at[s]).wait()

    @pl.when(j == 0)
    def _():
        @pl.when(i == 0)
        def _():
            def body(r, carry):
                row_copy(tokc_ref, r, 0).start()
                return carry

            lax.fori_loop(0, ROW_BLK, body, 0)

        @pl.when(i <= n_used)
        def _():
            wait_slot(slot)

        @pl.when(i < n_used)
        def _():
            xb_ref[...] = xf_ref[slot].astype(BF16)

    @pl.when(i < n_used)
    def _():
        x = xb_ref[...]
        g = jnp.minimum(_dot(x, wg_ref[...]) + bg_ref[...], SWIGLU_LIMIT)
        u = jnp.clip(_dot(x, wu_ref[...]) + bu_ref[...], -SWIGLU_LIMIT, SWIGLU_LIMIT)
        act = g * jax.nn.sigmoid(SWIGLU_ALPHA * g) * (u + 1.0)
        part = _dot(act.astype(BF16), wd_ref[...])
        for r in range(rows_per_step):
            row_copy(tokn_ref, j * rows_per_step + r, 1 - slot).start()

        @pl.when(j == 0)
        def _():
            y_ref[...] = part + bd_ref[...]

        @pl.when(j > 0)
        def _():
            y_ref[...] = y_ref[...] + part

    @pl.when(jnp.logical_and(i >= n_used, j == last))
    def _():
        y_ref[...] = jnp.zeros(y_ref.shape, F32)

    @pl.when(jnp.logical_and(jnp.logical_and(i == pl.num_programs(0) - 1, j == last), i < n_used))
    def _():
        wait_slot(1 - slot)


def _gmm(blk_e, n_used, row_tok, h, wgu, bgu, wd, bd):
    d = h.shape[1]
    ff = wd.shape[1]
    p = row_tok.shape[0]
    nblk = p // ROW_BLK
    nft = ff // FF_TILE
    tok3 = row_tok.reshape(nblk, 1, ROW_BLK)

    def expert(i, be, nu):
        return be[jnp.minimum(i, nu[0] - 1)]

    def tile(i, j, nu):
        return jnp.where(i < nu[0], j, nft - 1)

    grid_spec = pltpu.PrefetchScalarGridSpec(
        num_scalar_prefetch=2,
        grid=(nblk, nft),
        in_specs=[pl.BlockSpec((1, 1, ROW_BLK), lambda i, j, be, nu: (i, 0, 0), memory_space=pltpu.SMEM),
                  pl.BlockSpec((1, 1, ROW_BLK), lambda i, j, be, nu: (jnp.minimum(i + 1, nblk - 1), 0, 0),
                               memory_space=pltpu.SMEM),
                  pl.BlockSpec(memory_space=pl.ANY),
                  pl.BlockSpec((None, d, FF_TILE), lambda i, j, be, nu: (expert(i, be, nu), 0, tile(i, j, nu))),
                  pl.BlockSpec((None, d, FF_TILE),
                               lambda i, j, be, nu: (expert(i, be, nu), 0, nft + tile(i, j, nu))),
                  pl.BlockSpec((None, 1, FF_TILE), lambda i, j, be, nu: (expert(i, be, nu), 0, tile(i, j, nu))),
                  pl.BlockSpec((None, 1, FF_TILE),
                               lambda i, j, be, nu: (expert(i, be, nu), 0, nft + tile(i, j, nu))),
                  pl.BlockSpec((None, FF_TILE, d), lambda i, j, be, nu: (expert(i, be, nu), tile(i, j, nu), 0)),
                  pl.BlockSpec((None, 1, d), lambda i, j, be, nu: (expert(i, be, nu), 0, 0))],
        out_specs=pl.BlockSpec((ROW_BLK, d), lambda i, j, be, nu: (i, 0)),
        scratch_shapes=[pltpu.VMEM((2, ROW_BLK, d), F32),
                        pltpu.VMEM((ROW_BLK, d), BF16),
                        pltpu.SemaphoreType.DMA((2,))],
    )
    assert ROW_BLK % nft == 0
    return pl.pallas_call(
        functools.partial(_gmm_body, ff_steps=nft),
        grid_spec=grid_spec,
        out_shape=jax.ShapeDtypeStruct((p, d), F32),
        compiler_params=_cparams(("arbitrary", "arbitrary")),
    )(blk_e, n_used, tok3, tok3, h, wgu, wgu, bgu, bgu, wd, bd)


def _combine_body(destc_ref, destn_ref, ys_ref, gate_ref, x1_ref, g_ref, gt_ref, o_ref, buf, sem):
    tm = CMB_TOK
    i = pl.program_id(0)
    slot = i % 2

    def issue(dest_ref, s):
        def body(r, carry):
            pltpu.make_async_copy(ys_ref.at[pl.ds(dest_ref[0, 0, r], 1), :],
                                  buf.at[s, pl.ds(r, 1), :], sem.at[s]).start()
            return carry

        lax.fori_loop(0, TOP_K * tm, body, 0, unroll=16)

    @pl.when(i == 0)
    def _():
        issue(destc_ref, 0)

    pltpu.make_async_copy(ys_ref.at[pl.ds(0, TOP_K * tm), :], buf.at[slot], sem.at[slot]).wait()

    @pl.when(i + 1 < pl.num_programs(0))
    def _():
        issue(destn_ref, 1 - slot)

    gate = gate_ref[...]
    y = buf[slot, 0:tm] * gate[:, 0:1]
    for k in range(1, TOP_K):
        y = y + buf[slot, k * tm:(k + 1) * tm] * gate[:, k:k + 1]
    o_ref[...] = x1_ref[...] + gt_ref[...] * _rms(y, g_ref[...])


def _combine(dest, ys, gates, x1, g_post, gt2):
    nb, t, d = x1.shape
    n = nb * t
    tm = CMB_TOK
    steps = n // tm
    dest_t = dest.reshape(steps, tm, TOP_K).transpose(0, 2, 1).reshape(steps, 1, TOP_K * tm)
    out = pl.pallas_call(
        _combine_body,
        grid=(steps,),
        in_specs=[pl.BlockSpec((1, 1, TOP_K * tm), lambda i: (i, 0, 0), memory_space=pltpu.SMEM),
                  pl.BlockSpec((1, 1, TOP_K * tm), lambda i: (jnp.minimum(i + 1, steps - 1), 0, 0),
                               memory_space=pltpu.SMEM),
                  pl.BlockSpec(memory_space=pl.ANY),
                  pl.BlockSpec((tm, TOP_K), lambda i: (i, 0)),
                  pl.BlockSpec((tm, d), lambda i: (i, 0)),
                  pl.BlockSpec((1, d), lambda i: (0, 0)),
                  pl.BlockSpec((None, 1, d), lambda i: ((i * tm) // t, 0, 0))],
        out_specs=pl.BlockSpec((tm, d), lambda i: (i, 0)),
        out_shape=jax.ShapeDtypeStruct((n, d), F32),
        scratch_shapes=[pltpu.VMEM((2, TOP_K * tm, d), F32), pltpu.SemaphoreType.DMA((2,))],
        compiler_params=_cparams(("arbitrary",)),
    )(dest_t, dest_t, ys, gates, x1.reshape(n, d), g_post[None], gt2)
    return out.reshape(nb, t, d)


def _route(top_idx):
    n = top_idx.shape[0]
    nk = n * TOP_K
    e = top_idx.reshape(nk)
    onehot = (e[:, None] == jnp.arange(N_EXPERTS, dtype=I32)[None, :]).astype(I32)
    csum = jnp.cumsum(onehot, axis=0)
    rank = jnp.sum((csum - onehot) * onehot, axis=1)
    counts = csum[-1]
    padded = (counts + ROW_BLK - 1) // ROW_BLK * ROW_BLK
    pend = jnp.cumsum(padded)
    pstart = pend - padded
    dest = (pstart[e] + rank).astype(I32)
    nblk = nk // ROW_BLK + N_EXPERTS
    tok = jnp.arange(nk, dtype=I32) // TOP_K
    row_tok = jnp.zeros((nblk * ROW_BLK,), I32).at[dest].set(tok, unique_indices=True)
    blk_start = jnp.arange(nblk, dtype=I32) * ROW_BLK
    blk_e = jnp.minimum(jnp.sum((pend[None, :] <= blk_start[:, None]).astype(I32), axis=1), N_EXPERTS - 1)
    n_used = (pend[-1:] // ROW_BLK).astype(I32)
    return dest.reshape(n, TOP_K), row_tok, blk_e.astype(I32), n_used


def _rot_cols(w, half):
    z = jnp.zeros_like(w)
    z = z.at[..., 0:half].set(w[..., half:2 * half])
    return z.at[..., half:2 * half].set(w[..., 0:half])


def _prep_w_in(w_in):
    d = w_in.shape[0]
    offs = np.cumsum([0, A_Q_RANK, A_KV_RANK, ROPE_DIM, IDX_DIM, IDX_HEADS, B_HEADS * HEAD_DIM,
                      6 * B_GROUPS * HEAD_DIM, 3 * B_HEADS, 2 * d])
    cq, ckv, krope, kidx, widx, bq, bkv, bgate, mgate = [w_in[:, int(offs[i]):int(offs[i + 1])] for i in range(9)]
    krope_rot = jnp.concatenate([krope[:, ROPE_DIM // 2:], krope[:, :ROPE_DIM // 2]], axis=1)
    small = jnp.concatenate([krope, kidx, krope_rot, widx, bgate,
                             jnp.zeros((d, 256 - 160), w_in.dtype)], axis=1)
    w = jnp.concatenate([mgate, bq, cq, ckv, small, bkv], axis=1)
    assert w.shape[1] == Z_COLS
    return w.astype(BF16)


def _rope_tables(positions):
    def tab(rot_dim, lo, width):
        inv = jnp.power(ROPE_THETA, -jnp.arange(0, rot_dim, 2, dtype=F32) / rot_dim)
        ang = positions.astype(F32)[..., None] * inv
        c, s = jnp.cos(ang), jnp.sin(ang)
        shape = positions.shape
        cos_t = jnp.concatenate([jnp.zeros(shape + (lo,), F32), c, c,
                                 jnp.ones(shape + (width - rot_dim,), F32),
                                 jnp.zeros(shape + (LANES - lo - width,), F32)], axis=-1)
        sin_t = jnp.concatenate([jnp.zeros(shape + (lo,), F32), -s, s,
                                 jnp.zeros(shape + (LANES - lo - rot_dim,), F32)], axis=-1)
        return jnp.concatenate([cos_t, sin_t], axis=-1)

    return tab(ROPE_DIM, 0, LANES), tab(IDX_ROPE, 32, IDX_DIM)


def _select_tables(t):
    n_cmp = (t - CMP_LEN) // CMP_STRIDE + 1
    nc = t // CMP_STRIDE
    n = np.arange(nc)[:, None]
    m = np.arange(LANES)[None, :]
    ov = (n * CMP_STRIDE < m * SLC_LEN + SLC_LEN) & (n * CMP_STRIDE + CMP_LEN - 1 >= m * SLC_LEN) & (n < n_cmp)
    key_blk = (np.arange(t) // SLC_LEN).reshape(t // KEY_TILE, 1, KEY_TILE)
    ex = key_blk == np.arange(LANES).reshape(1, LANES, 1)
    return jnp.asarray(ov, BF16), jnp.asarray(ex, BF16)


def kernel(x, c, positions, w_ada, b_ada, g_pre_mix, g_post_mix, g_pre_ffn, g_post_ffn, w_in, a_cq_norm, a_ckv_norm, a_w_uq, a_w_uk, a_w_uv, idx_w_q, idx_k_norm_g, idx_k_norm_b, cmp_k_pe, cmp_k_w1, cmp_k_w2, cmp_v_pe, cmp_v_w1, cmp_v_w2, w_br_a, w_br_b, w_out, w_router, b_router, w_gate_up, b_gate_up, w_down, b_down):
    nb, t, d = x.shape
    n = nb * t
    assert t % KEY_TILE == 0 and t // SLC_LEN <= LANES and d % LANES == 0

    mod = _adaln(c, w_ada, b_ada)
    sh1, sc1, gt1, sh2, sc2, gt2 = [m[:, None, :] for m in jnp.split(mod, 6, axis=-1)]

    z = _inproj(x, g_pre_mix, sc1, sh1, _prep_w_in(w_in))
    tabh, tabi = _rope_tables(positions)

    wuq = jnp.transpose(a_w_uq, (1, 0, 2))
    wuk = jnp.transpose(a_w_uk, (1, 2, 0))
    wuk = jnp.concatenate([jnp.zeros((A_HEADS, ROPE_DIM, A_KV_RANK), wuk.dtype), wuk], axis=1)
    widx = jnp.transpose(idx_w_q, (1, 0, 2))
    pad_idx = lambda w: jnp.pad(w, ((0, 0), (0, 0), (32, LANES - 32 - IDX_DIM)))
    pad_vec = lambda v: jnp.pad(v, (32, LANES - 32 - IDX_DIM))[None]
    qa, ka, qi, ki, wi, sg = _dsa_prep(
        z, tabh, tabi, a_cq_norm[None], a_ckv_norm[None],
        wuq.astype(BF16), _rot_cols(wuq, ROPE_DIM // 2).astype(BF16), wuk.astype(BF16),
        pad_idx(widx).astype(BF16), pad_idx(_rot_cols(widx, IDX_ROPE // 2)).astype(BF16),
        pad_vec(idx_k_norm_g), pad_vec(idx_k_norm_b))
    o_a = _dsa_attn(qi, wi, ki, qa, ka, a_w_uv.astype(BF16))

    qb_all, kc_tok, vc_tok, ks, vs, kw, vw = _nsa_prep(z, tabh)
    kc = _compress(kc_tok, cmp_k_pe, cmp_k_w1, cmp_k_w2)
    vc = _compress(vc_tok, cmp_v_pe, cmp_v_w1, cmp_v_w2)
    gates_b = sg[:, :, IDX_HEADS:IDX_HEADS + 3 * B_HEADS].reshape(nb, t, B_GROUPS, 3 * B_HPG).transpose(0, 2, 1, 3)
    ov, ex = _select_tables(t)
    o_b = _nsa_attn(qb_all, kc, vc, ks, vs, kw, vw, gates_b, ov, ex)

    merged = _merge(o_a, o_b, w_br_a.astype(BF16), w_br_b.astype(BF16), z)

    wr_hi = w_router.astype(BF16)
    wr = jnp.stack([wr_hi, (w_router - wr_hi.astype(F32)).astype(BF16)])
    x1, h2, top_idx, gates = _outproj(merged, w_out.astype(BF16), x, g_post_mix, gt1, g_pre_ffn, sc2, sh2,
                                      wr, b_router)
    dest, row_tok, blk_e, n_used = _route(top_idx.reshape(n, TOP_K))
    ys = _gmm(blk_e, n_used, row_tok, h2.reshape(n, d), w_gate_up.astype(BF16), b_gate_up[:, None, :],
              w_down.astype(BF16), b_down[:, None, :])
    return _combine(dest, ys, gates.reshape(n, TOP_K), x1, g_post_ffn, gt2)
```

```python
import functools

import numpy as np
import jax
import jax.numpy as jnp
from jax import lax
from jax.experimental import pallas as pl
from jax.experimental.pallas import tpu as pltpu

F32 = jnp.float32
BF16 = jnp.bfloat16
I32 = jnp.int32

HEAD_DIM = 128
ROPE_DIM = 32
ROPE_THETA = 500000.0
NORM_EPS = 1e-6
Q_BLOCK = 128
A_HEADS = 8
A_Q_RANK = 512
A_KV_RANK = 256
IDX_HEADS = 8
IDX_DIM = 64
IDX_ROPE = 16
IDX_TOPK_MAX = 256
B_HEADS = 8
B_GROUPS = 2
B_HPG = 4
CMP_LEN = 32
CMP_STRIDE = 16
CMP_HIDDEN = 256
SLC_LEN = 64
SLC_TOPN = 16
WINDOW = 512
N_EXPERTS = 32
TOP_K = 4
SWIGLU_ALPHA = 1.702
SWIGLU_LIMIT = 7.0

LANES = 128
VMEM_LIMIT = 52 * 1024 * 1024

KEY_TILE = 512
QA_DIM = A_KV_RANK + LANES
ROW_BLK = 512
FF_TILE = 512
CMB_TOK = 128
ATTN_CHUNKS = 2
NSA_QB = 256
DSA_QB = 256

Z_MGATE, Z_BQ, Z_CQ, Z_CKV, Z_SMALL, Z_BKV, Z_COLS = 0, 4096, 5120, 5632, 5888, 6144, 7680

NEG_BIG = -1e30
INT_MIN = np.int32(-2147483648)
KEY_NEG_INF = np.int32(-2139095041)
BISECT_MAX_ITER = 24


def _cparams(sem):
    return pltpu.CompilerParams(dimension_semantics=sem, vmem_limit_bytes=VMEM_LIMIT)


def _dot(a, b):
    return jnp.dot(a, b, preferred_element_type=F32)


def _dot_nt(a, b):
    return lax.dot_general(a, b, (((1,), (1,)), ((), ())), preferred_element_type=F32)


def _split3(a):
    hi = a.astype(BF16)
    r1 = a - hi.astype(F32)
    mid = r1.astype(BF16)
    lo = (r1 - mid.astype(F32)).astype(BF16)
    return hi, mid, lo


def _lane_tile(a, n):
    return jnp.concatenate([a] * n, axis=1) if n > 1 else a


def _row_tile(a, n):
    return jnp.concatenate([a] * n, axis=0) if n > 1 else a


def _f32_key(s):
    bits = pltpu.bitcast(s, I32)
    return bits ^ (jnp.right_shift(bits, 31) & np.int32(0x7FFFFFFF))


def _count(key_ref, nt, pred):
    rows, width = key_ref.shape[1], key_ref.shape[2]

    def body(kt, acc):
        ind = jnp.where(pred(key_ref[kt], kt), 1.0, 0.0)
        part = ind[:, 0:LANES]
        for c in range(1, width // LANES):
            part = part + ind[:, c * LANES:(c + 1) * LANES]
        return acc + part

    acc = lax.fori_loop(0, nt, body, jnp.zeros((rows, LANES), F32))
    return jnp.sum(acc, axis=1, keepdims=True)


def _kth_key(key_ref, nt, k):
    rows = key_ref.shape[1]

    def bit_body(b, ans):
        cand = ans | jnp.left_shift(jnp.int32(1), 31 - b)
        cs = cand ^ INT_MIN
        cnt = _count(key_ref, nt, lambda kk, kt: kk >= cs)
        return jnp.where(cnt >= k, cand, ans)

    ans = lax.fori_loop(0, 32, bit_body, jnp.zeros((rows, 1), I32))
    return ans ^ INT_MIN


def _break_ties(key_ref, nt, t, k, idx_bits):
    rows, width = key_ref.shape[1], key_ref.shape[2]
    cnt_gt = _count(key_ref, nt, lambda kk, kt: kk > t)
    cnt_ge = _count(key_ref, nt, lambda kk, kt: kk >= t)
    need = k - cnt_gt

    @pl.when(jnp.max(cnt_ge) > k)
    def _():
        lane = lax.broadcasted_iota(I32, (rows, width), 1)

        def bit_body(b, c0):
            cand = c0 | jnp.left_shift(jnp.int32(1), idx_bits - 1 - b)
            f = _count(key_ref, nt, lambda kk, kt: jnp.where(kk == t, kt * width + lane, cand) < cand)
            return jnp.where(f < need, cand, c0)

        c0 = lax.fori_loop(0, idx_bits, bit_body, jnp.zeros((rows, 1), I32))

        def fix(kt, carry):
            kk = key_ref[kt]
            idx = kt * width + lane
            key_ref[kt] = jnp.where(jnp.where(kk == t, idx, c0) > c0, t - 1, kk)
            return carry

        lax.fori_loop(0, nt, fix, 0)


def _topk_select(key_ref, thr_ref, nt, k, idx_bits, lo0, hi0):
    rows = key_ref.shape[1]
    n_fin = _count(key_ref, nt, lambda kk, kt: kk > KEY_NEG_INF)
    done0 = jnp.where(n_fin <= k, 1.0, 0.0)
    thr0 = jnp.full((rows, 1), KEY_NEG_INF, I32)

    def cond(st):
        it, _, _, _, done, stuck = st
        return jnp.logical_and(it < BISECT_MAX_ITER, jnp.min(jnp.maximum(done, stuck)) < 0.5)

    def body(st):
        it, lo, hi, thr, done, stuck = st
        mid = 0.5 * lo + 0.5 * hi
        midk = _f32_key(mid)
        cnt = _count(key_ref, nt, lambda kk, kt: kk > midk)
        hit = jnp.where(cnt == k, 1.0 - done, 0.0)
        thr = jnp.where(hit > 0.5, midk, thr)
        done = jnp.maximum(done, hit)
        stuck = jnp.maximum(stuck, jnp.where(mid <= lo, 1.0, jnp.where(mid >= hi, 1.0, 0.0)))
        above = cnt >= k
        return it + 1, jnp.where(above, mid, lo), jnp.where(above, hi, mid), thr, done, stuck

    init = (jnp.int32(0), lo0, hi0, thr0, done0, jnp.zeros((rows, 1), F32))
    _, _, _, thr, done, _ = lax.while_loop(cond, body, init)
    thr_ref[...] = jnp.broadcast_to(thr, thr_ref.shape)

    @pl.when(jnp.min(done) < 0.5)
    def _():
        t = _kth_key(key_ref, nt, k)
        _break_ties(key_ref, nt, t, k, idx_bits)
        thr_ref[...] = jnp.broadcast_to(t - 1, thr_ref.shape)


def _ada_body(c_ref, w_ref, b_ref, o_ref):
    c = c_ref[...]
    s = c * jax.nn.sigmoid(c)
    s_hi, s_mid, s_lo = _split3(s)
    w_hi, w_mid, w_lo = _split3(w_ref[...])
    acc = _dot(s_hi, w_hi) + (_dot(s_hi, w_mid) + _dot(s_mid, w_hi))
    acc = acc + (_dot(s_hi, w_lo) + _dot(s_mid, w_mid) + _dot(s_lo, w_hi))
    o_ref[...] = acc + b_ref[...]


def _adaln(c, w_ada, b_ada):
    nb, d = c.shape
    n = w_ada.shape[1]
    tn = 1024
    cp = jnp.zeros((8, d), F32).at[:nb].set(c)
    out = pl.pallas_call(
        _ada_body,
        grid=(n // tn,),
        in_specs=[pl.BlockSpec((8, d), lambda j: (0, 0)),
                  pl.BlockSpec((d, tn), lambda j: (0, j)),
                  pl.BlockSpec((1, tn), lambda j: (0, j))],
        out_specs=pl.BlockSpec((8, tn), lambda j: (0, j)),
        out_shape=jax.ShapeDtypeStruct((8, n), F32),
        compiler_params=_cparams(("parallel",)),
    )(cp, w_ada, b_ada[None])
    return out[:nb]


def _inproj_body(x_ref, g_ref, sc_ref, sh_ref, w_ref, z_ref, h_ref):
    @pl.when(pl.program_id(2) == 0)
    def _():
        x = x_ref[...]
        y = x * lax.rsqrt(jnp.mean(x * x, axis=-1, keepdims=True) + NORM_EPS) * g_ref[...]
        h_ref[...] = (y * (1.0 + sc_ref[...]) + sh_ref[...]).astype(BF16)

    z_ref[...] = _dot(h_ref[...], w_ref[...])


def _inproj(x, g, sc, sh, w):
    nb, t, d = x.shape
    tm, tn = 512, 1536
    return pl.pallas_call(
        _inproj_body,
        grid=(nb, t // tm, Z_COLS // tn),
        in_specs=[pl.BlockSpec((None, tm, d), lambda b, i, j: (b, i, 0)),
                  pl.BlockSpec((1, d), lambda b, i, j: (0, 0)),
                  pl.BlockSpec((None, 1, d), lambda b, i, j: (b, 0, 0)),
                  pl.BlockSpec((None, 1, d), lambda b, i, j: (b, 0, 0)),
                  pl.BlockSpec((d, tn), lambda b, i, j: (0, j))],
        out_specs=pl.BlockSpec((None, tm, tn), lambda b, i, j: (b, i, j)),
        out_shape=jax.ShapeDtypeStruct((nb, t, Z_COLS), F32),
        scratch_shapes=[pltpu.VMEM((tm, d), BF16)],
        compiler_params=_cparams(("parallel", "parallel", "arbitrary")),
    )(x, g[None], sc, sh, w)


def _rms(x, g):
    return x * lax.rsqrt(jnp.mean(x * x, axis=-1, keepdims=True) + NORM_EPS) * g


def _dsa_prep_body(cq_ref, ckv_ref, sm_ref, tabh_ref, tabi_ref, gq_ref, gkv_ref,
                   wuq_ref, wuqr_ref, wuk_ref, widx_ref, widxr_ref, lng_ref, lnb_ref,
                   qa_ref, ka_ref, qi_ref, ki_ref, wi_ref, sg_ref):
    tm = cq_ref.shape[0]
    lane = lax.broadcasted_iota(I32, (tm, LANES), 1)
    cqn = _rms(cq_ref[...], gq_ref[...]).astype(BF16)
    ckvn = _rms(ckv_ref[...], gkv_ref[...])
    s0 = sm_ref[:, 0:LANES]
    s1 = sm_ref[:, LANES:2 * LANES]
    ch, sh = tabh_ref[:, 0:LANES], tabh_ref[:, LANES:2 * LANES]
    ci, si = tabi_ref[:, 0:LANES], tabi_ref[:, LANES:2 * LANES]

    k_rope = s0 * ch + pltpu.roll(s0, 32, 1) * sh
    ka_ref[:, 0:A_KV_RANK] = ckvn.astype(BF16)
    ka_ref[:, A_KV_RANK:QA_DIM] = jnp.where(lane < ROPE_DIM, k_rope, 0.0).astype(BF16)

    inside = jnp.where(lane >= 32, jnp.where(lane < 96, 1.0, 0.0), 0.0)
    mu = jnp.sum(s0 * inside, axis=-1, keepdims=True) * (1.0 / IDX_DIM)
    xc = (s0 - mu) * inside
    var = jnp.sum(xc * xc, axis=-1, keepdims=True) * (1.0 / IDX_DIM)
    y = xc * lax.rsqrt(var + NORM_EPS) * lng_ref[...] + lnb_ref[...]
    rot = jnp.where(lane < 40, pltpu.roll(y, LANES - 8, 1), pltpu.roll(y, 8, 1))
    ki_ref[...] = (y * ci + rot * si).astype(BF16)

    scale = HEAD_DIM ** -0.5
    w_scale = IDX_HEADS ** -0.5 * IDX_DIM ** -0.5
    for h in range(A_HEADS):
        qh = _dot(cqn, wuq_ref[h])
        qr = _dot(cqn, wuqr_ref[h])
        roped = qh * ch + qr * sh
        qabs = _dot(roped.astype(BF16), wuk_ref[h])
        qa_ref[h, :, 0:A_KV_RANK] = (qabs * scale).astype(BF16)
        qa_ref[h, :, A_KV_RANK:QA_DIM] = (jnp.where(lane < ROPE_DIM, roped, 0.0) * scale).astype(BF16)
        qih = _dot(cqn, widx_ref[h])
        qir = _dot(cqn, widxr_ref[h])
        qi_ref[h] = (qih * ci + qir * si).astype(BF16)
        wi_ref[h] = jnp.broadcast_to(s1[:, h:h + 1], (tm, LANES)) * w_scale
    sg_ref[...] = jax.nn.sigmoid(s1)


def _dsa_prep(z, tabh, tabi, gq, gkv, wuq, wuqr, wuk, widx, widxr, lng, lnb):
    nb, t, _ = z.shape
    tm = 256
    full = lambda a: pl.BlockSpec(a.shape, lambda b, i: (0,) * a.ndim)
    return pl.pallas_call(
        _dsa_prep_body,
        grid=(nb, t // tm),
        in_specs=[pl.BlockSpec((None, tm, A_Q_RANK), lambda b, i: (b, i, Z_CQ // A_Q_RANK)),
                  pl.BlockSpec((None, tm, A_KV_RANK), lambda b, i: (b, i, Z_CKV // A_KV_RANK)),
                  pl.BlockSpec((None, tm, 256), lambda b, i: (b, i, Z_SMALL // 256)),
                  pl.BlockSpec((None, tm, 256), lambda b, i: (b, i, 0)),
                  pl.BlockSpec((None, tm, 256), lambda b, i: (b, i, 0)),
                  full(gq), full(gkv), full(wuq), full(wuqr), full(wuk), full(widx), full(widxr),
                  full(lng), full(lnb)],
        out_specs=[pl.BlockSpec((None, A_HEADS, tm, QA_DIM), lambda b, i: (b, 0, i, 0)),
                   pl.BlockSpec((None, tm, QA_DIM), lambda b, i: (b, i, 0)),
                   pl.BlockSpec((None, IDX_HEADS, tm, LANES), lambda b, i: (b, 0, i, 0)),
                   pl.BlockSpec((None, tm, LANES), lambda b, i: (b, i, 0)),
                   pl.BlockSpec((None, IDX_HEADS, tm, LANES), lambda b, i: (b, 0, i, 0)),
                   pl.BlockSpec((None, tm, LANES), lambda b, i: (b, i, 0))],
        out_shape=[jax.ShapeDtypeStruct((nb, A_HEADS, t, QA_DIM), BF16),
                   jax.ShapeDtypeStruct((nb, t, QA_DIM), BF16),
                   jax.ShapeDtypeStruct((nb, IDX_HEADS, t, LANES), BF16),
                   jax.ShapeDtypeStruct((nb, t, LANES), BF16),
                   jax.ShapeDtypeStruct((nb, IDX_HEADS, t, LANES), F32),
                   jax.ShapeDtypeStruct((nb, t, LANES), F32)],
        compiler_params=_cparams(("parallel", "parallel")),
    )(z, z, z, tabh, tabi, gq, gkv, wuq, wuqr, wuk, widx, widxr, lng, lnb)


def _fold_lanes(x, op):
    out = x[:, 0:LANES]
    for c in range(1, x.shape[1] // LANES):
        out = op(out, x[:, c * LANES:(c + 1) * LANES])
    return out


def _dsa_attn_body(qi_ref, wi_ref, ki_ref, qa_ref, ka_ref, wuv_ref, o_ref,
                   key_ref, thr_ref, acc_ref, m_ref, l_ref, *, topk, idx_bits):
    qb = DSA_QB
    rows = A_HEADS * qb
    q0 = pl.program_id(1) * qb
    nt = (q0 + qb + KEY_TILE - 1) // KEY_TILE
    tq = q0 + lax.broadcasted_iota(I32, (qb, KEY_TILE), 0)
    lane = lax.broadcasted_iota(I32, (qb, KEY_TILE), 1)

    def score_body(kt, carry):
        smin, smax = carry
        k0 = pl.multiple_of(kt * KEY_TILE, KEY_TILE)
        ki = ki_ref[pl.ds(k0, KEY_TILE), :]
        s = None
        for h in range(IDX_HEADS):
            r = jnp.maximum(_dot_nt(qi_ref[h], ki), 0.0) * _lane_tile(wi_ref[h], KEY_TILE // LANES)
            s = r if s is None else s + r
        s = s + 0.0
        causal = k0 + lane <= tq
        key_ref[kt] = _f32_key(jnp.where(causal, s, -jnp.inf))
        smin = jnp.minimum(smin, _fold_lanes(jnp.where(causal, s, jnp.inf), jnp.minimum))
        smax = jnp.maximum(smax, _fold_lanes(jnp.where(causal, s, -jnp.inf), jnp.maximum))
        return smin, smax

    smin, smax = lax.fori_loop(0, nt, score_body,
                               (jnp.full((qb, LANES), jnp.inf, F32), jnp.full((qb, LANES), -jnp.inf, F32)))
    _topk_select(key_ref, thr_ref, nt, topk, idx_bits,
                 jnp.min(smin, axis=1, keepdims=True), jnp.max(smax, axis=1, keepdims=True))
    thr = thr_ref[:, 0:1]

    m_ref[...] = jnp.full((rows, LANES), NEG_BIG, F32)
    l_ref[...] = jnp.zeros((rows, LANES), F32)
    acc_ref[...] = jnp.zeros((rows, A_KV_RANK), F32)

    def attn_body(kt, carry):
        k0 = pl.multiple_of(kt * KEY_TILE, KEY_TILE)
        ka = ka_ref[pl.ds(k0, KEY_TILE), :]
        val = ka[:, 0:A_KV_RANK]
        sel = jnp.where(key_ref[kt] > thr, k0 + lane, tq + 1) <= tq
        bias = jnp.where(sel, 0.0, NEG_BIG)
        s = _dot_nt(qa_ref[...].reshape(rows, QA_DIM), ka) + _row_tile(bias, A_HEADS)
        m_old = m_ref[...]
        m_new = jnp.maximum(m_old, jnp.max(s, axis=1, keepdims=True))
        p = jnp.exp(s - _lane_tile(m_new, KEY_TILE // LANES))
        alpha = jnp.exp(m_old - m_new)
        l_ref[...] = l_ref[...] * alpha + jnp.sum(p, axis=1, keepdims=True)
        pb = p.astype(BF16)
        half = rows // ATTN_CHUNKS
        for c in range(ATTN_CHUNKS):
            r = slice(c * half, (c + 1) * half)
            acc_ref[r] = acc_ref[r] * _lane_tile(alpha[r], A_KV_RANK // LANES) + _dot(pb[r], val)
        m_ref[...] = m_new
        return carry

    lax.fori_loop(0, nt, attn_body, 0)

    inv = 1.0 / jnp.maximum(l_ref[...], 1e-30)
    o_lat = (acc_ref[...] * _lane_tile(inv, A_KV_RANK // LANES)).astype(BF16)
    for h in range(A_HEADS):
        o_ref[:, h * HEAD_DIM:(h + 1) * HEAD_DIM] = _dot(o_lat[h * qb:(h + 1) * qb], wuv_ref[h]).astype(BF16)


def _dsa_attn(qi, wi, ki, qa, ka, wuv):
    nb, _, t, _ = qa.shape
    topk = min(IDX_TOPK_MAX, t // 4)
    idx_bits = int(np.ceil(np.log2(t)))
    qb = DSA_QB
    body = functools.partial(_dsa_attn_body, topk=topk, idx_bits=idx_bits)
    return pl.pallas_call(
        body,
        grid=(nb, t // qb),
        in_specs=[pl.BlockSpec((None, IDX_HEADS, qb, LANES), lambda b, i: (b, 0, i, 0)),
                  pl.BlockSpec((None, IDX_HEADS, qb, LANES), lambda b, i: (b, 0, i, 0)),
                  pl.BlockSpec((None, t, LANES), lambda b, i: (b, 0, 0)),
                  pl.BlockSpec((None, A_HEADS, qb, QA_DIM), lambda b, i: (b, 0, i, 0)),
                  pl.BlockSpec((None, t, QA_DIM), lambda b, i: (b, 0, 0)),
                  pl.BlockSpec(wuv.shape, lambda b, i: (0, 0, 0))],
        out_specs=pl.BlockSpec((None, qb, A_HEADS * HEAD_DIM), lambda b, i: (b, i, 0)),
        out_shape=jax.ShapeDtypeStruct((nb, t, A_HEADS * HEAD_DIM), BF16),
        scratch_shapes=[pltpu.VMEM((t // KEY_TILE, qb, KEY_TILE), I32),
                        pltpu.VMEM((qb, LANES), I32),
                        pltpu.VMEM((A_HEADS * qb, A_KV_RANK), F32),
                        pltpu.VMEM((A_HEADS * qb, LANES), F32),
                        pltpu.VMEM((A_HEADS * qb, LANES), F32)],
        compiler_params=_cparams(("parallel", "parallel")),
    )(qi, wi, ki, qa, ka, wuv)


def _nsa_prep_body(bq_ref, bkv_ref, tabh_ref, qb_ref, kc_ref, vc_ref, ks_ref, vs_ref, kw_ref, vw_ref):
    tm = bq_ref.shape[0]
    lane = lax.broadcasted_iota(I32, (tm, LANES), 1)
    ch, sh = tabh_ref[:, 0:LANES], tabh_ref[:, LANES:2 * LANES]

    def rope(x):
        rot = jnp.where(lane < ROPE_DIM // 2, pltpu.roll(x, LANES - ROPE_DIM // 2, 1),
                        pltpu.roll(x, ROPE_DIM // 2, 1))
        return x * ch + rot * sh

    scale = HEAD_DIM ** -0.5
    for h in range(B_HEADS):
        sl = slice(h * HEAD_DIM, (h + 1) * HEAD_DIM)
        qb_ref[:, sl] = (rope(bq_ref[:, sl]) * scale).astype(BF16)
    outs = (kc_ref, vc_ref, ks_ref, vs_ref, kw_ref, vw_ref)
    for kind in range(6):
        for g in range(B_GROUPS):
            c0 = (kind * B_GROUPS + g) * HEAD_DIM
            v = bkv_ref[:, c0:c0 + HEAD_DIM]
            if kind % 2 == 0:
                v = rope(v)
            outs[kind][g] = v.astype(BF16)


def _nsa_prep(z, tabh):
    nb, t, _ = z.shape
    tm = 256
    kv_spec = pl.BlockSpec((None, B_GROUPS, tm, HEAD_DIM), lambda b, i: (b, 0, i, 0))
    kv_shape = jax.ShapeDtypeStruct((nb, B_GROUPS, t, HEAD_DIM), BF16)
    return pl.pallas_call(
        _nsa_prep_body,
        grid=(nb, t // tm),
        in_specs=[pl.BlockSpec((None, tm, 1024), lambda b, i: (b, i, Z_BQ // 1024)),
                  pl.BlockSpec((None, tm, 1536), lambda b, i: (b, i, Z_BKV // 1536)),
                  pl.BlockSpec((None, tm, 256), lambda b, i: (b, i, 0))],
        out_specs=[pl.BlockSpec((None, tm, 1024), lambda b, i: (b, i, 0))] + [kv_spec] * 6,
        out_shape=[jax.ShapeDtypeStruct((nb, t, 1024), BF16)] + [kv_shape] * 6,
        compiler_params=_cparams(("parallel", "parallel")),
    )(z, z, tabh)


def _compress_body(ch_ref, pe_ref, w1_ref, w2_ref, o_ref):
    nc = ch_ref.shape[0]
    half = CMP_STRIDE * HEAD_DIM
    x = ch_ref[...]
    ha = _dot(x, w1_ref[0:half, :])
    hb = _dot(x, w1_ref[half:2 * half, :])
    bias = _dot(pe_ref[...], w1_ref[...])[0:1]
    pre = ha + pltpu.roll(hb, nc - 1, 0) + bias
    hid = jax.nn.gelu(pre, approximate=True)
    o_ref[...] = _dot(hid.astype(BF16), w2_ref[...]).astype(BF16)


def _compress(tok, pe, w1, w2):
    nb, ng, t, d = tok.shape
    nc = t // CMP_STRIDE
    chunks = tok.reshape(nb, ng, nc, CMP_STRIDE * d)
    pe8 = jnp.broadcast_to(pe.reshape(1, CMP_LEN * d), (8, CMP_LEN * d)).astype(BF16)
    return pl.pallas_call(
        _compress_body,
        grid=(nb, ng),
        in_specs=[pl.BlockSpec((None, None, nc, CMP_STRIDE * d), lambda b, g: (b, g, 0, 0)),
                  pl.BlockSpec(pe8.shape, lambda b, g: (0, 0)),
                  pl.BlockSpec(w1.shape, lambda b, g: (0, 0)),
                  pl.BlockSpec(w2.shape, lambda b, g: (0, 0))],
        out_specs=pl.BlockSpec((None, None, nc, d), lambda b, g: (b, g, 0, 0)),
        out_shape=jax.ShapeDtypeStruct((nb, ng, nc, d), BF16),
        compiler_params=_cparams(("parallel", "parallel")),
    )(chunks, pe8, w1.astype(BF16), w2.astype(BF16))


def _nsa_attn_body(q_ref, kc_ref, vc_ref, ks_ref, vs_ref, kw_ref, vw_ref, gate_ref, ov_ref, ex_ref,
                   o_ref, key_ref, thr_ref, acc_ref, m_ref, l_ref, *, n_cmp, n_sel):
    qb = NSA_QB
    rows = B_HPG * qb
    nc = kc_ref.shape[0]
    q0 = pl.program_id(2) * qb
    qs = jnp.concatenate([q_ref[:, j * HEAD_DIM:(j + 1) * HEAD_DIM] for j in range(B_HPG)], axis=0)

    tq_c = q0 + lax.broadcasted_iota(I32, (qb, nc), 0)
    n_id = lax.broadcasted_iota(I32, (qb, nc), 1)
    c_ok = jnp.where(n_id < n_cmp, n_id * CMP_STRIDE + (CMP_LEN - 1), tq_c + 1) <= tq_c
    c_bias = _row_tile(jnp.where(c_ok, 0.0, NEG_BIG), B_HPG)
    c_keep = _row_tile(jnp.where(c_ok, 1.0, 0.0), B_HPG)
    sc = _dot_nt(qs, kc_ref[...]) + c_bias
    e = jnp.exp(sc - jnp.max(sc, axis=1, keepdims=True)) * c_keep
    pc = e / jnp.maximum(jnp.sum(e, axis=1, keepdims=True), 1e-30)
    oc = _dot(pc.astype(BF16), vc_ref[...])

    pc4 = pc[0:qb]
    for j in range(1, B_HPG):
        pc4 = pc4 + pc[j * qb:(j + 1) * qb]
    p_hi, p_mid, p_lo = _split3(pc4)
    ov = ov_ref[...]
    imp = _dot(p_hi, ov) + _dot(p_mid, ov) + _dot(p_lo, ov)
    tq_b = q0 + lax.broadcasted_iota(I32, (qb, LANES), 0)
    blk = lax.broadcasted_iota(I32, (qb, LANES), 1)
    cur = tq_b // SLC_LEN
    d_cur = cur - blk
    forced = jnp.where(blk == 0, 1.0, jnp.where(d_cur == 0, 1.0, jnp.where(d_cur == 1, 1.0, 0.0)))
    admissible = blk * SLC_LEN <= tq_b
    free = jnp.where(admissible, jnp.where(forced > 0.5, -1.0, imp), -1.0)
    imp = jnp.where(forced > 0.5, jnp.inf, imp + 0.0)
    key_ref[0] = _f32_key(jnp.where(admissible, imp, -jnp.inf))
    _topk_select(key_ref, thr_ref, 1, n_sel, 7,
                 jnp.full((qb, 1), -1.0, F32), jnp.max(free, axis=1, keepdims=True))
    selm = jnp.where(key_ref[0] > thr_ref[:, 0:1], 1.0, 0.0).astype(BF16)

    nt = (q0 + qb + KEY_TILE - 1) // KEY_TILE
    tq_k = q0 + lax.broadcasted_iota(I32, (qb, KEY_TILE), 0)
    lane_k = lax.broadcasted_iota(I32, (qb, KEY_TILE), 1)
    m_ref[...] = jnp.full((rows, LANES), NEG_BIG, F32)
    l_ref[...] = jnp.zeros((rows, LANES), F32)
    acc_ref[...] = jnp.zeros((rows, HEAD_DIM), F32)

    def sel_body(kt, carry):
        k0 = pl.multiple_of(kt * KEY_TILE, KEY_TILE)
        chosen = _dot(selm, ex_ref[kt])
        ok = jnp.where(chosen > 0.5, k0 + lane_k, tq_k + 1) <= tq_k
        bias = jnp.where(ok, 0.0, NEG_BIG)
        ks = ks_ref[pl.ds(k0, KEY_TILE), :]
        vs = vs_ref[pl.ds(k0, KEY_TILE), :]
        s = _dot_nt(qs, ks) + _row_tile(bias, B_HPG)
        m_old = m_ref[...]
        m_new = jnp.maximum(m_old, jnp.max(s, axis=1, keepdims=True))
        p = jnp.exp(s - _lane_tile(m_new, KEY_TILE // LANES))
        alpha = jnp.exp(m_old - m_new)
        l_ref[...] = l_ref[...] * alpha + jnp.sum(p, axis=1, keepdims=True)
        acc_ref[...] = acc_ref[...] * alpha + _dot(p.astype(BF16), vs)
        m_ref[...] = m_new
        return carry

    lax.fori_loop(0, nt, sel_body, 0)
    os_ = acc_ref[...] / jnp.maximum(l_ref[...], 1e-30)

    span = WINDOW + qb
    start = pl.multiple_of(jnp.maximum(q0 - WINDOW, 0), qb)
    tq_w = q0 + lax.broadcasted_iota(I32, (qb, span), 0)
    wpos = start + lax.broadcasted_iota(I32, (qb, span), 1)
    w_ok = jnp.where(wpos > tq_w - WINDOW, wpos, tq_w + 1) <= tq_w
    w_bias = _row_tile(jnp.where(w_ok, 0.0, NEG_BIG), B_HPG)
    sw = _dot_nt(qs, kw_ref[pl.ds(start, span), :]) + w_bias
    ew = jnp.exp(sw - jnp.max(sw, axis=1, keepdims=True))
    pw = ew / jnp.maximum(jnp.sum(ew, axis=1, keepdims=True), 1e-30)
    ow = _dot(pw.astype(BF16), vw_ref[pl.ds(start, span), :])

    gate = gate_ref[...]
    for j in range(B_HPG):
        r = slice(j * qb, (j + 1) * qb)
        res = (gate[:, 3 * j:3 * j + 1] * oc[r] + gate[:, 3 * j + 1:3 * j + 2] * os_[r]
               + gate[:, 3 * j + 2:3 * j + 3] * ow[r])
        o_ref[:, j * HEAD_DIM:(j + 1) * HEAD_DIM] = res.astype(BF16)


def _nsa_attn(qb_all, kc, vc, ks, vs, kw, vw, gates, ov, ex):
    nb, t, _ = qb_all.shape
    nc = kc.shape[2]
    n_cmp = (t - CMP_LEN) // CMP_STRIDE + 1
    n_sel = min(SLC_TOPN, t // SLC_LEN)
    qb = NSA_QB
    body = functools.partial(_nsa_attn_body, n_cmp=n_cmp, n_sel=n_sel)
    res_c = pl.BlockSpec((None, None, nc, HEAD_DIM), lambda b, g, i: (b, g, 0, 0))
    res_t = pl.BlockSpec((None, None, t, HEAD_DIM), lambda b, g, i: (b, g, 0, 0))
    return pl.pallas_call(
        body,
        grid=(nb, B_GROUPS, t // qb),
        in_specs=[pl.BlockSpec((None, qb, B_HPG * HEAD_DIM), lambda b, g, i: (b, i, g)),
                  res_c, res_c, res_t, res_t, res_t, res_t,
                  pl.BlockSpec((None, None, qb, 3 * B_HPG), lambda b, g, i: (b, g, i, 0)),
                  pl.BlockSpec(ov.shape, lambda b, g, i: (0, 0)),
                  pl.BlockSpec(ex.shape, lambda b, g, i: (0, 0, 0))],
        out_specs=pl.BlockSpec((None, qb, B_HPG * HEAD_DIM), lambda b, g, i: (b, i, g)),
        out_shape=jax.ShapeDtypeStruct((nb, t, B_HEADS * HEAD_DIM), BF16),
        scratch_shapes=[pltpu.VMEM((1, qb, LANES), I32),
                        pltpu.VMEM((qb, LANES), I32),
                        pltpu.VMEM((B_HPG * qb, HEAD_DIM), F32),
                        pltpu.VMEM((B_HPG * qb, LANES), F32),
                        pltpu.VMEM((B_HPG * qb, LANES), F32)],
        compiler_params=_cparams(("parallel", "parallel", "parallel")),
    )(qb_all, kc, vc, ks, vs, kw, vw, gates, ov, ex)


def _merge_body(oa_ref, ob_ref, wa_ref, wb_ref, ga_ref, gb_ref, o_ref):
    a = _dot(oa_ref[...], wa_ref[...])
    b = _dot(ob_ref[...], wb_ref[...])
    o_ref[...] = (jax.nn.sigmoid(ga_ref[...]) * a + jax.nn.sigmoid(gb_ref[...]) * b).astype(BF16)


def _merge(o_a, o_b, w_a, w_b, z):
    nb, t, da = o_a.shape
    d = w_a.shape[1]
    tm, tn = 512, 1024
    return pl.pallas_call(
        _merge_body,
        grid=(nb, t // tm, d // tn),
        in_specs=[pl.BlockSpec((None, tm, da), lambda b, i, j: (b, i, 0)),
                  pl.BlockSpec((None, tm, da), lambda b, i, j: (b, i, 0)),
                  pl.BlockSpec((da, tn), lambda b, i, j: (0, j)),
                  pl.BlockSpec((da, tn), lambda b, i, j: (0, j)),
                  pl.BlockSpec((None, tm, tn), lambda b, i, j: (b, i, j)),
                  pl.BlockSpec((None, tm, tn), lambda b, i, j: (b, i, d // tn + j))],
        out_specs=pl.BlockSpec((None, tm, tn), lambda b, i, j: (b, i, j)),
        out_shape=jax.ShapeDtypeStruct((nb, t, d), BF16),
        compiler_params=_cparams(("parallel", "parallel", "parallel")),
    )(o_a, o_b, w_a, w_b, z, z)


def _outproj_body(mg_ref, wo_ref, x_ref, gpost_ref, gt_ref, gpre_ref, sc_ref, sh_ref,
                  wr_ref, br_ref, x1_ref, h_ref, idx_ref, gate_ref):
    tm = x_ref.shape[0]
    y = _dot(mg_ref[...], wo_ref[...])
    x1 = x_ref[...] + gt_ref[...] * _rms(y, gpost_ref[...])
    x1_ref[...] = x1
    h = _rms(x1, gpre_ref[...]) * (1.0 + sc_ref[...]) + sh_ref[...]
    h_ref[...] = h
    h_hi = h.astype(BF16)
    h_lo = (h - h_hi.astype(F32)).astype(BF16)
    logits = (_dot(h_hi, wr_ref[0]) + (_dot(h_hi, wr_ref[1]) + _dot(h_lo, wr_ref[0]))) + br_ref[...]
    lane = lax.broadcasted_iota(I32, (tm, N_EXPERTS), 1).astype(F32)
    vals, idxs = [], []
    cur = logits
    for _ in range(TOP_K):
        m = jnp.max(cur, axis=1, keepdims=True)
        am = jnp.min(jnp.where(cur == m, lane, float(N_EXPERTS)), axis=1, keepdims=True)
        vals.append(m)
        idxs.append(am)
        cur = jnp.where(lane == am, -jnp.inf, cur)
    es = [jnp.exp(v - vals[0]) for v in vals]
    tot = es[0] + es[1] + es[2] + es[3]
    for k in range(TOP_K):
        idx_ref[:, k:k + 1] = idxs[k].astype(I32)
        gate_ref[:, k:k + 1] = es[k] / tot


def _outproj(merged, w_out, x, g_post, gt1, g_pre, sc2, sh2, wr, br):
    nb, t, d = x.shape
    tm = 256
    vec = lambda: pl.BlockSpec((1, d), lambda b, i: (0, 0))
    mod = lambda: pl.BlockSpec((None, 1, d), lambda b, i: (b, 0, 0))
    row = lambda w: pl.BlockSpec((None, tm, w), lambda b, i: (b, i, 0))
    return pl.pallas_call(
        _outproj_body,
        grid=(nb, t // tm),
        in_specs=[row(d), pl.BlockSpec((d, d), lambda b, i: (0, 0)), row(d), vec(), mod(), vec(), mod(), mod(),
                  pl.BlockSpec(wr.shape, lambda b, i: (0, 0, 0)),
                  pl.BlockSpec((1, N_EXPERTS), lambda b, i: (0, 0))],
        out_specs=[row(d), row(d), row(TOP_K), row(TOP_K)],
        out_shape=[jax.ShapeDtypeStruct((nb, t, d), F32),
                   jax.ShapeDtypeStruct((nb, t, d), F32),
                   jax.ShapeDtypeStruct((nb, t, TOP_K), I32),
                   jax.ShapeDtypeStruct((nb, t, TOP_K), F32)],
        compiler_params=_cparams(("parallel", "parallel")),
    )(merged, w_out, x, g_post[None], gt1, g_pre[None], sc2, sh2, wr, br[None])


def _gmm_body(be_ref, nu_ref, tokc_ref, tokn_ref, h_ref, wg_ref, wu_ref, bg_ref, bu_ref, wd_ref, bd_ref,
              y_ref, xf_ref, xb_ref, sem, *, ff_steps):
    i, j = pl.program_id(0), pl.program_id(1)
    n_used = nu_ref[0]
    slot = i % 2
    last = ff_steps - 1
    rows_per_step = ROW_BLK // ff_steps

    def row_copy(tok_ref, r, s):
        return pltpu.make_async_copy(h_ref.at[pl.ds(tok_ref[0, 0, r], 1), :],
                                     xf_ref.at[s, pl.ds(r, 1), :], sem.at[s])

    def wait_slot(s):
        pltpu.make_async_copy(h_ref.at[pl.ds(0, ROW_BLK), :], xf_ref.at[s], sem.at[s]).wait()

    @pl.when(j == 0)
    def _():
        @pl.when(i == 0)
        def _():
            def body(r, carry):
                row_copy(tokc_ref, r, 0).start()
                return carry

            lax.fori_loop(0, ROW_BLK, body, 0)

        @pl.when(i <= n_used)
        def _():
            wait_slot(slot)

        @pl.when(i < n_used)
        def _():
            xb_ref[...] = xf_ref[slot].astype(BF16)

    @pl.when(i < n_used)
    def _():
        x = xb_ref[...]
        g = jnp.minimum(_dot(x, wg_ref[...]) + bg_ref[...], SWIGLU_LIMIT)
        u = jnp.clip(_dot(x, wu_ref[...]) + bu_ref[...], -SWIGLU_LIMIT, SWIGLU_LIMIT)
        act = g * jax.nn.sigmoid(SWIGLU_ALPHA * g) * (u + 1.0)
        part = _dot(act.astype(BF16), wd_ref[...])
        for r in range(rows_per_step):
            row_copy(tokn_ref, j * rows_per_step + r, 1 - slot).start()

        @pl.when(j == 0)
        def _():
            y_ref[...] = part + bd_ref[...]

        @pl.when(j > 0)
        def _():
            y_ref[...] = y_ref[...] + part

    @pl.when(jnp.logical_and(i >= n_used, j == last))
    def _():
        y_ref[...] = jnp.zeros(y_ref.shape, F32)

    @pl.when(jnp.logical_and(jnp.logical_and(i == pl.num_programs(0) - 1, j == last), i < n_used))
    def _():
        wait_slot(1 - slot)


def _gmm(blk_e, n_used, row_tok, h, wgu, bgu, wd, bd):
    d = h.shape[1]
    ff = wd.shape[1]
    p = row_tok.shape[0]
    nblk = p // ROW_BLK
    nft = ff // FF_TILE
    tok3 = row_tok.reshape(nblk, 1, ROW_BLK)

    def expert(i, be, nu):
        return be[jnp.minimum(i, nu[0] - 1)]

    def tile(i, j, nu):
        return jnp.where(i < nu[0], j, nft - 1)

    grid_spec = pltpu.PrefetchScalarGridSpec(
        num_scalar_prefetch=2,
        grid=(nblk, nft),
        in_specs=[pl.BlockSpec((1, 1, ROW_BLK), lambda i, j, be, nu: (i, 0, 0), memory_space=pltpu.SMEM),
                  pl.BlockSpec((1, 1, ROW_BLK), lambda i, j, be, nu: (jnp.minimum(i + 1, nblk - 1), 0, 0),
                               memory_space=pltpu.SMEM),
                  pl.BlockSpec(memory_space=pl.ANY),
                  pl.BlockSpec((None, d, FF_TILE), lambda i, j, be, nu: (expert(i, be, nu), 0, tile(i, j, nu))),
                  pl.BlockSpec((None, d, FF_TILE),
                               lambda i, j, be, nu: (expert(i, be, nu), 0, nft + tile(i, j, nu))),
                  pl.BlockSpec((None, 1, FF_TILE), lambda i, j, be, nu: (expert(i, be, nu), 0, tile(i, j, nu))),
                  pl.BlockSpec((None, 1, FF_TILE),
                               lambda i, j, be, nu: (expert(i, be, nu), 0, nft + tile(i, j, nu))),
                  pl.BlockSpec((None, FF_TILE, d), lambda i, j, be, nu: (expert(i, be, nu), tile(i, j, nu), 0)),
                  pl.BlockSpec((None, 1, d), lambda i, j, be, nu: (expert(i, be, nu), 0, 0))],
        out_specs=pl.BlockSpec((ROW_BLK, d), lambda i, j, be, nu: (i, 0)),
        scratch_shapes=[pltpu.VMEM((2, ROW_BLK, d), F32),
                        pltpu.VMEM((ROW_BLK, d), BF16),
                        pltpu.SemaphoreType.DMA((2,))],
    )
    assert ROW_BLK % nft == 0
    return pl.pallas_call(
        functools.partial(_gmm_body, ff_steps=nft),
        grid_spec=grid_spec,
        out_shape=jax.ShapeDtypeStruct((p, d), F32),
        compiler_params=_cparams(("arbitrary", "arbitrary")),
    )(blk_e, n_used, tok3, tok3, h, wgu, wgu, bgu, bgu, wd, bd)


def _combine_body(destc_ref, destn_ref, ys_ref, gate_ref, x1_ref, g_ref, gt_ref, o_ref, buf, sem):
    tm = CMB_TOK
    i = pl.program_id(0)
    slot = i % 2

    def issue(dest_ref, s):
        def body(r, carry):
            pltpu.make_async_copy(ys_ref.at[pl.ds(dest_ref[0, 0, r], 1), :],
                                  buf.at[s, pl.ds(r, 1), :], sem.at[s]).start()
            return carry

        lax.fori_loop(0, TOP_K * tm, body, 0, unroll=16)

    @pl.when(i == 0)
    def _():
        issue(destc_ref, 0)

    pltpu.make_async_copy(ys_ref.at[pl.ds(0, TOP_K * tm), :], buf.at[slot], sem.at[slot]).wait()

    @pl.when(i + 1 < pl.num_programs(0))
    def _():
        issue(destn_ref, 1 - slot)

    gate = gate_ref[...]
    y = buf[slot, 0:tm] * gate[:, 0:1]
    for k in range(1, TOP_K):
        y = y + buf[slot, k * tm:(k + 1) * tm] * gate[:, k:k + 1]
    o_ref[...] = x1_ref[...] + gt_ref[...] * _rms(y, g_ref[...])


def _combine(dest, ys, gates, x1, g_post, gt2):
    nb, t, d = x1.shape
    n = nb * t
    tm = CMB_TOK
    steps = n // tm
    dest_t = dest.reshape(steps, tm, TOP_K).transpose(0, 2, 1).reshape(steps, 1, TOP_K * tm)
    out = pl.pallas_call(
        _combine_body,
        grid=(steps,),
        in_specs=[pl.BlockSpec((1, 1, TOP_K * tm), lambda i: (i, 0, 0), memory_space=pltpu.SMEM),
                  pl.BlockSpec((1, 1, TOP_K * tm), lambda i: (jnp.minimum(i + 1, steps - 1), 0, 0),
                               memory_space=pltpu.SMEM),
                  pl.BlockSpec(memory_space=pl.ANY),
                  pl.BlockSpec((tm, TOP_K), lambda i: (i, 0)),
                  pl.BlockSpec((tm, d), lambda i: (i, 0)),
                  pl.BlockSpec((1, d), lambda i: (0, 0)),
                  pl.BlockSpec((None, 1, d), lambda i: ((i * tm) // t, 0, 0))],
        out_specs=pl.BlockSpec((tm, d), lambda i: (i, 0)),
        out_shape=jax.ShapeDtypeStruct((n, d), F32),
        scratch_shapes=[pltpu.VMEM((2, TOP_K * tm, d), F32), pltpu.SemaphoreType.DMA((2,))],
        compiler_params=_cparams(("arbitrary",)),
    )(dest_t, dest_t, ys, gates, x1.reshape(n, d), g_post[None], gt2)
    return out.reshape(nb, t, d)


def _route(top_idx):
    n = top_idx.shape[0]
    nk = n * TOP_K
    e = top_idx.reshape(nk)
    onehot = (e[:, None] == jnp.arange(N_EXPERTS, dtype=I32)[None, :]).astype(I32)
    csum = jnp.cumsum(onehot, axis=0)
    rank = jnp.sum((csum - onehot) * onehot, axis=1)
    counts = csum[-1]
    padded = (counts + ROW_BLK - 1) // ROW_BLK * ROW_BLK
    pend = jnp.cumsum(padded)
    pstart = pend - padded
    dest = (pstart[e] + rank).astype(I32)
    nblk = nk // ROW_BLK + N_EXPERTS
    tok = jnp.arange(nk, dtype=I32) // TOP_K
    row_tok = jnp.zeros((nblk * ROW_BLK,), I32).at[dest].set(tok, unique_indices=True)
    blk_start = jnp.arange(nblk, dtype=I32) * ROW_BLK
    blk_e = jnp.minimum(jnp.sum((pend[None, :] <= blk_start[:, None]).astype(I32), axis=1), N_EXPERTS - 1)
    n_used = (pend[-1:] // ROW_BLK).astype(I32)
    return dest.reshape(n, TOP_K), row_tok, blk_e.astype(I32), n_used


def _rot_cols(w, half):
    z = jnp.zeros_like(w)
    z = z.at[..., 0:half].set(w[..., half:2 * half])
    return z.at[..., half:2 * half].set(w[..., 0:half])


def _prep_w_in(w_in):
    d = w_in.shape[0]
    offs = np.cumsum([0, A_Q_RANK, A_KV_RANK, ROPE_DIM, IDX_DIM, IDX_HEADS, B_HEADS * HEAD_DIM,
                      6 * B_GROUPS * HEAD_DIM, 3 * B_HEADS, 2 * d])
    cq, ckv, krope, kidx, widx, bq, bkv, bgate, mgate = [w_in[:, int(offs[i]):int(offs[i + 1])] for i in range(9)]
    krope_rot = jnp.concatenate([krope[:, ROPE_DIM // 2:], krope[:, :ROPE_DIM // 2]], axis=1)
    small = jnp.concatenate([krope, kidx, krope_rot, widx, bgate,
                             jnp.zeros((d, 256 - 160), w_in.dtype)], axis=1)
    w = jnp.concatenate([mgate, bq, cq, ckv, small, bkv], axis=1)
    assert w.shape[1] == Z_COLS
    return w.astype(BF16)


def _rope_tables(positions):
    def tab(rot_dim, lo, width):
        inv = jnp.power(ROPE_THETA, -jnp.arange(0, rot_dim, 2, dtype=F32) / rot_dim)
        ang = positions.astype(F32)[..., None] * inv
        c, s = jnp.cos(ang), jnp.sin(ang)
        shape = positions.shape
        cos_t = jnp.concatenate([jnp.zeros(shape + (lo,), F32), c, c,
                                 jnp.ones(shape + (width - rot_dim,), F32),
                                 jnp.zeros(shape + (LANES - lo - width,), F32)], axis=-1)
        sin_t = jnp.concatenate([jnp.zeros(shape + (lo,), F32), -s, s,
                                 jnp.zeros(shape + (LANES - lo - rot_dim,), F32)], axis=-1)
        return jnp.concatenate([cos_t, sin_t], axis=-1)

    return tab(ROPE_DIM, 0, LANES), tab(IDX_ROPE, 32, IDX_DIM)


def _select_tables(t):
    n_cmp = (t - CMP_LEN) // CMP_STRIDE + 1
    nc = t // CMP_STRIDE
    n = np.arange(nc)[:, None]
    m = np.arange(LANES)[None, :]
    ov = (n * CMP_STRIDE < m * SLC_LEN + SLC_LEN) & (n * CMP_STRIDE + CMP_LEN - 1 >= m * SLC_LEN) & (n < n_cmp)
    key_blk = (np.arange(t) // SLC_LEN).reshape(t // KEY_TILE, 1, KEY_TILE)
    ex = key_blk == np.arange(LANES).reshape(1, LANES, 1)
    return jnp.asarray(ov, BF16), jnp.asarray(ex, BF16)


def kernel(x, c, positions, w_ada, b_ada, g_pre_mix, g_post_mix, g_pre_ffn, g_post_ffn, w_in, a_cq_norm, a_ckv_norm, a_w_uq, a_w_uk, a_w_uv, idx_w_q, idx_k_norm_g, idx_k_norm_b, cmp_k_pe, cmp_k_w1, cmp_k_w2, cmp_v_pe, cmp_v_w1, cmp_v_w2, w_br_a, w_br_b, w_out, w_router, b_router, w_gate_up, b_gate_up, w_down, b_down):
    nb, t, d = x.shape
    n = nb * t
    assert t % KEY_TILE == 0 and t // SLC_LEN <= LANES and d % LANES == 0

    mod = _adaln(c, w_ada, b_ada)
    sh1, sc1, gt1, sh2, sc2, gt2 = [m[:, None, :] for m in jnp.split(mod, 6, axis=-1)]

    z = _inproj(x, g_pre_mix, sc1, sh1, _prep_w_in(w_in))
    tabh, tabi = _rope_tables(positions)

    wuq = jnp.transpose(a_w_uq, (1, 0, 2))
    wuk = jnp.transpose(a_w_uk, (1, 2, 0))
    wuk = jnp.concatenate([jnp.zeros((A_HEADS, ROPE_DIM, A_KV_RANK), wuk.dtype), wuk], axis=1)
    widx = jnp.transpose(idx_w_q, (1, 0, 2))
    pad_idx = lambda w: jnp.pad(w, ((0, 0), (0, 0), (32, LANES - 32 - IDX_DIM)))
    pad_vec = lambda v: jnp.pad(v, (32, LANES - 32 - IDX_DIM))[None]
    qa, ka, qi, ki, wi, sg = _dsa_prep(
        z, tabh, tabi, a_cq_norm[None], a_ckv_norm[None],
        wuq.astype(BF16), _rot_cols(wuq, ROPE_DIM // 2).astype(BF16), wuk.astype(BF16),
        pad_idx(widx).astype(BF16), pad_idx(_rot_cols(widx, IDX_ROPE // 2)).astype(BF16),
        pad_vec(idx_k_norm_g), pad_vec(idx_k_norm_b))
    o_a = _dsa_attn(qi, wi, ki, qa, ka, a_w_uv.astype(BF16))

    qb_all, kc_tok, vc_tok, ks, vs, kw, vw = _nsa_prep(z, tabh)
    kc = _compress(kc_tok, cmp_k_pe, cmp_k_w1, cmp_k_w2)
    vc = _compress(vc_tok, cmp_v_pe, cmp_v_w1, cmp_v_w2)
    gates_b = sg[:, :, IDX_HEADS:IDX_HEADS + 3 * B_HEADS].reshape(nb, t, B_GROUPS, 3 * B_HPG).transpose(0, 2, 1, 3)
    ov, ex = _select_tables(t)
    o_b = _nsa_attn(qb_all, kc, vc, ks, vs, kw, vw, gates_b, ov, ex)

    merged = _merge(o_a, o_b, w_br_a.astype(BF16), w_br_b.astype(BF16), z)

    wr_hi = w_router.astype(BF16)
    wr = jnp.stack([wr_hi, (w_router - wr_hi.astype(F32)).astype(BF16)])
    x1, h2, top_idx, gates = _outproj(merged, w_out.astype(BF16), x, g_post_mix, gt1, g_pre_ffn, sc2, sh2,
                                      wr, b_router)
    dest, row_tok, blk_e, n_used = _route(top_idx.reshape(n, TOP_K))
    ys = _gmm(blk_e, n_used, row_tok, h2.reshape(n, d), w_gate_up.astype(BF16), b_gate_up[:, None, :],
              w_down.astype(BF16), b_down[:, None, :])
    return _combine(dest, ys, gates.reshape(n, TOP_K), x1, g_post_ffn, gt2)
```

```python
import functools

import numpy as np
import jax
import jax.numpy as jnp
from jax import lax
from jax.experimental import pallas as pl
from jax.experimental.pallas import tpu as pltpu

F32 = jnp.float32
BF16 = jnp.bfloat16
I32 = jnp.int32

HEAD_DIM = 128
ROPE_DIM = 32
ROPE_THETA = 500000.0
NORM_EPS = 1e-6
Q_BLOCK = 128
A_HEADS = 8
A_Q_RANK = 512
A_KV_RANK = 256
IDX_HEADS = 8
IDX_DIM = 64
IDX_ROPE = 16
IDX_TOPK_MAX = 256
B_HEADS = 8
B_GROUPS = 2
B_HPG = 4
CMP_LEN = 32
CMP_STRIDE = 16
CMP_HIDDEN = 256
SLC_LEN = 64
SLC_TOPN = 16
WINDOW = 512
N_EXPERTS = 32
TOP_K = 4
SWIGLU_ALPHA = 1.702
SWIGLU_LIMIT = 7.0

LANES = 128
VMEM_LIMIT = 52 * 1024 * 1024

KEY_TILE = 512
QA_DIM = A_KV_RANK + LANES
ROW_BLK = 512
FF_TILE = 512
CMB_TOK = 128
ATTN_CHUNKS = 4
NSA_QB = 256
DSA_QB = 128

Z_MGATE, Z_BQ, Z_CQ, Z_CKV, Z_SMALL, Z_BKV, Z_COLS = 0, 4096, 5120, 5632, 5888, 6144, 7680

NEG_BIG = -1e30
INT_MIN = np.int32(-2147483648)
KEY_NEG_INF = np.int32(-2139095041)
BISECT_MAX_ITER = 24


def _cparams(sem):
    return pltpu.CompilerParams(dimension_semantics=sem, vmem_limit_bytes=VMEM_LIMIT)


def _dot(a, b):
    return jnp.dot(a, b, preferred_element_type=F32)


def _dot_nt(a, b):
    return lax.dot_general(a, b, (((1,), (1,)), ((), ())), preferred_element_type=F32)


def _split3(a):
    hi = a.astype(BF16)
    r1 = a - hi.astype(F32)
    mid = r1.astype(BF16)
    lo = (r1 - mid.astype(F32)).astype(BF16)
    return hi, mid, lo


def _lane_tile(a, n):
    return jnp.concatenate([a] * n, axis=1) if n > 1 else a


def _row_tile(a, n):
    return jnp.concatenate([a] * n, axis=0) if n > 1 else a


def _f32_key(s):
    bits = pltpu.bitcast(s, I32)
    return bits ^ (jnp.right_shift(bits, 31) & np.int32(0x7FFFFFFF))


def _count(key_ref, nt, pred):
    rows, width = key_ref.shape[1], key_ref.shape[2]

    def body(kt, acc):
        ind = jnp.where(pred(key_ref[kt], kt), 1.0, 0.0)
        part = ind[:, 0:LANES]
        for c in range(1, width // LANES):
            part = part + ind[:, c * LANES:(c + 1) * LANES]
        return acc + part

    acc = lax.fori_loop(0, nt, body, jnp.zeros((rows, LANES), F32))
    return jnp.sum(acc, axis=1, keepdims=True)


def _kth_key(key_ref, nt, k):
    rows = key_ref.shape[1]

    def bit_body(b, ans):
        cand = ans | jnp.left_shift(jnp.int32(1), 31 - b)
        cs = cand ^ INT_MIN
        cnt = _count(key_ref, nt, lambda kk, kt: kk >= cs)
        return jnp.where(cnt >= k, cand, ans)

    ans = lax.fori_loop(0, 32, bit_body, jnp.zeros((rows, 1), I32))
    return ans ^ INT_MIN


def _break_ties(key_ref, nt, t, k, idx_bits):
    rows, width = key_ref.shape[1], key_ref.shape[2]
    cnt_gt = _count(key_ref, nt, lambda kk, kt: kk > t)
    cnt_ge = _count(key_ref, nt, lambda kk, kt: kk >= t)
    need = k - cnt_gt

    @pl.when(jnp.max(cnt_ge) > k)
    def _():
        lane = lax.broadcasted_iota(I32, (rows, width), 1)

        def bit_body(b, c0):
            cand = c0 | jnp.left_shift(jnp.int32(1), idx_bits - 1 - b)
            f = _count(key_ref, nt, lambda kk, kt: jnp.where(kk == t, kt * width + lane, cand) < cand)
            return jnp.where(f < need, cand, c0)

        c0 = lax.fori_loop(0, idx_bits, bit_body, jnp.zeros((rows, 1), I32))

        def fix(kt, carry):
            kk = key_ref[kt]
            idx = kt * width + lane
            key_ref[kt] = jnp.where(jnp.where(kk == t, idx, c0) > c0, t - 1, kk)
            return carry

        lax.fori_loop(0, nt, fix, 0)


def _topk_select(key_ref, thr_ref, nt, k, idx_bits, lo0, hi0):
    rows = key_ref.shape[1]
    n_fin = _count(key_ref, nt, lambda kk, kt: kk > KEY_NEG_INF)
    done0 = jnp.where(n_fin <= k, 1.0, 0.0)
    thr0 = jnp.full((rows, 1), KEY_NEG_INF, I32)

    def cond(st):
        it, _, _, _, done, stuck = st
        return jnp.logical_and(it < BISECT_MAX_ITER, jnp.min(jnp.maximum(done, stuck)) < 0.5)

    def body(st):
        it, lo, hi, thr, done, stuck = st
        mid = 0.5 * lo + 0.5 * hi
        midk = _f32_key(mid)
        cnt = _count(key_ref, nt, lambda kk, kt: kk > midk)
        hit = jnp.where(cnt == k, 1.0 - done, 0.0)
        thr = jnp.where(hit > 0.5, midk, thr)
        done = jnp.maximum(done, hit)
        stuck = jnp.maximum(stuck, jnp.where(mid <= lo, 1.0, jnp.where(mid >= hi, 1.0, 0.0)))
        above = cnt >= k
        return it + 1, jnp.where(above, mid, lo), jnp.where(above, hi, mid), thr, done, stuck

    init = (jnp.int32(0), lo0, hi0, thr0, done0, jnp.zeros((rows, 1), F32))
    _, _, _, thr, done, _ = lax.while_loop(cond, body, init)
    thr_ref[...] = jnp.broadcast_to(thr, thr_ref.shape)

    @pl.when(jnp.min(done) < 0.5)
    def _():
        t = _kth_key(key_ref, nt, k)
        _break_ties(key_ref, nt, t, k, idx_bits)
        thr_ref[...] = jnp.broadcast_to(t - 1, thr_ref.shape)


def _ada_body(c_ref, w_ref, b_ref, o_ref):
    c = c_ref[...]
    s = c * jax.nn.sigmoid(c)
    s_hi, s_mid, s_lo = _split3(s)
    w_hi, w_mid, w_lo = _split3(w_ref[...])
    acc = _dot(s_hi, w_hi) + (_dot(s_hi, w_mid) + _dot(s_mid, w_hi))
    acc = acc + (_dot(s_hi, w_lo) + _dot(s_mid, w_mid) + _dot(s_lo, w_hi))
    o_ref[...] = acc + b_ref[...]


def _adaln(c, w_ada, b_ada):
    nb, d = c.shape
    n = w_ada.shape[1]
    tn = 1024
    cp = jnp.zeros((8, d), F32).at[:nb].set(c)
    out = pl.pallas_call(
        _ada_body,
        grid=(n // tn,),
        in_specs=[pl.BlockSpec((8, d), lambda j: (0, 0)),
                  pl.BlockSpec((d, tn), lambda j: (0, j)),
                  pl.BlockSpec((1, tn), lambda j: (0, j))],
        out_specs=pl.BlockSpec((8, tn), lambda j: (0, j)),
        out_shape=jax.ShapeDtypeStruct((8, n), F32),
        compiler_params=_cparams(("parallel",)),
    )(cp, w_ada, b_ada[None])
    return out[:nb]


def _inproj_body(x_ref, g_ref, sc_ref, sh_ref, w_ref, z_ref, h_ref):
    @pl.when(pl.program_id(2) == 0)
    def _():
        x = x_ref[...]
        y = x * lax.rsqrt(jnp.mean(x * x, axis=-1, keepdims=True) + NORM_EPS) * g_ref[...]
        h_ref[...] = (y * (1.0 + sc_ref[...]) + sh_ref[...]).astype(BF16)

    z_ref[...] = _dot(h_ref[...], w_ref[...])


def _inproj(x, g, sc, sh, w):
    nb, t, d = x.shape
    tm, tn = 512, 1536
    return pl.pallas_call(
        _inproj_body,
        grid=(nb, t // tm, Z_COLS // tn),
        in_specs=[pl.BlockSpec((None, tm, d), lambda b, i, j: (b, i, 0)),
                  pl.BlockSpec((1, d), lambda b, i, j: (0, 0)),
                  pl.BlockSpec((None, 1, d), lambda b, i, j: (b, 0, 0)),
                  pl.BlockSpec((None, 1, d), lambda b, i, j: (b, 0, 0)),
                  pl.BlockSpec((d, tn), lambda b, i, j: (0, j))],
        out_specs=pl.BlockSpec((None, tm, tn), lambda b, i, j: (b, i, j)),
        out_shape=jax.ShapeDtypeStruct((nb, t, Z_COLS), F32),
        scratch_shapes=[pltpu.VMEM((tm, d), BF16)],
        compiler_params=_cparams(("parallel", "parallel", "arbitrary")),
    )(x, g[None], sc, sh, w)


def _rms(x, g):
    return x * lax.rsqrt(jnp.mean(x * x, axis=-1, keepdims=True) + NORM_EPS) * g


def _dsa_prep_body(cq_ref, ckv_ref, sm_ref, tabh_ref, tabi_ref, gq_ref, gkv_ref,
                   wuq_ref, wuqr_ref, wuk_ref, widx_ref, widxr_ref, lng_ref, lnb_ref,
                   qa_ref, ka_ref, qi_ref, ki_ref, wi_ref, sg_ref):
    tm = cq_ref.shape[0]
    lane = lax.broadcasted_iota(I32, (tm, LANES), 1)
    cqn = _rms(cq_ref[...], gq_ref[...]).astype(BF16)
    ckvn = _rms(ckv_ref[...], gkv_ref[...])
    s0 = sm_ref[:, 0:LANES]
    s1 = sm_ref[:, LANES:2 * LANES]
    ch, sh = tabh_ref[:, 0:LANES], tabh_ref[:, LANES:2 * LANES]
    ci, si = tabi_ref[:, 0:LANES], tabi_ref[:, LANES:2 * LANES]

    k_rope = s0 * ch + pltpu.roll(s0, 32, 1) * sh
    ka_ref[:, 0:A_KV_RANK] = ckvn.astype(BF16)
    ka_ref[:, A_KV_RANK:QA_DIM] = jnp.where(lane < ROPE_DIM, k_rope, 0.0).astype(BF16)

    inside = jnp.where(lane >= 32, jnp.where(lane < 96, 1.0, 0.0), 0.0)
    mu = jnp.sum(s0 * inside, axis=-1, keepdims=True) * (1.0 / IDX_DIM)
    xc = (s0 - mu) * inside
    var = jnp.sum(xc * xc, axis=-1, keepdims=True) * (1.0 / IDX_DIM)
    y = xc * lax.rsqrt(var + NORM_EPS) * lng_ref[...] + lnb_ref[...]
    rot = jnp.where(lane < 40, pltpu.roll(y, LANES - 8, 1), pltpu.roll(y, 8, 1))
    ki_ref[...] = (y * ci + rot * si).astype(BF16)

    scale = HEAD_DIM ** -0.5
    w_scale = IDX_HEADS ** -0.5 * IDX_DIM ** -0.5
    for h in range(A_HEADS):
        qh = _dot(cqn, wuq_ref[h])
        qr = _dot(cqn, wuqr_ref[h])
        roped = qh * ch + qr * sh
        qabs = _dot(roped.astype(BF16), wuk_ref[h])
        qa_ref[h, :, 0:A_KV_RANK] = (qabs * scale).astype(BF16)
        qa_ref[h, :, A_KV_RANK:QA_DIM] = (jnp.where(lane < ROPE_DIM, roped, 0.0) * scale).astype(BF16)
        qih = _dot(cqn, widx_ref[h])
        qir = _dot(cqn, widxr_ref[h])
        qi_ref[h] = (qih * ci + qir * si).astype(BF16)
        wi_ref[h] = jnp.broadcast_to(s1[:, h:h + 1], (tm, LANES)) * w_scale
    sg_ref[...] = jax.nn.sigmoid(s1)


def _dsa_prep(z, tabh, tabi, gq, gkv, wuq, wuqr, wuk, widx, widxr, lng, lnb):
    nb, t, _ = z.shape
    tm = 256
    full = lambda a: pl.BlockSpec(a.shape, lambda b, i: (0,) * a.ndim)
    return pl.pallas_call(
        _dsa_prep_body,
        grid=(nb, t // tm),
        in_specs=[pl.BlockSpec((None, tm, A_Q_RANK), lambda b, i: (b, i, Z_CQ // A_Q_RANK)),
                  pl.BlockSpec((None, tm, A_KV_RANK), lambda b, i: (b, i, Z_CKV // A_KV_RANK)),
                  pl.BlockSpec((None, tm, 256), lambda b, i: (b, i, Z_SMALL // 256)),
                  pl.BlockSpec((None, tm, 256), lambda b, i: (b, i, 0)),
                  pl.BlockSpec((None, tm, 256), lambda b, i: (b, i, 0)),
                  full(gq), full(gkv), full(wuq), full(wuqr), full(wuk), full(widx), full(widxr),
                  full(lng), full(lnb)],
        out_specs=[pl.BlockSpec((None, A_HEADS, tm, QA_DIM), lambda b, i: (b, 0, i, 0)),
                   pl.BlockSpec((None, tm, QA_DIM), lambda b, i: (b, i, 0)),
                   pl.BlockSpec((None, IDX_HEADS, tm, LANES), lambda b, i: (b, 0, i, 0)),
                   pl.BlockSpec((None, tm, LANES), lambda b, i: (b, i, 0)),
                   pl.BlockSpec((None, IDX_HEADS, tm, LANES), lambda b, i: (b, 0, i, 0)),
                   pl.BlockSpec((None, tm, LANES), lambda b, i: (b, i, 0))],
        out_shape=[jax.ShapeDtypeStruct((nb, A_HEADS, t, QA_DIM), BF16),
                   jax.ShapeDtypeStruct((nb, t, QA_DIM), BF16),
                   jax.ShapeDtypeStruct((nb, IDX_HEADS, t, LANES), BF16),
                   jax.ShapeDtypeStruct((nb, t, LANES), BF16),
                   jax.ShapeDtypeStruct((nb, IDX_HEADS, t, LANES), F32),
                   jax.ShapeDtypeStruct((nb, t, LANES), F32)],
        compiler_params=_cparams(("parallel", "parallel")),
    )(z, z, z, tabh, tabi, gq, gkv, wuq, wuqr, wuk, widx, widxr, lng, lnb)


def _fold_lanes(x, op):
    out = x[:, 0:LANES]
    for c in range(1, x.shape[1] // LANES):
        out = op(out, x[:, c * LANES:(c + 1) * LANES])
    return out


def _dsa_attn_body(qi_ref, wi_ref, ki_ref, qa_ref, ka_ref, wuv_ref, o_ref,
                   key_ref, thr_ref, acc_ref, m_ref, l_ref, *, topk, idx_bits):
    qb = DSA_QB
    rows = A_HEADS * qb
    q0 = pl.program_id(1) * qb
    nt = (q0 + qb + KEY_TILE - 1) // KEY_TILE
    tq = q0 + lax.broadcasted_iota(I32, (qb, KEY_TILE), 0)
    lane = lax.broadcasted_iota(I32, (qb, KEY_TILE), 1)

    def score_body(kt, carry):
        smin, smax = carry
        k0 = pl.multiple_of(kt * KEY_TILE, KEY_TILE)
        ki = ki_ref[pl.ds(k0, KEY_TILE), :]
        s = None
        for h in range(IDX_HEADS):
            r = jnp.maximum(_dot_nt(qi_ref[h], ki), 0.0) * _lane_tile(wi_ref[h], KEY_TILE // LANES)
            s = r if s is None else s + r
        s = s + 0.0
        causal = k0 + lane <= tq
        key_ref[kt] = _f32_key(jnp.where(causal, s, -jnp.inf))
        smin = jnp.minimum(smin, _fold_lanes(jnp.where(causal, s, jnp.inf), jnp.minimum))
        smax = jnp.maximum(smax, _fold_lanes(jnp.where(causal, s, -jnp.inf), jnp.maximum))
        return smin, smax

    smin, smax = lax.fori_loop(0, nt, score_body,
                               (jnp.full((qb, LANES), jnp.inf, F32), jnp.full((qb, LANES), -jnp.inf, F32)))
    _topk_select(key_ref, thr_ref, nt, topk, idx_bits,
                 jnp.min(smin, axis=1, keepdims=True), jnp.max(smax, axis=1, keepdims=True))
    thr = thr_ref[:, 0:1]

    m_ref[...] = jnp.full((rows, LANES), NEG_BIG, F32)
    l_ref[...] = jnp.zeros((rows, LANES), F32)
    acc_ref[...] = jnp.zeros((rows, A_KV_RANK), F32)

    def attn_body(kt, carry):
        k0 = pl.multiple_of(kt * KEY_TILE, KEY_TILE)
        ka = ka_ref[pl.ds(k0, KEY_TILE), :]
        val = ka[:, 0:A_KV_RANK]
        sel = jnp.where(key_ref[kt] > thr, k0 + lane, tq + 1) <= tq
        bias = jnp.where(sel, 0.0, NEG_BIG)
        s = _dot_nt(qa_ref[...].reshape(rows, QA_DIM), ka) + _row_tile(bias, A_HEADS)
        m_old = m_ref[...]
        m_new = jnp.maximum(m_old, jnp.max(s, axis=1, keepdims=True))
        p = jnp.exp(s - _lane_tile(m_new, KEY_TILE // LANES))
        alpha = jnp.exp(m_old - m_new)
        l_ref[...] = l_ref[...] * alpha + jnp.sum(p, axis=1, keepdims=True)
        pb = p.astype(BF16)
        half = rows // ATTN_CHUNKS
        for c in range(ATTN_CHUNKS):
            r = slice(c * half, (c + 1) * half)
            acc_ref[r] = acc_ref[r] * _lane_tile(alpha[r], A_KV_RANK // LANES) + _dot(pb[r], val)
        m_ref[...] = m_new
        return carry

    lax.fori_loop(0, nt, attn_body, 0)

    inv = 1.0 / jnp.maximum(l_ref[...], 1e-30)
    o_lat = (acc_ref[...] * _lane_tile(inv, A_KV_RANK // LANES)).astype(BF16)
    for h in range(A_HEADS):
        o_ref[:, h * HEAD_DIM:(h + 1) * HEAD_DIM] = _dot(o_lat[h * qb:(h + 1) * qb], wuv_ref[h]).astype(BF16)


def _dsa_attn(qi, wi, ki, qa, ka, wuv):
    nb, _, t, _ = qa.shape
    topk = min(IDX_TOPK_MAX, t // 4)
    idx_bits = int(np.ceil(np.log2(t)))
    qb = DSA_QB
    body = functools.partial(_dsa_attn_body, topk=topk, idx_bits=idx_bits)
    return pl.pallas_call(
        body,
        grid=(nb, t // qb),
        in_specs=[pl.BlockSpec((None, IDX_HEADS, qb, LANES), lambda b, i: (b, 0, i, 0)),
                  pl.BlockSpec((None, IDX_HEADS, qb, LANES), lambda b, i: (b, 0, i, 0)),
                  pl.BlockSpec((None, t, LANES), lambda b, i: (b, 0, 0)),
                  pl.BlockSpec((None, A_HEADS, qb, QA_DIM), lambda b, i: (b, 0, i, 0)),
                  pl.BlockSpec((None, t, QA_DIM), lambda b, i: (b, 0, 0)),
                  pl.BlockSpec(wuv.shape, lambda b, i: (0, 0, 0))],
        out_specs=pl.BlockSpec((None, qb, A_HEADS * HEAD_DIM), lambda b, i: (b, i, 0)),
        out_shape=jax.ShapeDtypeStruct((nb, t, A_HEADS * HEAD_DIM), BF16),
        scratch_shapes=[pltpu.VMEM((t // KEY_TILE, qb, KEY_TILE), I32),
                        pltpu.VMEM((qb, LANES), I32),
                        pltpu.VMEM((A_HEADS * qb, A_KV_RANK), F32),
                        pltpu.VMEM((A_HEADS * qb, LANES), F32),
                        pltpu.VMEM((A_HEADS * qb, LANES), F32)],
        compiler_params=_cparams(("parallel", "parallel")),
    )(qi, wi, ki, qa, ka, wuv)


def _nsa_prep_body(bq_ref, bkv_ref, tabh_ref, qb_ref, kc_ref, vc_ref, ks_ref, vs_ref, kw_ref, vw_ref):
    tm = bq_ref.shape[0]
    lane = lax.broadcasted_iota(I32, (tm, LANES), 1)
    ch, sh = tabh_ref[:, 0:LANES], tabh_ref[:, LANES:2 * LANES]

    def rope(x):
        rot = jnp.where(lane < ROPE_DIM // 2, pltpu.roll(x, LANES - ROPE_DIM // 2, 1),
                        pltpu.roll(x, ROPE_DIM // 2, 1))
        return x * ch + rot * sh

    scale = HEAD_DIM ** -0.5
    for h in range(B_HEADS):
        sl = slice(h * HEAD_DIM, (h + 1) * HEAD_DIM)
        qb_ref[:, sl] = (rope(bq_ref[:, sl]) * scale).astype(BF16)
    outs = (kc_ref, vc_ref, ks_ref, vs_ref, kw_ref, vw_ref)
    for kind in range(6):
        for g in range(B_GROUPS):
            c0 = (kind * B_GROUPS + g) * HEAD_DIM
            v = bkv_ref[:, c0:c0 + HEAD_DIM]
            if kind % 2 == 0:
                v = rope(v)
            outs[kind][g] = v.astype(BF16)


def _nsa_prep(z, tabh):
    nb, t, _ = z.shape
    tm = 256
    kv_spec = pl.BlockSpec((None, B_GROUPS, tm, HEAD_DIM), lambda b, i: (b, 0, i, 0))
    kv_shape = jax.ShapeDtypeStruct((nb, B_GROUPS, t, HEAD_DIM), BF16)
    return pl.pallas_call(
        _nsa_prep_body,
        grid=(nb, t // tm),
        in_specs=[pl.BlockSpec((None, tm, 1024), lambda b, i: (b, i, Z_BQ // 1024)),
                  pl.BlockSpec((None, tm, 1536), lambda b, i: (b, i, Z_BKV // 1536)),
                  pl.BlockSpec((None, tm, 256), lambda b, i: (b, i, 0))],
        out_specs=[pl.BlockSpec((None, tm, 1024), lambda b, i: (b, i, 0))] + [kv_spec] * 6,
        out_shape=[jax.ShapeDtypeStruct((nb, t, 1024), BF16)] + [kv_shape] * 6,
        compiler_params=_cparams(("parallel", "parallel")),
    )(z, z, tabh)


def _compress_body(ch_ref, pe_ref, w1_ref, w2_ref, o_ref):
    nc = ch_ref.shape[0]
    half = CMP_STRIDE * HEAD_DIM
    x = ch_ref[...]
    ha = _dot(x, w1_ref[0:half, :])
    hb = _dot(x, w1_ref[half:2 * half, :])
    bias = _dot(pe_ref[...], w1_ref[...])[0:1]
    pre = ha + pltpu.roll(hb, nc - 1, 0) + bias
    hid = jax.nn.gelu(pre, approximate=True)
    o_ref[...] = _dot(hid.astype(BF16), w2_ref[...]).astype(BF16)


def _compress(tok, pe, w1, w2):
    nb, ng, t, d = tok.shape
    nc = t // CMP_STRIDE
    chunks = tok.reshape(nb, ng, nc, CMP_STRIDE * d)
    pe8 = jnp.broadcast_to(pe.reshape(1, CMP_LEN * d), (8, CMP_LEN * d)).astype(BF16)
    return pl.pallas_call(
        _compress_body,
        grid=(nb, ng),
        in_specs=[pl.BlockSpec((None, None, nc, CMP_STRIDE * d), lambda b, g: (b, g, 0, 0)),
                  pl.BlockSpec(pe8.shape, lambda b, g: (0, 0)),
                  pl.BlockSpec(w1.shape, lambda b, g: (0, 0)),
                  pl.BlockSpec(w2.shape, lambda b, g: (0, 0))],
        out_specs=pl.BlockSpec((None, None, nc, d), lambda b, g: (b, g, 0, 0)),
        out_shape=jax.ShapeDtypeStruct((nb, ng, nc, d), BF16),
        compiler_params=_cparams(("parallel", "parallel")),
    )(chunks, pe8, w1.astype(BF16), w2.astype(BF16))


def _nsa_attn_body(q_ref, kc_ref, vc_ref, ks_ref, vs_ref, kw_ref, vw_ref, gate_ref, ov_ref, ex_ref,
                   o_ref, key_ref, thr_ref, acc_ref, m_ref, l_ref, *, n_cmp, n_sel):
    qb = NSA_QB
    rows = B_HPG * qb
    nc = kc_ref.shape[0]
    q0 = pl.program_id(2) * qb
    qs = jnp.concatenate([q_ref[:, j * HEAD_DIM:(j + 1) * HEAD_DIM] for j in range(B_HPG)], axis=0)

    tq_c = q0 + lax.broadcasted_iota(I32, (qb, nc), 0)
    n_id = lax.broadcasted_iota(I32, (qb, nc), 1)
    c_ok = jnp.where(n_id < n_cmp, n_id * CMP_STRIDE + (CMP_LEN - 1), tq_c + 1) <= tq_c
    c_bias = _row_tile(jnp.where(c_ok, 0.0, NEG_BIG), B_HPG)
    c_keep = _row_tile(jnp.where(c_ok, 1.0, 0.0), B_HPG)
    sc = _dot_nt(qs, kc_ref[...]) + c_bias
    e = jnp.exp(sc - jnp.max(sc, axis=1, keepdims=True)) * c_keep
    pc = e / jnp.maximum(jnp.sum(e, axis=1, keepdims=True), 1e-30)
    oc = _dot(pc.astype(BF16), vc_ref[...])

    pc4 = pc[0:qb]
    for j in range(1, B_HPG):
        pc4 = pc4 + pc[j * qb:(j + 1) * qb]
    p_hi, p_mid, p_lo = _split3(pc4)
    ov = ov_ref[...]
    imp = _dot(p_hi, ov) + _dot(p_mid, ov) + _dot(p_lo, ov)
    tq_b = q0 + lax.broadcasted_iota(I32, (qb, LANES), 0)
    blk = lax.broadcasted_iota(I32, (qb, LANES), 1)
    cur = tq_b // SLC_LEN
    d_cur = cur - blk
    forced = jnp.where(blk == 0, 1.0, jnp.where(d_cur == 0, 1.0, jnp.where(d_cur == 1, 1.0, 0.0)))
    admissible = blk * SLC_LEN <= tq_b
    free = jnp.where(admissible, jnp.where(forced > 0.5, -1.0, imp), -1.0)
    imp = jnp.where(forced > 0.5, jnp.inf, imp + 0.0)
    key_ref[0] = _f32_key(jnp.where(admissible, imp, -jnp.inf))
    _topk_select(key_ref, thr_ref, 1, n_sel, 7,
                 jnp.full((qb, 1), -1.0, F32), jnp.max(free, axis=1, keepdims=True))
    selm = jnp.where(key_ref[0] > thr_ref[:, 0:1], 1.0, 0.0).astype(BF16)

    nt = (q0 + qb + KEY_TILE - 1) // KEY_TILE
    tq_k = q0 + lax.broadcasted_iota(I32, (qb, KEY_TILE), 0)
    lane_k = lax.broadcasted_iota(I32, (qb, KEY_TILE), 1)
    m_ref[...] = jnp.full((rows, LANES), NEG_BIG, F32)
    l_ref[...] = jnp.zeros((rows, LANES), F32)
    acc_ref[...] = jnp.zeros((rows, HEAD_DIM), F32)

    def sel_body(kt, carry):
        k0 = pl.multiple_of(kt * KEY_TILE, KEY_TILE)
        chosen = _dot(selm, ex_ref[kt])
        ok = jnp.where(chosen > 0.5, k0 + lane_k, tq_k + 1) <= tq_k
        bias = jnp.where(ok, 0.0, NEG_BIG)
        ks = ks_ref[pl.ds(k0, KEY_TILE), :]
        vs = vs_ref[pl.ds(k0, KEY_TILE), :]
        s = _dot_nt(qs, ks) + _row_tile(bias, B_HPG)
        m_old = m_ref[...]
        m_new = jnp.maximum(m_old, jnp.max(s, axis=1, keepdims=True))
        p = jnp.exp(s - _lane_tile(m_new, KEY_TILE // LANES))
        alpha = jnp.exp(m_old - m_new)
        l_ref[...] = l_ref[...] * alpha + jnp.sum(p, axis=1, keepdims=True)
        acc_ref[...] = acc_ref[...] * alpha + _dot(p.astype(BF16), vs)
        m_ref[...] = m_new
        return carry

    lax.fori_loop(0, nt, sel_body, 0)
    os_ = acc_ref[...] / jnp.maximum(l_ref[...], 1e-30)

    span = WINDOW + qb
    start = pl.multiple_of(jnp.maximum(q0 - WINDOW, 0), qb)
    tq_w = q0 + lax.broadcasted_iota(I32, (qb, span), 0)
    wpos = start + lax.broadcasted_iota(I32, (qb, span), 1)
    w_ok = jnp.where(wpos > tq_w - WINDOW, wpos, tq_w + 1) <= tq_w
    w_bias = _row_tile(jnp.where(w_ok, 0.0, NEG_BIG), B_HPG)
    sw = _dot_nt(qs, kw_ref[pl.ds(start, span), :]) + w_bias
    ew = jnp.exp(sw - jnp.max(sw, axis=1, keepdims=True))
    pw = ew / jnp.maximum(jnp.sum(ew, axis=1, keepdims=True), 1e-30)
    ow = _dot(pw.astype(BF16), vw_ref[pl.ds(start, span), :])

    gate = gate_ref[...]
    for j in range(B_HPG):
        r = slice(j * qb, (j + 1) * qb)
        res = (gate[:, 3 * j:3 * j + 1] * oc[r] + gate[:, 3 * j + 1:3 * j + 2] * os_[r]
               + gate[:, 3 * j + 2:3 * j + 3] * ow[r])
        o_ref[:, j * HEAD_DIM:(j + 1) * HEAD_DIM] = res.astype(BF16)


def _nsa_attn(qb_all, kc, vc, ks, vs, kw, vw, gates, ov, ex):
    nb, t, _ = qb_all.shape
    nc = kc.shape[2]
    n_cmp = (t - CMP_LEN) // CMP_STRIDE + 1
    n_sel = min(SLC_TOPN, t // SLC_LEN)
    qb = NSA_QB
    body = functools.partial(_nsa_attn_body, n_cmp=n_cmp, n_sel=n_sel)
    res_c = pl.BlockSpec((None, None, nc, HEAD_DIM), lambda b, g, i: (b, g, 0, 0))
    res_t = pl.BlockSpec((None, None, t, HEAD_DIM), lambda b, g, i: (b, g, 0, 0))
    return pl.pallas_call(
        body,
        grid=(nb, B_GROUPS, t // qb),
        in_specs=[pl.BlockSpec((None, qb, B_HPG * HEAD_DIM), lambda b, g, i: (b, i, g)),
                  res_c, res_c, res_t, res_t, res_t, res_t,
                  pl.BlockSpec((None, None, qb, 3 * B_HPG), lambda b, g, i: (b, g, i, 0)),
                  pl.BlockSpec(ov.shape, lambda b, g, i: (0, 0)),
                  pl.BlockSpec(ex.shape, lambda b, g, i: (0, 0, 0))],
        out_specs=pl.BlockSpec((None, qb, B_HPG * HEAD_DIM), lambda b, g, i: (b, i, g)),
        out_shape=jax.ShapeDtypeStruct((nb, t, B_HEADS * HEAD_DIM), BF16),
        scratch_shapes=[pltpu.VMEM((1, qb, LANES), I32),
                        pltpu.VMEM((qb, LANES), I32),
                        pltpu.VMEM((B_HPG * qb, HEAD_DIM), F32),
                        pltpu.VMEM((B_HPG * qb, LANES), F32),
                        pltpu.VMEM((B_HPG * qb, LANES), F32)],
        compiler_params=_cparams(("parallel", "parallel", "parallel")),
    )(qb_all, kc, vc, ks, vs, kw, vw, gates, ov, ex)


def _merge_body(oa_ref, ob_ref, wa_ref, wb_ref, ga_ref, gb_ref, o_ref):
    a = _dot(oa_ref[...], wa_ref[...])
    b = _dot(ob_ref[...], wb_ref[...])
    o_ref[...] = (jax.nn.sigmoid(ga_ref[...]) * a + jax.nn.sigmoid(gb_ref[...]) * b).astype(BF16)


def _merge(o_a, o_b, w_a, w_b, z):
    nb, t, da = o_a.shape
    d = w_a.shape[1]
    tm, tn = 512, 1024
    return pl.pallas_call(
        _merge_body,
        grid=(nb, t // tm, d // tn),
        in_specs=[pl.BlockSpec((None, tm, da), lambda b, i, j: (b, i, 0)),
                  pl.BlockSpec((None, tm, da), lambda b, i, j: (b, i, 0)),
                  pl.BlockSpec((da, tn), lambda b, i, j: (0, j)),
                  pl.BlockSpec((da, tn), lambda b, i, j: (0, j)),
                  pl.BlockSpec((None, tm, tn), lambda b, i, j: (b, i, j)),
                  pl.BlockSpec((None, tm, tn), lambda b, i, j: (b, i, d // tn + j))],
        out_specs=pl.BlockSpec((None, tm, tn), lambda b, i, j: (b, i, j)),
        out_shape=jax.ShapeDtypeStruct((nb, t, d), BF16),
        compiler_params=_cparams(("parallel", "parallel", "parallel")),
    )(o_a, o_b, w_a, w_b, z, z)


def _outproj_body(mg_ref, wo_ref, x_ref, gpost_ref, gt_ref, gpre_ref, sc_ref, sh_ref,
                  wr_ref, br_ref, x1_ref, h_ref, idx_ref, gate_ref):
    tm = x_ref.shape[0]
    y = _dot(mg_ref[...], wo_ref[...])
    x1 = x_ref[...] + gt_ref[...] * _rms(y, gpost_ref[...])
    x1_ref[...] = x1
    h = _rms(x1, gpre_ref[...]) * (1.0 + sc_ref[...]) + sh_ref[...]
    h_ref[...] = h
    h_hi = h.astype(BF16)
    h_lo = (h - h_hi.astype(F32)).astype(BF16)
    logits = (_dot(h_hi, wr_ref[0]) + (_dot(h_hi, wr_ref[1]) + _dot(h_lo, wr_ref[0]))) + br_ref[...]
    lane = lax.broadcasted_iota(I32, (tm, N_EXPERTS), 1).astype(F32)
    vals, idxs = [], []
    cur = logits
    for _ in range(TOP_K):
        m = jnp.max(cur, axis=1, keepdims=True)
        am = jnp.min(jnp.where(cur == m, lane, float(N_EXPERTS)), axis=1, keepdims=True)
        vals.append(m)
        idxs.append(am)
        cur = jnp.where(lane == am, -jnp.inf, cur)
    es = [jnp.exp(v - vals[0]) for v in vals]
    tot = es[0] + es[1] + es[2] + es[3]
    for k in range(TOP_K):
        idx_ref[:, k:k + 1] = idxs[k].astype(I32)
        gate_ref[:, k:k + 1] = es[k] / tot


def _outproj(merged, w_out, x, g_post, gt1, g_pre, sc2, sh2, wr, br):
    nb, t, d = x.shape
    tm = 256
    vec = lambda: pl.BlockSpec((1, d), lambda b, i: (0, 0))
    mod = lambda: pl.BlockSpec((None, 1, d), lambda b, i: (b, 0, 0))
    row = lambda w: pl.BlockSpec((None, tm, w), lambda b, i: (b, i, 0))
    return pl.pallas_call(
        _outproj_body,
        grid=(nb, t // tm),
        in_specs=[row(d), pl.BlockSpec((d, d), lambda b, i: (0, 0)), row(d), vec(), mod(), vec(), mod(), mod(),
                  pl.BlockSpec(wr.shape, lambda b, i: (0, 0, 0)),
                  pl.BlockSpec((1, N_EXPERTS), lambda b, i: (0, 0))],
        out_specs=[row(d), row(d), row(TOP_K), row(TOP_K)],
        out_shape=[jax.ShapeDtypeStruct((nb, t, d), F32),
                   jax.ShapeDtypeStruct((nb, t, d), F32),
                   jax.ShapeDtypeStruct((nb, t, TOP_K), I32),
                   jax.ShapeDtypeStruct((nb, t, TOP_K), F32)],
        compiler_params=_cparams(("parallel", "parallel")),
    )(merged, w_out, x, g_post[None], gt1, g_pre[None], sc2, sh2, wr, br[None])


def _gmm_body(be_ref, nu_ref, tokc_ref, tokn_ref, h_ref, wg_ref, wu_ref, bg_ref, bu_ref, wd_ref, bd_ref,
              y_ref, xf_ref, xb_ref, sem, *, ff_steps):
    i, j = pl.program_id(0), pl.program_id(1)
    n_used = nu_ref[0]
    slot = i % 2
    last = ff_steps - 1
    rows_per_step = ROW_BLK // ff_steps

    def row_copy(tok_ref, r, s):
        return pltpu.make_async_copy(h_ref.at[pl.ds(tok_ref[0, 0, r], 1), :],
                                     xf_ref.at[s, pl.ds(r, 1), :], sem.at[s])

    def wait_slot(s):
        pltpu.make_async_copy(h_ref.at[pl.ds(0, ROW_BLK), :], xf_ref.at[s], sem.at[s]).wait()

    @pl.when(j == 0)
    def _():
        @pl.when(i == 0)
        def _():
            def body(r, carry):
                row_copy(tokc_ref, r, 0).start()
                return carry

            lax.fori_loop(0, ROW_BLK, body, 0)

        @pl.when(i <= n_used)
        def _():
            wait_slot(slot)

        @pl.when(i < n_used)
        def _():
            xb_ref[...] = xf_ref[slot].astype(BF16)

    @pl.when(i < n_used)
    def _():
        x = xb_ref[...]
        g = jnp.minimum(_dot(x, wg_ref[...]) + bg_ref[...], SWIGLU_LIMIT)
        u = jnp.clip(_dot(x, wu_ref[...]) + bu_ref[...], -SWIGLU_LIMIT, SWIGLU_LIMIT)
        act = g * jax.nn.sigmoid(SWIGLU_ALPHA * g) * (u + 1.0)
        part = _dot(act.astype(BF16), wd_ref[...])
        for r in range(rows_per_step):
            row_copy(tokn_ref, j * rows_per_step + r, 1 - slot).start()

        @pl.when(j == 0)
        def _():
            y_ref[...] = part + bd_ref[...]

        @pl.when(j > 0)
        def _():
            y_ref[...] = y_ref[...] + part

    @pl.when(jnp.logical_and(i >= n_used, j == last))
    def _():
        y_ref[...] = jnp.zeros(y_ref.shape, F32)

    @pl.when(jnp.logical_and(jnp.logical_and(i == pl.num_programs(0) - 1, j == last), i < n_used))
    def _():
        wait_slot(1 - slot)


def _gmm(blk_e, n_used, row_tok, h, wgu, bgu, wd, bd):
    d = h.shape[1]
    ff = wd.shape[1]
    p = row_tok.shape[0]
    nblk = p // ROW_BLK
    nft = ff // FF_TILE
    tok3 = row_tok.reshape(nblk, 1, ROW_BLK)

    def expert(i, be, nu):
        return be[jnp.minimum(i, nu[0] - 1)]

    def tile(i, j, nu):
        return jnp.where(i < nu[0], j, nft - 1)

    grid_spec = pltpu.PrefetchScalarGridSpec(
        num_scalar_prefetch=2,
        grid=(nblk, nft),
        in_specs=[pl.BlockSpec((1, 1, ROW_BLK), lambda i, j, be, nu: (i, 0, 0), memory_space=pltpu.SMEM),
                  pl.BlockSpec((1, 1, ROW_BLK), lambda i, j, be, nu: (jnp.minimum(i + 1, nblk - 1), 0, 0),
                               memory_space=pltpu.SMEM),
                  pl.BlockSpec(memory_space=pl.ANY),
                  pl.BlockSpec((None, d, FF_TILE), lambda i, j, be, nu: (expert(i, be, nu), 0, tile(i, j, nu))),
                  pl.BlockSpec((None, d, FF_TILE),
                               lambda i, j, be, nu: (expert(i, be, nu), 0, nft + tile(i, j, nu))),
                  pl.BlockSpec((None, 1, FF_TILE), lambda i, j, be, nu: (expert(i, be, nu), 0, tile(i, j, nu))),
                  pl.BlockSpec((None, 1, FF_TILE),
                               lambda i, j, be, nu: (expert(i, be, nu), 0, nft + tile(i, j, nu))),
                  pl.BlockSpec((None, FF_TILE, d), lambda i, j, be, nu: (expert(i, be, nu), tile(i, j, nu), 0)),
                  pl.BlockSpec((None, 1, d), lambda i, j, be, nu: (expert(i, be, nu), 0, 0))],
        out_specs=pl.BlockSpec((ROW_BLK, d), lambda i, j, be, nu: (i, 0)),
        scratch_shapes=[pltpu.VMEM((2, ROW_BLK, d), F32),
                        pltpu.VMEM((ROW_BLK, d), BF16),
                        pltpu.SemaphoreType.DMA((2,))],
    )
    assert ROW_BLK % nft == 0
    return pl.pallas_call(
        functools.partial(_gmm_body, ff_steps=nft),
        grid_spec=grid_spec,
        out_shape=jax.ShapeDtypeStruct((p, d), F32),
        compiler_params=_cparams(("arbitrary", "arbitrary")),
    )(blk_e, n_used, tok3, tok3, h, wgu, wgu, bgu, bgu, wd, bd)


def _combine_body(destc_ref, destn_ref, ys_ref, gate_ref, x1_ref, g_ref, gt_ref, o_ref, buf, sem):
    tm = CMB_TOK
    i = pl.program_id(0)
    slot = i % 2

    def issue(dest_ref, s):
        def body(r, carry):
            pltpu.make_async_copy(ys_ref.at[pl.ds(dest_ref[0, 0, r], 1), :],
                                  buf.at[s, pl.ds(r, 1), :], sem.at[s]).start()
            return carry

        lax.fori_loop(0, TOP_K * tm, body, 0, unroll=16)

    @pl.when(i == 0)
    def _():
        issue(destc_ref, 0)

    pltpu.make_async_copy(ys_ref.at[pl.ds(0, TOP_K * tm), :], buf.at[slot], sem.at[slot]).wait()

    @pl.when(i + 1 < pl.num_programs(0))
    def _():
        issue(destn_ref, 1 - slot)

    gate = gate_ref[...]
    y = buf[slot, 0:tm] * gate[:, 0:1]
    for k in range(1, TOP_K):
        y = y + buf[slot, k * tm:(k + 1) * tm] * gate[:, k:k + 1]
    o_ref[...] = x1_ref[...] + gt_ref[...] * _rms(y, g_ref[...])


def _combine(dest, ys, gates, x1, g_post, gt2):
    nb, t, d = x1.shape
    n = nb * t
    tm = CMB_TOK
    steps = n // tm
    dest_t = dest.reshape(steps, tm, TOP_K).transpose(0, 2, 1).reshape(steps, 1, TOP_K * tm)
    out = pl.pallas_call(
        _combine_body,
        grid=(steps,),
        in_specs=[pl.BlockSpec((1, 1, TOP_K * tm), lambda i: (i, 0, 0), memory_space=pltpu.SMEM),
                  pl.BlockSpec((1, 1, TOP_K * tm), lambda i: (jnp.minimum(i + 1, steps - 1), 0, 0),
                               memory_space=pltpu.SMEM),
                  pl.BlockSpec(memory_space=pl.ANY),
                  pl.BlockSpec((tm, TOP_K), lambda i: (i, 0)),
                  pl.BlockSpec((tm, d), lambda i: (i, 0)),
                  pl.BlockSpec((1, d), lambda i: (0, 0)),
                  pl.BlockSpec((None, 1, d), lambda i: ((i * tm) // t, 0, 0))],
        out_specs=pl.BlockSpec((tm, d), lambda i: (i, 0)),
        out_shape=jax.ShapeDtypeStruct((n, d), F32),
        scratch_shapes=[pltpu.VMEM((2, TOP_K * tm, d), F32), pltpu.SemaphoreType.DMA((2,))],
        compiler_params=_cparams(("arbitrary",)),
    )(dest_t, dest_t, ys, gates, x1.reshape(n, d), g_post[None], gt2)
    return out.reshape(nb, t, d)


def _route(top_idx):
    n = top_idx.shape[0]
    nk = n * TOP_K
    e = top_idx.reshape(nk)
    onehot = (e[:, None] == jnp.arange(N_EXPERTS, dtype=I32)[None, :]).astype(I32)
    csum = jnp.cumsum(onehot, axis=0)
    rank = jnp.sum((csum - onehot) * onehot, axis=1)
    counts = csum[-1]
    padded = (counts + ROW_BLK - 1) // ROW_BLK * ROW_BLK
    pend = jnp.cumsum(padded)
    pstart = pend - padded
    dest = (pstart[e] + rank).astype(I32)
    nblk = nk // ROW_BLK + N_EXPERTS
    tok = jnp.arange(nk, dtype=I32) // TOP_K
    row_tok = jnp.zeros((nblk * ROW_BLK,), I32).at[dest].set(tok, unique_indices=True)
    blk_start = jnp.arange(nblk, dtype=I32) * ROW_BLK
    blk_e = jnp.minimum(jnp.sum((pend[None, :] <= blk_start[:, None]).astype(I32), axis=1), N_EXPERTS - 1)
    n_used = (pend[-1:] // ROW_BLK).astype(I32)
    return dest.reshape(n, TOP_K), row_tok, blk_e.astype(I32), n_used


def _rot_cols(w, half):
    z = jnp.zeros_like(w)
    z = z.at[..., 0:half].set(w[..., half:2 * half])
    return z.at[..., half:2 * half].set(w[..., 0:half])


def _prep_w_in(w_in):
    d = w_in.shape[0]
    offs = np.cumsum([0, A_Q_RANK, A_KV_RANK, ROPE_DIM, IDX_DIM, IDX_HEADS, B_HEADS * HEAD_DIM,
                      6 * B_GROUPS * HEAD_DIM, 3 * B_HEADS, 2 * d])
    cq, ckv, krope, kidx, widx, bq, bkv, bgate, mgate = [w_in[:, int(offs[i]):int(offs[i + 1])] for i in range(9)]
    krope_rot = jnp.concatenate([krope[:, ROPE_DIM // 2:], krope[:, :ROPE_DIM // 2]], axis=1)
    small = jnp.concatenate([krope, kidx, krope_rot, widx, bgate,
                             jnp.zeros((d, 256 - 160), w_in.dtype)], axis=1)
    w = jnp.concatenate([mgate, bq, cq, ckv, small, bkv], axis=1)
    assert w.shape[1] == Z_COLS
    return w.astype(BF16)


def _rope_tables(positions):
    def tab(rot_dim, lo, width):
        inv = jnp.power(ROPE_THETA, -jnp.arange(0, rot_dim, 2, dtype=F32) / rot_dim)
        ang = positions.astype(F32)[..., None] * inv
        c, s = jnp.cos(ang), jnp.sin(ang)
        shape = positions.shape
        cos_t = jnp.concatenate([jnp.zeros(shape + (lo,), F32), c, c,
                                 jnp.ones(shape + (width - rot_dim,), F32),
                                 jnp.zeros(shape + (LANES - lo - width,), F32)], axis=-1)
        sin_t = jnp.concatenate([jnp.zeros(shape + (lo,), F32), -s, s,
                                 jnp.zeros(shape + (LANES - lo - rot_dim,), F32)], axis=-1)
        return jnp.concatenate([cos_t, sin_t], axis=-1)

    return tab(ROPE_DIM, 0, LANES), tab(IDX_ROPE, 32, IDX_DIM)


def _select_tables(t):
    n_cmp = (t - CMP_LEN) // CMP_STRIDE + 1
    nc = t // CMP_STRIDE
    n = np.arange(nc)[:, None]
    m = np.arange(LANES)[None, :]
    ov = (n * CMP_STRIDE < m * SLC_LEN + SLC_LEN) & (n * CMP_STRIDE + CMP_LEN - 1 >= m * SLC_LEN) & (n < n_cmp)
    key_blk = (np.arange(t) // SLC_LEN).reshape(t // KEY_TILE, 1, KEY_TILE)
    ex = key_blk == np.arange(LANES).reshape(1, LANES, 1)
    return jnp.asarray(ov, BF16), jnp.asarray(ex, BF16)


def kernel(x, c, positions, w_ada, b_ada, g_pre_mix, g_post_mix, g_pre_ffn, g_post_ffn, w_in, a_cq_norm, a_ckv_norm, a_w_uq, a_w_uk, a_w_uv, idx_w_q, idx_k_norm_g, idx_k_norm_b, cmp_k_pe, cmp_k_w1, cmp_k_w2, cmp_v_pe, cmp_v_w1, cmp_v_w2, w_br_a, w_br_b, w_out, w_router, b_router, w_gate_up, b_gate_up, w_down, b_down):
    nb, t, d = x.shape
    n = nb * t
    assert t % KEY_TILE == 0 and t // SLC_LEN <= LANES and d % LANES == 0

    mod = _adaln(c, w_ada, b_ada)
    sh1, sc1, gt1, sh2, sc2, gt2 = [m[:, None, :] for m in jnp.split(mod, 6, axis=-1)]

    z = _inproj(x, g_pre_mix, sc1, sh1, _prep_w_in(w_in))
    tabh, tabi = _rope_tables(positions)

    wuq = jnp.transpose(a_w_uq, (1, 0, 2))
    wuk = jnp.transpose(a_w_uk, (1, 2, 0))
    wuk = jnp.concatenate([jnp.zeros((A_HEADS, ROPE_DIM, A_KV_RANK), wuk.dtype), wuk], axis=1)
    widx = jnp.transpose(idx_w_q, (1, 0, 2))
    pad_idx = lambda w: jnp.pad(w, ((0, 0), (0, 0), (32, LANES - 32 - IDX_DIM)))
    pad_vec = lambda v: jnp.pad(v, (32, LANES - 32 - IDX_DIM))[None]
    qa, ka, qi, ki, wi, sg = _dsa_prep(
        z, tabh, tabi, a_cq_norm[None], a_ckv_norm[None],
        wuq.astype(BF16), _rot_cols(wuq, ROPE_DIM // 2).astype(BF16), wuk.astype(BF16),
        pad_idx(widx).astype(BF16), pad_idx(_rot_cols(widx, IDX_ROPE // 2)).astype(BF16),
        pad_vec(idx_k_norm_g), pad_vec(idx_k_norm_b))
    o_a = _dsa_attn(qi, wi, ki, qa, ka, a_w_uv.astype(BF16))

    qb_all, kc_tok, vc_tok, ks, vs, kw, vw = _nsa_prep(z, tabh)
    kc = _compress(kc_tok, cmp_k_pe, cmp_k_w1, cmp_k_w2)
    vc = _compress(vc_tok, cmp_v_pe, cmp_v_w1, cmp_v_w2)
    gates_b = sg[:, :, IDX_HEADS:IDX_HEADS + 3 * B_HEADS].reshape(nb, t, B_GROUPS, 3 * B_HPG).transpose(0, 2, 1, 3)
    ov, ex = _select_tables(t)
    o_b = _nsa_attn(qb_all, kc, vc, ks, vs, kw, vw, gates_b, ov, ex)

    merged = _merge(o_a, o_b, w_br_a.astype(BF16), w_br_b.astype(BF16), z)

    wr_hi = w_router.astype(BF16)
    wr = jnp.stack([wr_hi, (w_router - wr_hi.astype(F32)).astype(BF16)])
    x1, h2, top_idx, gates = _outproj(merged, w_out.astype(BF16), x, g_post_mix, gt1, g_pre_ffn, sc2, sh2,
                                      wr, b_router)
    dest, row_tok, blk_e, n_used = _route(top_idx.reshape(n, TOP_K))
    ys = _gmm(blk_e, n_used, row_tok, h2.reshape(n, d), w_gate_up.astype(BF16), b_gate_up[:, None, :],
              w_down.astype(BF16), b_down[:, None, :])
    return _combine(dest, ys, gates.reshape(n, TOP_K), x1, g_post_ffn, gt2)
```

```python
import functools

import numpy as np
import jax
import jax.numpy as jnp
from jax import lax
from jax.experimental import pallas as pl
from jax.experimental.pallas import tpu as pltpu

F32 = jnp.float32
BF16 = jnp.bfloat16
I32 = jnp.int32

HEAD_DIM = 128
ROPE_DIM = 32
ROPE_THETA = 500000.0
NORM_EPS = 1e-6
Q_BLOCK = 128
A_HEADS = 8
A_Q_RANK = 512
A_KV_RANK = 256
IDX_HEADS = 8
IDX_DIM = 64
IDX_ROPE = 16
IDX_TOPK_MAX = 256
B_HEADS = 8
B_GROUPS = 2
B_HPG = 4
CMP_LEN = 32
CMP_STRIDE = 16
CMP_HIDDEN = 256
SLC_LEN = 64
SLC_TOPN = 16
WINDOW = 512
N_EXPERTS = 32
TOP_K = 4
SWIGLU_ALPHA = 1.702
SWIGLU_LIMIT = 7.0

LANES = 128
VMEM_LIMIT = 58 * 1024 * 1024

KEY_TILE = 512
QA_DIM = A_KV_RANK + LANES
ROW_BLK = 512
FF_TILE = 512
CMB_TOK = 128
ATTN_CHUNKS = 4
NSA_QB = 256
DSA_QB = 128

Z_MGATE, Z_BQ, Z_CQ, Z_CKV, Z_SMALL, Z_BKV, Z_COLS = 0, 4096, 5120, 5632, 5888, 6144, 7680

NEG_BIG = -1e30
INT_MIN = np.int32(-2147483648)
KEY_NEG_INF = np.int32(-2139095041)
BISECT_MAX_ITER = 24


def _cparams(sem):
    return pltpu.CompilerParams(dimension_semantics=sem, vmem_limit_bytes=VMEM_LIMIT)


def _dot(a, b):
    return jnp.dot(a, b, preferred_element_type=F32)


def _dot_nt(a, b):
    return lax.dot_general(a, b, (((1,), (1,)), ((), ())), preferred_element_type=F32)


def _split3(a):
    hi = a.astype(BF16)
    r1 = a - hi.astype(F32)
    mid = r1.astype(BF16)
    lo = (r1 - mid.astype(F32)).astype(BF16)
    return hi, mid, lo


def _lane_tile(a, n):
    return jnp.concatenate([a] * n, axis=1) if n > 1 else a


def _row_tile(a, n):
    return jnp.concatenate([a] * n, axis=0) if n > 1 else a


def _f32_key(s):
    bits = pltpu.bitcast(s, I32)
    return bits ^ (jnp.right_shift(bits, 31) & np.int32(0x7FFFFFFF))


def _count(key_ref, nt, pred):
    rows, width = key_ref.shape[1], key_ref.shape[2]

    def body(kt, acc):
        hit = pred(key_ref[kt], kt)
        for c in range(width // LANES):
            acc = jnp.where(hit[:, c * LANES:(c + 1) * LANES], acc + 1.0, acc)
        return acc

    acc = lax.fori_loop(0, nt, body, jnp.zeros((rows, LANES), F32))
    return jnp.sum(acc, axis=1, keepdims=True)


def _kth_key(key_ref, nt, k):
    rows = key_ref.shape[1]

    def bit_body(b, ans):
        cand = ans | jnp.left_shift(jnp.int32(1), 31 - b)
        cs = cand ^ INT_MIN
        cnt = _count(key_ref, nt, lambda kk, kt: kk >= cs)
        return jnp.where(cnt >= k, cand, ans)

    ans = lax.fori_loop(0, 32, bit_body, jnp.zeros((rows, 1), I32))
    return ans ^ INT_MIN


def _break_ties(key_ref, nt, t, k, idx_bits):
    rows, width = key_ref.shape[1], key_ref.shape[2]
    cnt_gt = _count(key_ref, nt, lambda kk, kt: kk > t)
    cnt_ge = _count(key_ref, nt, lambda kk, kt: kk >= t)
    need = k - cnt_gt

    @pl.when(jnp.max(cnt_ge) > k)
    def _():
        lane = lax.broadcasted_iota(I32, (rows, width), 1)

        def bit_body(b, c0):
            cand = c0 | jnp.left_shift(jnp.int32(1), idx_bits - 1 - b)
            f = _count(key_ref, nt, lambda kk, kt: jnp.where(kk == t, kt * width + lane, cand) < cand)
            return jnp.where(f < need, cand, c0)

        c0 = lax.fori_loop(0, idx_bits, bit_body, jnp.zeros((rows, 1), I32))

        def fix(kt, carry):
            kk = key_ref[kt]
            idx = kt * width + lane
            key_ref[kt] = jnp.where(jnp.where(kk == t, idx, c0) > c0, t - 1, kk)
            return carry

        lax.fori_loop(0, nt, fix, 0)


def _topk_select(key_ref, thr_ref, nt, k, idx_bits, lo0, hi0):
    rows = key_ref.shape[1]
    n_fin = _count(key_ref, nt, lambda kk, kt: kk > KEY_NEG_INF)
    done0 = jnp.where(n_fin <= k, 1.0, 0.0)
    thr0 = jnp.full((rows, 1), KEY_NEG_INF, I32)

    def cond(st):
        it, _, _, _, done, stuck = st
        return jnp.logical_and(it < BISECT_MAX_ITER, jnp.min(jnp.maximum(done, stuck)) < 0.5)

    def body(st):
        it, lo, hi, thr, done, stuck = st
        mid = 0.5 * lo + 0.5 * hi
        midk = _f32_key(mid)
        cnt = _count(key_ref, nt, lambda kk, kt: kk > midk)
        hit = jnp.where(cnt == k, 1.0 - done, 0.0)
        thr = jnp.where(hit > 0.5, midk, thr)
        done = jnp.maximum(done, hit)
        stuck = jnp.maximum(stuck, jnp.where(mid <= lo, 1.0, jnp.where(mid >= hi, 1.0, 0.0)))
        above = cnt >= k
        return it + 1, jnp.where(above, mid, lo), jnp.where(above, hi, mid), thr, done, stuck

    init = (jnp.int32(0), lo0, hi0, thr0, done0, jnp.zeros((rows, 1), F32))
    _, _, _, thr, done, _ = lax.while_loop(cond, body, init)
    thr_ref[...] = jnp.broadcast_to(thr, thr_ref.shape)

    @pl.when(jnp.min(done) < 0.5)
    def _():
        t = _kth_key(key_ref, nt, k)
        _break_ties(key_ref, nt, t, k, idx_bits)
        thr_ref[...] = jnp.broadcast_to(t - 1, thr_ref.shape)


def _ada_body(c_ref, w_ref, b_ref, o_ref):
    c = c_ref[...]
    s = c * jax.nn.sigmoid(c)
    s_hi, s_mid, s_lo = _split3(s)
    w_hi, w_mid, w_lo = _split3(w_ref[...])
    acc = _dot(s_hi, w_hi) + (_dot(s_hi, w_mid) + _dot(s_mid, w_hi))
    acc = acc + (_dot(s_hi, w_lo) + _dot(s_mid, w_mid) + _dot(s_lo, w_hi))
    o_ref[...] = acc + b_ref[...]


def _adaln(c, w_ada, b_ada):
    nb, d = c.shape
    n = w_ada.shape[1]
    tn = 1024
    cp = jnp.zeros((8, d), F32).at[:nb].set(c)
    out = pl.pallas_call(
        _ada_body,
        grid=(n // tn,),
        in_specs=[pl.BlockSpec((8, d), lambda j: (0, 0)),
                  pl.BlockSpec((d, tn), lambda j: (0, j)),
                  pl.BlockSpec((1, tn), lambda j: (0, j))],
        out_specs=pl.BlockSpec((8, tn), lambda j: (0, j)),
        out_shape=jax.ShapeDtypeStruct((8, n), F32),
        compiler_params=_cparams(("parallel",)),
    )(cp, w_ada, b_ada[None])
    return out[:nb]


def _inproj_body(x_ref, g_ref, sc_ref, sh_ref, w_ref, z_ref, h_ref):
    @pl.when(pl.program_id(2) == 0)
    def _():
        x = x_ref[...]
        y = x * lax.rsqrt(jnp.mean(x * x, axis=-1, keepdims=True) + NORM_EPS) * g_ref[...]
        h_ref[...] = (y * (1.0 + sc_ref[...]) + sh_ref[...]).astype(BF16)

    z_ref[...] = _dot(h_ref[...], w_ref[...])


def _inproj(x, g, sc, sh, w):
    nb, t, d = x.shape
    tm, tn = 512, 1536
    return pl.pallas_call(
        _inproj_body,
        grid=(nb, t // tm, Z_COLS // tn),
        in_specs=[pl.BlockSpec((None, tm, d), lambda b, i, j: (b, i, 0)),
                  pl.BlockSpec((1, d), lambda b, i, j: (0, 0)),
                  pl.BlockSpec((None, 1, d), lambda b, i, j: (b, 0, 0)),
                  pl.BlockSpec((None, 1, d), lambda b, i, j: (b, 0, 0)),
                  pl.BlockSpec((d, tn), lambda b, i, j: (0, j))],
        out_specs=pl.BlockSpec((None, tm, tn), lambda b, i, j: (b, i, j)),
        out_shape=jax.ShapeDtypeStruct((nb, t, Z_COLS), F32),
        scratch_shapes=[pltpu.VMEM((tm, d), BF16)],
        compiler_params=_cparams(("parallel", "parallel", "arbitrary")),
    )(x, g[None], sc, sh, w)


def _rms(x, g):
    return x * lax.rsqrt(jnp.mean(x * x, axis=-1, keepdims=True) + NORM_EPS) * g


def _dsa_prep_body(cq_ref, ckv_ref, sm_ref, tabh_ref, tabi_ref, gq_ref, gkv_ref,
                   wuq_ref, wuqr_ref, wuk_ref, widx_ref, widxr_ref, lng_ref, lnb_ref,
                   qa_ref, ka_ref, qi_ref, ki_ref, wi_ref, sg_ref):
    tm = cq_ref.shape[0]
    lane = lax.broadcasted_iota(I32, (tm, LANES), 1)
    cqn = _rms(cq_ref[...], gq_ref[...]).astype(BF16)
    ckvn = _rms(ckv_ref[...], gkv_ref[...])
    s0 = sm_ref[:, 0:LANES]
    s1 = sm_ref[:, LANES:2 * LANES]
    ch, sh = tabh_ref[:, 0:LANES], tabh_ref[:, LANES:2 * LANES]
    ci, si = tabi_ref[:, 0:LANES], tabi_ref[:, LANES:2 * LANES]

    k_rope = s0 * ch + pltpu.roll(s0, 32, 1) * sh
    ka_ref[:, 0:A_KV_RANK] = ckvn.astype(BF16)
    ka_ref[:, A_KV_RANK:QA_DIM] = jnp.where(lane < ROPE_DIM, k_rope, 0.0).astype(BF16)

    inside = jnp.where(lane >= 32, jnp.where(lane < 96, 1.0, 0.0), 0.0)
    mu = jnp.sum(s0 * inside, axis=-1, keepdims=True) * (1.0 / IDX_DIM)
    xc = (s0 - mu) * inside
    var = jnp.sum(xc * xc, axis=-1, keepdims=True) * (1.0 / IDX_DIM)
    y = xc * lax.rsqrt(var + NORM_EPS) * lng_ref[...] + lnb_ref[...]
    rot = jnp.where(lane < 40, pltpu.roll(y, LANES - 8, 1), pltpu.roll(y, 8, 1))
    ki_ref[...] = (y * ci + rot * si).astype(BF16)

    scale = HEAD_DIM ** -0.5
    w_scale = IDX_HEADS ** -0.5 * IDX_DIM ** -0.5
    for h in range(A_HEADS):
        qh = _dot(cqn, wuq_ref[h])
        qr = _dot(cqn, wuqr_ref[h])
        roped = qh * ch + qr * sh
        qabs = _dot(roped.astype(BF16), wuk_ref[h])
        qa_ref[h, :, 0:A_KV_RANK] = (qabs * scale).astype(BF16)
        qa_ref[h, :, A_KV_RANK:QA_DIM] = (jnp.where(lane < ROPE_DIM, roped, 0.0) * scale).astype(BF16)
        qih = _dot(cqn, widx_ref[h])
        qir = _dot(cqn, widxr_ref[h])
        qi_ref[h] = (qih * ci + qir * si).astype(BF16)
        wi_ref[h] = jnp.broadcast_to(s1[:, h:h + 1], (tm, LANES)) * w_scale
    sg_ref[...] = jax.nn.sigmoid(s1)


def _dsa_prep(z, tabh, tabi, gq, gkv, wuq, wuqr, wuk, widx, widxr, lng, lnb):
    nb, t, _ = z.shape
    tm = 256
    full = lambda a: pl.BlockSpec(a.shape, lambda b, i: (0,) * a.ndim)
    return pl.pallas_call(
        _dsa_prep_body,
        grid=(nb, t // tm),
        in_specs=[pl.BlockSpec((None, tm, A_Q_RANK), lambda b, i: (b, i, Z_CQ // A_Q_RANK)),
                  pl.BlockSpec((None, tm, A_KV_RANK), lambda b, i: (b, i, Z_CKV // A_KV_RANK)),
                  pl.BlockSpec((None, tm, 256), lambda b, i: (b, i, Z_SMALL // 256)),
                  pl.BlockSpec((None, tm, 256), lambda b, i: (b, i, 0)),
                  pl.BlockSpec((None, tm, 256), lambda b, i: (b, i, 0)),
                  full(gq), full(gkv), full(wuq), full(wuqr), full(wuk), full(widx), full(widxr),
                  full(lng), full(lnb)],
        out_specs=[pl.BlockSpec((None, A_HEADS, tm, QA_DIM), lambda b, i: (b, 0, i, 0)),
                   pl.BlockSpec((None, tm, QA_DIM), lambda b, i: (b, i, 0)),
                   pl.BlockSpec((None, IDX_HEADS, tm, LANES), lambda b, i: (b, 0, i, 0)),
                   pl.BlockSpec((None, tm, LANES), lambda b, i: (b, i, 0)),
                   pl.BlockSpec((None, IDX_HEADS, tm, LANES), lambda b, i: (b, 0, i, 0)),
                   pl.BlockSpec((None, tm, LANES), lambda b, i: (b, i, 0))],
        out_shape=[jax.ShapeDtypeStruct((nb, A_HEADS, t, QA_DIM), BF16),
                   jax.ShapeDtypeStruct((nb, t, QA_DIM), BF16),
                   jax.ShapeDtypeStruct((nb, IDX_HEADS, t, LANES), BF16),
                   jax.ShapeDtypeStruct((nb, t, LANES), BF16),
                   jax.ShapeDtypeStruct((nb, IDX_HEADS, t, LANES), F32),
                   jax.ShapeDtypeStruct((nb, t, LANES), F32)],
        compiler_params=_cparams(("parallel", "parallel")),
    )(z, z, z, tabh, tabi, gq, gkv, wuq, wuqr, wuk, widx, widxr, lng, lnb)


def _fold_lanes(x, op):
    out = x[:, 0:LANES]
    for c in range(1, x.shape[1] // LANES):
        out = op(out, x[:, c * LANES:(c + 1) * LANES])
    return out


def _dsa_attn_body(qi_ref, wi_ref, ki_ref, qa_ref, ka_ref, wuv_ref, o_ref,
                   key_ref, thr_ref, acc_ref, m_ref, l_ref, *, topk, idx_bits):
    qb = DSA_QB
    rows = A_HEADS * qb
    q0 = pl.program_id(1) * qb
    nt = (q0 + qb + KEY_TILE - 1) // KEY_TILE
    tq = q0 + lax.broadcasted_iota(I32, (qb, KEY_TILE), 0)
    lane = lax.broadcasted_iota(I32, (qb, KEY_TILE), 1)

    def score_body(kt, carry):
        smin, smax = carry
        k0 = pl.multiple_of(kt * KEY_TILE, KEY_TILE)
        ki = ki_ref[pl.ds(k0, KEY_TILE), :]
        s = None
        for h in range(IDX_HEADS):
            r = jnp.maximum(_dot_nt(qi_ref[h], ki), 0.0) * _lane_tile(wi_ref[h], KEY_TILE // LANES)
            s = r if s is None else s + r
        s = s + 0.0
        causal = k0 + lane <= tq
        key_ref[kt] = _f32_key(jnp.where(causal, s, -jnp.inf))
        smin = jnp.minimum(smin, _fold_lanes(jnp.where(causal, s, jnp.inf), jnp.minimum))
        smax = jnp.maximum(smax, _fold_lanes(jnp.where(causal, s, -jnp.inf), jnp.maximum))
        return smin, smax

    smin, smax = lax.fori_loop(0, nt, score_body,
                               (jnp.full((qb, LANES), jnp.inf, F32), jnp.full((qb, LANES), -jnp.inf, F32)))
    _topk_select(key_ref, thr_ref, nt, topk, idx_bits,
                 jnp.min(smin, axis=1, keepdims=True), jnp.max(smax, axis=1, keepdims=True))
    thr = thr_ref[:, 0:1]

    m_ref[...] = jnp.full((rows, LANES), NEG_BIG, F32)
    l_ref[...] = jnp.zeros((rows, LANES), F32)
    acc_ref[...] = jnp.zeros((rows, A_KV_RANK), F32)

    def attn_body(kt, carry):
        k0 = pl.multiple_of(kt * KEY_TILE, KEY_TILE)
        ka = ka_ref[pl.ds(k0, KEY_TILE), :]
        val = ka[:, 0:A_KV_RANK]
        sel = jnp.where(key_ref[kt] > thr, k0 + lane, tq + 1) <= tq
        bias = jnp.where(sel, 0.0, NEG_BIG)
        s = _dot_nt(qa_ref[...].reshape(rows, QA_DIM), ka) + _row_tile(bias, A_HEADS)
        m_old = m_ref[...]
        m_new = jnp.maximum(m_old, jnp.max(s, axis=1, keepdims=True))
        p = jnp.exp(s - _lane_tile(m_new, KEY_TILE // LANES))
        alpha = jnp.exp(m_old - m_new)
        l_ref[...] = l_ref[...] * alpha + jnp.sum(p, axis=1, keepdims=True)
        pb = p.astype(BF16)
        half = rows // ATTN_CHUNKS
        for c in range(ATTN_CHUNKS):
            r = slice(c * half, (c + 1) * half)
            acc_ref[r] = acc_ref[r] * _lane_tile(alpha[r], A_KV_RANK // LANES) + _dot(pb[r], val)
        m_ref[...] = m_new
        return carry

    lax.fori_loop(0, nt, attn_body, 0)

    inv = 1.0 / jnp.maximum(l_ref[...], 1e-30)
    o_lat = (acc_ref[...] * _lane_tile(inv, A_KV_RANK // LANES)).astype(BF16)
    for h in range(A_HEADS):
        o_ref[:, h * HEAD_DIM:(h + 1) * HEAD_DIM] = _dot(o_lat[h * qb:(h + 1) * qb], wuv_ref[h]).astype(BF16)


def _dsa_attn(qi, wi, ki, qa, ka, wuv):
    nb, _, t, _ = qa.shape
    topk = min(IDX_TOPK_MAX, t // 4)
    idx_bits = int(np.ceil(np.log2(t)))
    qb = DSA_QB
    body = functools.partial(_dsa_attn_body, topk=topk, idx_bits=idx_bits)
    return pl.pallas_call(
        body,
        grid=(nb, t // qb),
        in_specs=[pl.BlockSpec((None, IDX_HEADS, qb, LANES), lambda b, i: (b, 0, i, 0)),
                  pl.BlockSpec((None, IDX_HEADS, qb, LANES), lambda b, i: (b, 0, i, 0)),
                  pl.BlockSpec((None, t, LANES), lambda b, i: (b, 0, 0)),
                  pl.BlockSpec((None, A_HEADS, qb, QA_DIM), lambda b, i: (b, 0, i, 0)),
                  pl.BlockSpec((None, t, QA_DIM), lambda b, i: (b, 0, 0)),
                  pl.BlockSpec(wuv.shape, lambda b, i: (0, 0, 0))],
        out_specs=pl.BlockSpec((None, qb, A_HEADS * HEAD_DIM), lambda b, i: (b, i, 0)),
        out_shape=jax.ShapeDtypeStruct((nb, t, A_HEADS * HEAD_DIM), BF16),
        scratch_shapes=[pltpu.VMEM((t // KEY_TILE, qb, KEY_TILE), I32),
                        pltpu.VMEM((qb, LANES), I32),
                        pltpu.VMEM((A_HEADS * qb, A_KV_RANK), F32),
                        pltpu.VMEM((A_HEADS * qb, LANES), F32),
                        pltpu.VMEM((A_HEADS * qb, LANES), F32)],
        compiler_params=_cparams(("parallel", "parallel")),
    )(qi, wi, ki, qa, ka, wuv)


def _nsa_prep_body(bq_ref, bkv_ref, tabh_ref, qb_ref, kc_ref, vc_ref, ks_ref, vs_ref, kw_ref, vw_ref):
    tm = bq_ref.shape[0]
    lane = lax.broadcasted_iota(I32, (tm, LANES), 1)
    ch, sh = tabh_ref[:, 0:LANES], tabh_ref[:, LANES:2 * LANES]

    def rope(x):
        rot = jnp.where(lane < ROPE_DIM // 2, pltpu.roll(x, LANES - ROPE_DIM // 2, 1),
                        pltpu.roll(x, ROPE_DIM // 2, 1))
        return x * ch + rot * sh

    scale = HEAD_DIM ** -0.5
    for h in range(B_HEADS):
        sl = slice(h * HEAD_DIM, (h + 1) * HEAD_DIM)
        qb_ref[:, sl] = (rope(bq_ref[:, sl]) * scale).astype(BF16)
    outs = (kc_ref, vc_ref, ks_ref, vs_ref, kw_ref, vw_ref)
    for kind in range(6):
        for g in range(B_GROUPS):
            c0 = (kind * B_GROUPS + g) * HEAD_DIM
            v = bkv_ref[:, c0:c0 + HEAD_DIM]
            if kind % 2 == 0:
                v = rope(v)
            outs[kind][g] = v.astype(BF16)


def _nsa_prep(z, tabh):
    nb, t, _ = z.shape
    tm = 256
    kv_spec = pl.BlockSpec((None, B_GROUPS, tm, HEAD_DIM), lambda b, i: (b, 0, i, 0))
    kv_shape = jax.ShapeDtypeStruct((nb, B_GROUPS, t, HEAD_DIM), BF16)
    return pl.pallas_call(
        _nsa_prep_body,
        grid=(nb, t // tm),
        in_specs=[pl.BlockSpec((None, tm, 1024), lambda b, i: (b, i, Z_BQ // 1024)),
                  pl.BlockSpec((None, tm, 1536), lambda b, i: (b, i, Z_BKV // 1536)),
                  pl.BlockSpec((None, tm, 256), lambda b, i: (b, i, 0))],
        out_specs=[pl.BlockSpec((None, tm, 1024), lambda b, i: (b, i, 0))] + [kv_spec] * 6,
        out_shape=[jax.ShapeDtypeStruct((nb, t, 1024), BF16)] + [kv_shape] * 6,
        compiler_params=_cparams(("parallel", "parallel")),
    )(z, z, tabh)


def _compress_body(ch_ref, pe_ref, w1_ref, w2_ref, o_ref):
    nc = ch_ref.shape[0]
    half = CMP_STRIDE * HEAD_DIM
    x = ch_ref[...]
    ha = _dot(x, w1_ref[0:half, :])
    hb = _dot(x, w1_ref[half:2 * half, :])
    bias = _dot(pe_ref[...], w1_ref[...])[0:1]
    pre = ha + pltpu.roll(hb, nc - 1, 0) + bias
    hid = jax.nn.gelu(pre, approximate=True)
    o_ref[...] = _dot(hid.astype(BF16), w2_ref[...]).astype(BF16)


def _compress(tok, pe, w1, w2):
    nb, ng, t, d = tok.shape
    nc = t // CMP_STRIDE
    chunks = tok.reshape(nb, ng, nc, CMP_STRIDE * d)
    pe8 = jnp.broadcast_to(pe.reshape(1, CMP_LEN * d), (8, CMP_LEN * d)).astype(BF16)
    return pl.pallas_call(
        _compress_body,
        grid=(nb, ng),
        in_specs=[pl.BlockSpec((None, None, nc, CMP_STRIDE * d), lambda b, g: (b, g, 0, 0)),
                  pl.BlockSpec(pe8.shape, lambda b, g: (0, 0)),
                  pl.BlockSpec(w1.shape, lambda b, g: (0, 0)),
                  pl.BlockSpec(w2.shape, lambda b, g: (0, 0))],
        out_specs=pl.BlockSpec((None, None, nc, d), lambda b, g: (b, g, 0, 0)),
        out_shape=jax.ShapeDtypeStruct((nb, ng, nc, d), BF16),
        compiler_params=_cparams(("parallel", "parallel")),
    )(chunks, pe8, w1.astype(BF16), w2.astype(BF16))


def _nsa_attn_body(q_ref, kc_ref, vc_ref, ks_ref, vs_ref, kw_ref, vw_ref, gate_ref, ov_ref, ex_ref,
                   o_ref, key_ref, thr_ref, acc_ref, m_ref, l_ref, *, n_cmp, n_sel):
    qb = NSA_QB
    rows = B_HPG * qb
    nc = kc_ref.shape[0]
    q0 = pl.program_id(2) * qb
    qs = jnp.concatenate([q_ref[:, j * HEAD_DIM:(j + 1) * HEAD_DIM] for j in range(B_HPG)], axis=0)

    tq_c = q0 + lax.broadcasted_iota(I32, (qb, nc), 0)
    n_id = lax.broadcasted_iota(I32, (qb, nc), 1)
    c_ok = jnp.where(n_id < n_cmp, n_id * CMP_STRIDE + (CMP_LEN - 1), tq_c + 1) <= tq_c
    c_bias = _row_tile(jnp.where(c_ok, 0.0, NEG_BIG), B_HPG)
    c_keep = _row_tile(jnp.where(c_ok, 1.0, 0.0), B_HPG)
    sc = _dot_nt(qs, kc_ref[...]) + c_bias
    e = jnp.exp(sc - jnp.max(sc, axis=1, keepdims=True)) * c_keep
    pc = e / jnp.maximum(jnp.sum(e, axis=1, keepdims=True), 1e-30)
    oc = _dot(pc.astype(BF16), vc_ref[...])

    pc4 = pc[0:qb]
    for j in range(1, B_HPG):
        pc4 = pc4 + pc[j * qb:(j + 1) * qb]
    p_hi, p_mid, p_lo = _split3(pc4)
    ov = ov_ref[...]
    imp = _dot(p_hi, ov) + _dot(p_mid, ov) + _dot(p_lo, ov)
    tq_b = q0 + lax.broadcasted_iota(I32, (qb, LANES), 0)
    blk = lax.broadcasted_iota(I32, (qb, LANES), 1)
    cur = tq_b // SLC_LEN
    d_cur = cur - blk
    forced = jnp.where(blk == 0, 1.0, jnp.where(d_cur == 0, 1.0, jnp.where(d_cur == 1, 1.0, 0.0)))
    admissible = blk * SLC_LEN <= tq_b
    free = jnp.where(admissible, jnp.where(forced > 0.5, -1.0, imp), -1.0)
    imp = jnp.where(forced > 0.5, jnp.inf, imp + 0.0)
    key_ref[0] = _f32_key(jnp.where(admissible, imp, -jnp.inf))
    _topk_select(key_ref, thr_ref, 1, n_sel, 7,
                 jnp.full((qb, 1), -1.0, F32), jnp.max(free, axis=1, keepdims=True))
    selm = jnp.where(key_ref[0] > thr_ref[:, 0:1], 1.0, 0.0).astype(BF16)

    nt = (q0 + qb + KEY_TILE - 1) // KEY_TILE
    tq_k = q0 + lax.broadcasted_iota(I32, (qb, KEY_TILE), 0)
    lane_k = lax.broadcasted_iota(I32, (qb, KEY_TILE), 1)
    m_ref[...] = jnp.full((rows, LANES), NEG_BIG, F32)
    l_ref[...] = jnp.zeros((rows, LANES), F32)
    acc_ref[...] = jnp.zeros((rows, HEAD_DIM), F32)

    def sel_body(kt, carry):
        k0 = pl.multiple_of(kt * KEY_TILE, KEY_TILE)
        chosen = _dot(selm, ex_ref[kt])
        ok = jnp.where(chosen > 0.5, k0 + lane_k, tq_k + 1) <= tq_k
        bias = jnp.where(ok, 0.0, NEG_BIG)
        ks = ks_ref[pl.ds(k0, KEY_TILE), :]
        vs = vs_ref[pl.ds(k0, KEY_TILE), :]
        s = _dot_nt(qs, ks) + _row_tile(bias, B_HPG)
        m_old = m_ref[...]
        m_new = jnp.maximum(m_old, jnp.max(s, axis=1, keepdims=True))
        p = jnp.exp(s - _lane_tile(m_new, KEY_TILE // LANES))
        alpha = jnp.exp(m_old - m_new)
        l_ref[...] = l_ref[...] * alpha + jnp.sum(p, axis=1, keepdims=True)
        acc_ref[...] = acc_ref[...] * alpha + _dot(p.astype(BF16), vs)
        m_ref[...] = m_new
        return carry

    lax.fori_loop(0, nt, sel_body, 0)
    os_ = acc_ref[...] / jnp.maximum(l_ref[...], 1e-30)

    span = WINDOW + qb
    start = pl.multiple_of(jnp.maximum(q0 - WINDOW, 0), qb)
    tq_w = q0 + lax.broadcasted_iota(I32, (qb, span), 0)
    wpos = start + lax.broadcasted_iota(I32, (qb, span), 1)
    w_ok = jnp.where(wpos > tq_w - WINDOW, wpos, tq_w + 1) <= tq_w
    w_bias = _row_tile(jnp.where(w_ok, 0.0, NEG_BIG), B_HPG)
    sw = _dot_nt(qs, kw_ref[pl.ds(start, span), :]) + w_bias
    ew = jnp.exp(sw - jnp.max(sw, axis=1, keepdims=True))
    pw = ew / jnp.maximum(jnp.sum(ew, axis=1, keepdims=True), 1e-30)
    ow = _dot(pw.astype(BF16), vw_ref[pl.ds(start, span), :])

    gate = gate_ref[...]
    for j in range(B_HPG):
        r = slice(j * qb, (j + 1) * qb)
        res = (gate[:, 3 * j:3 * j + 1] * oc[r] + gate[:, 3 * j + 1:3 * j + 2] * os_[r]
               + gate[:, 3 * j + 2:3 * j + 3] * ow[r])
        o_ref[:, j * HEAD_DIM:(j + 1) * HEAD_DIM] = res.astype(BF16)


def _nsa_attn(qb_all, kc, vc, ks, vs, kw, vw, gates, ov, ex):
    nb, t, _ = qb_all.shape
    nc = kc.shape[2]
    n_cmp = (t - CMP_LEN) // CMP_STRIDE + 1
    n_sel = min(SLC_TOPN, t // SLC_LEN)
    qb = NSA_QB
    body = functools.partial(_nsa_attn_body, n_cmp=n_cmp, n_sel=n_sel)
    res_c = pl.BlockSpec((None, None, nc, HEAD_DIM), lambda b, g, i: (b, g, 0, 0))
    res_t = pl.BlockSpec((None, None, t, HEAD_DIM), lambda b, g, i: (b, g, 0, 0))
    return pl.pallas_call(
        body,
        grid=(nb, B_GROUPS, t // qb),
        in_specs=[pl.BlockSpec((None, qb, B_HPG * HEAD_DIM), lambda b, g, i: (b, i, g)),
                  res_c, res_c, res_t, res_t, res_t, res_t,
                  pl.BlockSpec((None, None, qb, 3 * B_HPG), lambda b, g, i: (b, g, i, 0)),
                  pl.BlockSpec(ov.shape, lambda b, g, i: (0, 0)),
                  pl.BlockSpec(ex.shape, lambda b, g, i: (0, 0, 0))],
        out_specs=pl.BlockSpec((None, qb, B_HPG * HEAD_DIM), lambda b, g, i: (b, i, g)),
        out_shape=jax.ShapeDtypeStruct((nb, t, B_HEADS * HEAD_DIM), BF16),
        scratch_shapes=[pltpu.VMEM((1, qb, LANES), I32),
                        pltpu.VMEM((qb, LANES), I32),
                        pltpu.VMEM((B_HPG * qb, HEAD_DIM), F32),
                        pltpu.VMEM((B_HPG * qb, LANES), F32),
                        pltpu.VMEM((B_HPG * qb, LANES), F32)],
        compiler_params=_cparams(("parallel", "parallel", "parallel")),
    )(qb_all, kc, vc, ks, vs, kw, vw, gates, ov, ex)


def _merge_body(oa_ref, ob_ref, wa_ref, wb_ref, ga_ref, gb_ref, o_ref):
    a = _dot(oa_ref[...], wa_ref[...])
    b = _dot(ob_ref[...], wb_ref[...])
    o_ref[...] = (jax.nn.sigmoid(ga_ref[...]) * a + jax.nn.sigmoid(gb_ref[...]) * b).astype(BF16)


def _merge(o_a, o_b, w_a, w_b, z):
    nb, t, da = o_a.shape
    d = w_a.shape[1]
    tm, tn = 512, 1024
    return pl.pallas_call(
        _merge_body,
        grid=(nb, t // tm, d // tn),
        in_specs=[pl.BlockSpec((None, tm, da), lambda b, i, j: (b, i, 0)),
                  pl.BlockSpec((None, tm, da), lambda b, i, j: (b, i, 0)),
                  pl.BlockSpec((da, tn), lambda b, i, j: (0, j)),
                  pl.BlockSpec((da, tn), lambda b, i, j: (0, j)),
                  pl.BlockSpec((None, tm, tn), lambda b, i, j: (b, i, j)),
                  pl.BlockSpec((None, tm, tn), lambda b, i, j: (b, i, d // tn + j))],
        out_specs=pl.BlockSpec((None, tm, tn), lambda b, i, j: (b, i, j)),
        out_shape=jax.ShapeDtypeStruct((nb, t, d), BF16),
        compiler_params=_cparams(("parallel", "parallel", "parallel")),
    )(o_a, o_b, w_a, w_b, z, z)


def _outproj_body(mg_ref, wo_ref, x_ref, gpost_ref, gt_ref, gpre_ref, sc_ref, sh_ref,
                  wr_ref, br_ref, x1_ref, h_ref, idx_ref, gate_ref):
    tm = x_ref.shape[0]
    y = _dot(mg_ref[...], wo_ref[...])
    x1 = x_ref[...] + gt_ref[...] * _rms(y, gpost_ref[...])
    x1_ref[...] = x1
    h = _rms(x1, gpre_ref[...]) * (1.0 + sc_ref[...]) + sh_ref[...]
    h_ref[...] = h
    h_hi = h.astype(BF16)
    h_lo = (h - h_hi.astype(F32)).astype(BF16)
    logits = (_dot(h_hi, wr_ref[0]) + (_dot(h_hi, wr_ref[1]) + _dot(h_lo, wr_ref[0]))) + br_ref[...]
    lane = lax.broadcasted_iota(I32, (tm, N_EXPERTS), 1).astype(F32)
    vals, idxs = [], []
    cur = logits
    for _ in range(TOP_K):
        m = jnp.max(cur, axis=1, keepdims=True)
        am = jnp.min(jnp.where(cur == m, lane, float(N_EXPERTS)), axis=1, keepdims=True)
        vals.append(m)
        idxs.append(am)
        cur = jnp.where(lane == am, -jnp.inf, cur)
    es = [jnp.exp(v - vals[0]) for v in vals]
    tot = es[0] + es[1] + es[2] + es[3]
    for k in range(TOP_K):
        idx_ref[:, k:k + 1] = idxs[k].astype(I32)
        gate_ref[:, k:k + 1] = es[k] / tot


def _outproj(merged, w_out, x, g_post, gt1, g_pre, sc2, sh2, wr, br):
    nb, t, d = x.shape
    tm = 256
    vec = lambda: pl.BlockSpec((1, d), lambda b, i: (0, 0))
    mod = lambda: pl.BlockSpec((None, 1, d), lambda b, i: (b, 0, 0))
    row = lambda w: pl.BlockSpec((None, tm, w), lambda b, i: (b, i, 0))
    return pl.pallas_call(
        _outproj_body,
        grid=(nb, t // tm),
        in_specs=[row(d), pl.BlockSpec((d, d), lambda b, i: (0, 0)), row(d), vec(), mod(), vec(), mod(), mod(),
                  pl.BlockSpec(wr.shape, lambda b, i: (0, 0, 0)),
                  pl.BlockSpec((1, N_EXPERTS), lambda b, i: (0, 0))],
        out_specs=[row(d), row(d), row(TOP_K), row(TOP_K)],
        out_shape=[jax.ShapeDtypeStruct((nb, t, d), F32),
                   jax.ShapeDtypeStruct((nb, t, d), F32),
                   jax.ShapeDtypeStruct((nb, t, TOP_K), I32),
                   jax.ShapeDtypeStruct((nb, t, TOP_K), F32)],
        compiler_params=_cparams(("parallel", "parallel")),
    )(merged, w_out, x, g_post[None], gt1, g_pre[None], sc2, sh2, wr, br[None])


def _gmm_body(be_ref, nu_ref, tokc_ref, tokn_ref, h_ref, wg_ref, wu_ref, bg_ref, bu_ref, wd_ref, bd_ref,
              y_ref, xf_ref, xb_ref, sem, *, ff_steps):
    i, j = pl.program_id(0), pl.program_id(1)
    n_used = nu_ref[0]
    slot = i % 2
    last = ff_steps - 1
    rows_per_step = ROW_BLK // ff_steps

    def row_copy(tok_ref, r, s):
        return pltpu.make_async_copy(h_ref.at[pl.ds(tok_ref[0, 0, r], 1), :],
                                     xf_ref.at[s, pl.ds(r, 1), :], sem.at[s])

    def wait_slot(s):
        pltpu.make_async_copy(h_ref.at[pl.ds(0, ROW_BLK), :], xf_ref.at[s], sem.at[s]).wait()

    @pl.when(j == 0)
    def _():
        @pl.when(i == 0)
        def _():
            def body(r, carry):
                row_copy(tokc_ref, r, 0).start()
                return carry

            lax.fori_loop(0, ROW_BLK, body, 0)

        @pl.when(i <= n_used)
        def _():
            wait_slot(slot)

        @pl.when(i < n_used)
        def _():
            xb_ref[...] = xf_ref[slot].astype(BF16)

    @pl.when(i < n_used)
    def _():
        x = xb_ref[...]
        g = jnp.minimum(_dot(x, wg_ref[...].astype(BF16)) + bg_ref[...], SWIGLU_LIMIT)
        u = jnp.clip(_dot(x, wu_ref[...].astype(BF16)) + bu_ref[...], -SWIGLU_LIMIT, SWIGLU_LIMIT)
        act = g * jax.nn.sigmoid(SWIGLU_ALPHA * g) * (u + 1.0)
        part = _dot(act.astype(BF16), wd_ref[...].astype(BF16))
        for r in range(rows_per_step):
            row_copy(tokn_ref, j * rows_per_step + r, 1 - slot).start()

        @pl.when(j == 0)
        def _():
            y_ref[...] = part + bd_ref[...]

        @pl.when(j > 0)
        def _():
            y_ref[...] = y_ref[...] + part

    @pl.when(jnp.logical_and(i >= n_used, j == last))
    def _():
        y_ref[...] = jnp.zeros(y_ref.shape, F32)

    @pl.when(jnp.logical_and(jnp.logical_and(i == pl.num_programs(0) - 1, j == last), i < n_used))
    def _():
        wait_slot(1 - slot)


def _gmm(blk_e, n_used, row_tok, h, wgu, bgu, wd, bd):
    d = h.shape[1]
    ff = wd.shape[1]
    p = row_tok.shape[0]
    nblk = p // ROW_BLK
    nft = ff // FF_TILE
    tok3 = row_tok.reshape(nblk, 1, ROW_BLK)

    def expert(i, be, nu):
        return be[jnp.minimum(i, nu[0] - 1)]

    def tile(i, j, nu):
        return jnp.where(i < nu[0], j, nft - 1)

    grid_spec = pltpu.PrefetchScalarGridSpec(
        num_scalar_prefetch=2,
        grid=(nblk, nft),
        in_specs=[pl.BlockSpec((1, 1, ROW_BLK), lambda i, j, be, nu: (i, 0, 0), memory_space=pltpu.SMEM),
                  pl.BlockSpec((1, 1, ROW_BLK), lambda i, j, be, nu: (jnp.minimum(i + 1, nblk - 1), 0, 0),
                               memory_space=pltpu.SMEM),
                  pl.BlockSpec(memory_space=pl.ANY),
                  pl.BlockSpec((None, d, FF_TILE), lambda i, j, be, nu: (expert(i, be, nu), 0, tile(i, j, nu))),
                  pl.BlockSpec((None, d, FF_TILE),
                               lambda i, j, be, nu: (expert(i, be, nu), 0, nft + tile(i, j, nu))),
                  pl.BlockSpec((None, 1, FF_TILE), lambda i, j, be, nu: (expert(i, be, nu), 0, tile(i, j, nu))),
                  pl.BlockSpec((None, 1, FF_TILE),
                               lambda i, j, be, nu: (expert(i, be, nu), 0, nft + tile(i, j, nu))),
                  pl.BlockSpec((None, FF_TILE, d), lambda i, j, be, nu: (expert(i, be, nu), tile(i, j, nu), 0)),
                  pl.BlockSpec((None, 1, d), lambda i, j, be, nu: (expert(i, be, nu), 0, 0))],
        out_specs=pl.BlockSpec((ROW_BLK, d), lambda i, j, be, nu: (i, 0)),
        scratch_shapes=[pltpu.VMEM((2, ROW_BLK, d), F32),
                        pltpu.VMEM((ROW_BLK, d), BF16),
                        pltpu.SemaphoreType.DMA((2,))],
    )
    assert ROW_BLK % nft == 0
    return pl.pallas_call(
        functools.partial(_gmm_body, ff_steps=nft),
        grid_spec=grid_spec,
        out_shape=jax.ShapeDtypeStruct((p, d), F32),
        compiler_params=_cparams(("arbitrary", "arbitrary")),
    )(blk_e, n_used, tok3, tok3, h, wgu, wgu, bgu, bgu, wd, bd)


def _combine_body(destc_ref, destn_ref, ys_ref, gate_ref, x1_ref, g_ref, gt_ref, o_ref, buf, sem):
    tm = CMB_TOK
    i = pl.program_id(0)
    slot = i % 2

    def issue(dest_ref, s):
        def body(r, carry):
            pltpu.make_async_copy(ys_ref.at[pl.ds(dest_ref[0, 0, r], 1), :],
                                  buf.at[s, pl.ds(r, 1), :], sem.at[s]).start()
            return carry

        lax.fori_loop(0, TOP_K * tm, body, 0, unroll=16)

    @pl.when(i == 0)
    def _():
        issue(destc_ref, 0)

    pltpu.make_async_copy(ys_ref.at[pl.ds(0, TOP_K * tm), :], buf.at[slot], sem.at[slot]).wait()

    @pl.when(i + 1 < pl.num_programs(0))
    def _():
        issue(destn_ref, 1 - slot)

    gate = gate_ref[...]
    y = buf[slot, 0:tm] * gate[:, 0:1]
    for k in range(1, TOP_K):
        y = y + buf[slot, k * tm:(k + 1) * tm] * gate[:, k:k + 1]
    o_ref[...] = x1_ref[...] + gt_ref[...] * _rms(y, g_ref[...])


def _combine(dest, ys, gates, x1, g_post, gt2):
    nb, t, d = x1.shape
    n = nb * t
    tm = CMB_TOK
    steps = n // tm
    dest_t = dest.reshape(steps, tm, TOP_K).transpose(0, 2, 1).reshape(steps, 1, TOP_K * tm)
    out = pl.pallas_call(
        _combine_body,
        grid=(steps,),
        in_specs=[pl.BlockSpec((1, 1, TOP_K * tm), lambda i: (i, 0, 0), memory_space=pltpu.SMEM),
                  pl.BlockSpec((1, 1, TOP_K * tm), lambda i: (jnp.minimum(i + 1, steps - 1), 0, 0),
                               memory_space=pltpu.SMEM),
                  pl.BlockSpec(memory_space=pl.ANY),
                  pl.BlockSpec((tm, TOP_K), lambda i: (i, 0)),
                  pl.BlockSpec((tm, d), lambda i: (i, 0)),
                  pl.BlockSpec((1, d), lambda i: (0, 0)),
                  pl.BlockSpec((None, 1, d), lambda i: ((i * tm) // t, 0, 0))],
        out_specs=pl.BlockSpec((tm, d), lambda i: (i, 0)),
        out_shape=jax.ShapeDtypeStruct((n, d), F32),
        scratch_shapes=[pltpu.VMEM((2, TOP_K * tm, d), F32), pltpu.SemaphoreType.DMA((2,))],
        compiler_params=_cparams(("arbitrary",)),
    )(dest_t, dest_t, ys, gates, x1.reshape(n, d), g_post[None], gt2)
    return out.reshape(nb, t, d)


def _route(top_idx):
    n = top_idx.shape[0]
    nk = n * TOP_K
    e = top_idx.reshape(nk)
    onehot = (e[:, None] == jnp.arange(N_EXPERTS, dtype=I32)[None, :]).astype(I32)
    csum = jnp.cumsum(onehot, axis=0)
    rank = jnp.sum((csum - onehot) * onehot, axis=1)
    counts = csum[-1]
    padded = (counts + ROW_BLK - 1) // ROW_BLK * ROW_BLK
    pend = jnp.cumsum(padded)
    pstart = pend - padded
    dest = (pstart[e] + rank).astype(I32)
    nblk = nk // ROW_BLK + N_EXPERTS
    tok = jnp.arange(nk, dtype=I32) // TOP_K
    row_tok = jnp.zeros((nblk * ROW_BLK,), I32).at[dest].set(tok, unique_indices=True)
    blk_start = jnp.arange(nblk, dtype=I32) * ROW_BLK
    blk_e = jnp.minimum(jnp.sum((pend[None, :] <= blk_start[:, None]).astype(I32), axis=1), N_EXPERTS - 1)
    n_used = (pend[-1:] // ROW_BLK).astype(I32)
    return dest.reshape(n, TOP_K), row_tok, blk_e.astype(I32), n_used


def _rot_cols(w, half):
    z = jnp.zeros_like(w)
    z = z.at[..., 0:half].set(w[..., half:2 * half])
    return z.at[..., half:2 * half].set(w[..., 0:half])


def _prep_w_in(w_in):
    d = w_in.shape[0]
    offs = np.cumsum([0, A_Q_RANK, A_KV_RANK, ROPE_DIM, IDX_DIM, IDX_HEADS, B_HEADS * HEAD_DIM,
                      6 * B_GROUPS * HEAD_DIM, 3 * B_HEADS, 2 * d])
    cq, ckv, krope, kidx, widx, bq, bkv, bgate, mgate = [w_in[:, int(offs[i]):int(offs[i + 1])] for i in range(9)]
    krope_rot = jnp.concatenate([krope[:, ROPE_DIM // 2:], krope[:, :ROPE_DIM // 2]], axis=1)
    small = jnp.concatenate([krope, kidx, krope_rot, widx, bgate,
                             jnp.zeros((d, 256 - 160), w_in.dtype)], axis=1)
    w = jnp.concatenate([mgate, bq, cq, ckv, small, bkv], axis=1)
    assert w.shape[1] == Z_COLS
    return w.astype(BF16)


def _rope_tables(positions):
    def tab(rot_dim, lo, width):
        inv = jnp.power(ROPE_THETA, -jnp.arange(0, rot_dim, 2, dtype=F32) / rot_dim)
        ang = positions.astype(F32)[..., None] * inv
        c, s = jnp.cos(ang), jnp.sin(ang)
        shape = positions.shape
        cos_t = jnp.concatenate([jnp.zeros(shape + (lo,), F32), c, c,
                                 jnp.ones(shape + (width - rot_dim,), F32),
                                 jnp.zeros(shape + (LANES - lo - width,), F32)], axis=-1)
        sin_t = jnp.concatenate([jnp.zeros(shape + (lo,), F32), -s, s,
                                 jnp.zeros(shape + (LANES - lo - rot_dim,), F32)], axis=-1)
        return jnp.concatenate([cos_t, sin_t], axis=-1)

    return tab(ROPE_DIM, 0, LANES), tab(IDX_ROPE, 32, IDX_DIM)


def _select_tables(t):
    n_cmp = (t - CMP_LEN) // CMP_STRIDE + 1
    nc = t // CMP_STRIDE
    n = np.arange(nc)[:, None]
    m = np.arange(LANES)[None, :]
    ov = (n * CMP_STRIDE < m * SLC_LEN + SLC_LEN) & (n * CMP_STRIDE + CMP_LEN - 1 >= m * SLC_LEN) & (n < n_cmp)
    key_blk = (np.arange(t) // SLC_LEN).reshape(t // KEY_TILE, 1, KEY_TILE)
    ex = key_blk == np.arange(LANES).reshape(1, LANES, 1)
    return jnp.asarray(ov, BF16), jnp.asarray(ex, BF16)


def kernel(x, c, positions, w_ada, b_ada, g_pre_mix, g_post_mix, g_pre_ffn, g_post_ffn, w_in, a_cq_norm, a_ckv_norm, a_w_uq, a_w_uk, a_w_uv, idx_w_q, idx_k_norm_g, idx_k_norm_b, cmp_k_pe, cmp_k_w1, cmp_k_w2, cmp_v_pe, cmp_v_w1, cmp_v_w2, w_br_a, w_br_b, w_out, w_router, b_router, w_gate_up, b_gate_up, w_down, b_down):
    nb, t, d = x.shape
    n = nb * t
    assert t % KEY_TILE == 0 and t // SLC_LEN <= LANES and d % LANES == 0

    mod = _adaln(c, w_ada, b_ada)
    sh1, sc1, gt1, sh2, sc2, gt2 = [m[:, None, :] for m in jnp.split(mod, 6, axis=-1)]

    z = _inproj(x, g_pre_mix, sc1, sh1, _prep_w_in(w_in))
    tabh, tabi = _rope_tables(positions)

    wuq = jnp.transpose(a_w_uq, (1, 0, 2))
    wuk = jnp.transpose(a_w_uk, (1, 2, 0))
    wuk = jnp.concatenate([jnp.zeros((A_HEADS, ROPE_DIM, A_KV_RANK), wuk.dtype), wuk], axis=1)
    widx = jnp.transpose(idx_w_q, (1, 0, 2))
    pad_idx = lambda w: jnp.pad(w, ((0, 0), (0, 0), (32, LANES - 32 - IDX_DIM)))
    pad_vec = lambda v: jnp.pad(v, (32, LANES - 32 - IDX_DIM))[None]
    qa, ka, qi, ki, wi, sg = _dsa_prep(
        z, tabh, tabi, a_cq_norm[None], a_ckv_norm[None],
        wuq.astype(BF16), _rot_cols(wuq, ROPE_DIM // 2).astype(BF16), wuk.astype(BF16),
        pad_idx(widx).astype(BF16), pad_idx(_rot_cols(widx, IDX_ROPE // 2)).astype(BF16),
        pad_vec(idx_k_norm_g), pad_vec(idx_k_norm_b))
    o_a = _dsa_attn(qi, wi, ki, qa, ka, a_w_uv.astype(BF16))

    qb_all, kc_tok, vc_tok, ks, vs, kw, vw = _nsa_prep(z, tabh)
    kc = _compress(kc_tok, cmp_k_pe, cmp_k_w1, cmp_k_w2)
    vc = _compress(vc_tok, cmp_v_pe, cmp_v_w1, cmp_v_w2)
    gates_b = sg[:, :, IDX_HEADS:IDX_HEADS + 3 * B_HEADS].reshape(nb, t, B_GROUPS, 3 * B_HPG).transpose(0, 2, 1, 3)
    ov, ex = _select_tables(t)
    o_b = _nsa_attn(qb_all, kc, vc, ks, vs, kw, vw, gates_b, ov, ex)

    merged = _merge(o_a, o_b, w_br_a.astype(BF16), w_br_b.astype(BF16), z)

    wr_hi = w_router.astype(BF16)
    wr = jnp.stack([wr_hi, (w_router - wr_hi.astype(F32)).astype(BF16)])
    x1, h2, top_idx, gates = _outproj(merged, w_out.astype(BF16), x, g_post_mix, gt1, g_pre_ffn, sc2, sh2,
                                      wr, b_router)
    dest, row_tok, blk_e, n_used = _route(top_idx.reshape(n, TOP_K))
    ys = _gmm(blk_e, n_used, row_tok, h2.reshape(n, d), w_gate_up, b_gate_up[:, None, :],
              w_down, b_down[:, None, :])
    return _combine(dest, ys, gates.reshape(n, TOP_K), x1, g_post_ffn, gt2)
```

```python
import functools

import numpy as np
import jax
import jax.numpy as jnp
from jax import lax
from jax.experimental import pallas as pl
from jax.experimental.pallas import tpu as pltpu

F32 = jnp.float32
BF16 = jnp.bfloat16
I32 = jnp.int32

HEAD_DIM = 128
ROPE_DIM = 32
ROPE_THETA = 500000.0
NORM_EPS = 1e-6
Q_BLOCK = 128
A_HEADS = 8
A_Q_RANK = 512
A_KV_RANK = 256
IDX_HEADS = 8
IDX_DIM = 64
IDX_ROPE = 16
IDX_TOPK_MAX = 256
B_HEADS = 8
B_GROUPS = 2
B_HPG = 4
CMP_LEN = 32
CMP_STRIDE = 16
CMP_HIDDEN = 256
SLC_LEN = 64
SLC_TOPN = 16
WINDOW = 512
N_EXPERTS = 32
TOP_K = 4
SWIGLU_ALPHA = 1.702
SWIGLU_LIMIT = 7.0

LANES = 128
VMEM_LIMIT = 58 * 1024 * 1024

KEY_TILE = 512
QA_DIM = A_KV_RANK + LANES
ROW_BLK = 512
FF_TILE = 512
CMB_TOK = 256
ATTN_CHUNKS = 4
NSA_QB = 512
DSA_QB = 128

Z_MGATE, Z_BQ, Z_CQ, Z_CKV, Z_SMALL, Z_BKV, Z_COLS = 0, 4096, 5120, 5632, 5888, 6144, 7680

NEG_BIG = -1e30
INT_MIN = np.int32(-2147483648)
KEY_NEG_INF = np.int32(-2139095041)
BISECT_MAX_ITER = 24


def _cparams(sem):
    return pltpu.CompilerParams(dimension_semantics=sem, vmem_limit_bytes=VMEM_LIMIT)


def _dot(a, b):
    return jnp.dot(a, b, preferred_element_type=F32)


def _dot_nt(a, b):
    return lax.dot_general(a, b, (((1,), (1,)), ((), ())), preferred_element_type=F32)


def _split3(a):
    hi = a.astype(BF16)
    r1 = a - hi.astype(F32)
    mid = r1.astype(BF16)
    lo = (r1 - mid.astype(F32)).astype(BF16)
    return hi, mid, lo


def _lane_tile(a, n):
    return jnp.concatenate([a] * n, axis=1) if n > 1 else a


def _row_tile(a, n):
    return jnp.concatenate([a] * n, axis=0) if n > 1 else a


def _f32_key(s):
    bits = pltpu.bitcast(s, I32)
    return bits ^ (jnp.right_shift(bits, 31) & np.int32(0x7FFFFFFF))


def _count(key_ref, nt, pred):
    rows, width = key_ref.shape[1], key_ref.shape[2]

    def body(kt, acc):
        hit = pred(key_ref[kt], kt)
        for c in range(width // LANES):
            acc = jnp.where(hit[:, c * LANES:(c + 1) * LANES], acc + 1.0, acc)
        return acc

    acc = lax.fori_loop(0, nt, body, jnp.zeros((rows, LANES), F32))
    return jnp.sum(acc, axis=1, keepdims=True)


def _kth_key(key_ref, nt, k):
    rows = key_ref.shape[1]

    def bit_body(b, ans):
        cand = ans | jnp.left_shift(jnp.int32(1), 31 - b)
        cs = cand ^ INT_MIN
        cnt = _count(key_ref, nt, lambda kk, kt: kk >= cs)
        return jnp.where(cnt >= k, cand, ans)

    ans = lax.fori_loop(0, 32, bit_body, jnp.zeros((rows, 1), I32))
    return ans ^ INT_MIN


def _break_ties(key_ref, nt, t, k, idx_bits):
    rows, width = key_ref.shape[1], key_ref.shape[2]
    cnt_gt = _count(key_ref, nt, lambda kk, kt: kk > t)
    cnt_ge = _count(key_ref, nt, lambda kk, kt: kk >= t)
    need = k - cnt_gt

    @pl.when(jnp.max(cnt_ge) > k)
    def _():
        lane = lax.broadcasted_iota(I32, (rows, width), 1)

        def bit_body(b, c0):
            cand = c0 | jnp.left_shift(jnp.int32(1), idx_bits - 1 - b)
            f = _count(key_ref, nt, lambda kk, kt: jnp.where(kk == t, kt * width + lane, cand) < cand)
            return jnp.where(f < need, cand, c0)

        c0 = lax.fori_loop(0, idx_bits, bit_body, jnp.zeros((rows, 1), I32))

        def fix(kt, carry):
            kk = key_ref[kt]
            idx = kt * width + lane
            key_ref[kt] = jnp.where(jnp.where(kk == t, idx, c0) > c0, t - 1, kk)
            return carry

        lax.fori_loop(0, nt, fix, 0)


def _topk_select(key_ref, thr_ref, nt, k, idx_bits, lo0, hi0):
    rows = key_ref.shape[1]
    n_fin = _count(key_ref, nt, lambda kk, kt: kk > KEY_NEG_INF)
    done0 = jnp.where(n_fin <= k, 1.0, 0.0)
    thr0 = jnp.full((rows, 1), KEY_NEG_INF, I32)

    def cond(st):
        it, _, _, _, done, stuck = st
        return jnp.logical_and(it < BISECT_MAX_ITER, jnp.min(jnp.maximum(done, stuck)) < 0.5)

    def body(st):
        it, lo, hi, thr, done, stuck = st
        mid = 0.5 * lo + 0.5 * hi
        midk = _f32_key(mid)
        cnt = _count(key_ref, nt, lambda kk, kt: kk > midk)
        hit = jnp.where(cnt == k, 1.0 - done, 0.0)
        thr = jnp.where(hit > 0.5, midk, thr)
        done = jnp.maximum(done, hit)
        stuck = jnp.maximum(stuck, jnp.where(mid <= lo, 1.0, jnp.where(mid >= hi, 1.0, 0.0)))
        above = cnt >= k
        return it + 1, jnp.where(above, mid, lo), jnp.where(above, hi, mid), thr, done, stuck

    init = (jnp.int32(0), lo0, hi0, thr0, done0, jnp.zeros((rows, 1), F32))
    _, _, _, thr, done, _ = lax.while_loop(cond, lambda st: body(body(st)), init)
    thr_ref[...] = jnp.broadcast_to(thr, thr_ref.shape)

    @pl.when(jnp.min(done) < 0.5)
    def _():
        t = _kth_key(key_ref, nt, k)
        _break_ties(key_ref, nt, t, k, idx_bits)
        thr_ref[...] = jnp.broadcast_to(t - 1, thr_ref.shape)


def _ada_body(c_ref, w_ref, b_ref, o_ref):
    c = c_ref[...]
    s = c * jax.nn.sigmoid(c)
    s_hi, s_mid, s_lo = _split3(s)
    w_hi, w_mid, w_lo = _split3(w_ref[...])
    acc = _dot(s_hi, w_hi) + (_dot(s_hi, w_mid) + _dot(s_mid, w_hi))
    acc = acc + (_dot(s_hi, w_lo) + _dot(s_mid, w_mid) + _dot(s_lo, w_hi))
    o_ref[...] = acc + b_ref[...]


def _adaln(c, w_ada, b_ada):
    nb, d = c.shape
    n = w_ada.shape[1]
    tn = 1024
    cp = jnp.zeros((8, d), F32).at[:nb].set(c)
    out = pl.pallas_call(
        _ada_body,
        grid=(n // tn,),
        in_specs=[pl.BlockSpec((8, d), lambda j: (0, 0)),
                  pl.BlockSpec((d, tn), lambda j: (0, j)),
                  pl.BlockSpec((1, tn), lambda j: (0, j))],
        out_specs=pl.BlockSpec((8, tn), lambda j: (0, j)),
        out_shape=jax.ShapeDtypeStruct((8, n), F32),
        compiler_params=_cparams(("parallel",)),
    )(cp, w_ada, b_ada[None])
    return out[:nb]


def _inproj_body(x_ref, g_ref, sc_ref, sh_ref, w_ref, z_ref, h_ref):
    @pl.when(pl.program_id(2) == 0)
    def _():
        x = x_ref[...]
        y = x * lax.rsqrt(jnp.mean(x * x, axis=-1, keepdims=True) + NORM_EPS) * g_ref[...]
        h_ref[...] = (y * (1.0 + sc_ref[...]) + sh_ref[...]).astype(BF16)

    z_ref[...] = _dot(h_ref[...], w_ref[...])


def _inproj(x, g, sc, sh, w):
    nb, t, d = x.shape
    tm, tn = 512, 1536
    return pl.pallas_call(
        _inproj_body,
        grid=(nb, t // tm, Z_COLS // tn),
        in_specs=[pl.BlockSpec((None, tm, d), lambda b, i, j: (b, i, 0)),
                  pl.BlockSpec((1, d), lambda b, i, j: (0, 0)),
                  pl.BlockSpec((None, 1, d), lambda b, i, j: (b, 0, 0)),
                  pl.BlockSpec((None, 1, d), lambda b, i, j: (b, 0, 0)),
                  pl.BlockSpec((d, tn), lambda b, i, j: (0, j))],
        out_specs=pl.BlockSpec((None, tm, tn), lambda b, i, j: (b, i, j)),
        out_shape=jax.ShapeDtypeStruct((nb, t, Z_COLS), F32),
        scratch_shapes=[pltpu.VMEM((tm, d), BF16)],
        compiler_params=_cparams(("parallel", "parallel", "arbitrary")),
    )(x, g[None], sc, sh, w)


def _rms(x, g):
    return x * lax.rsqrt(jnp.mean(x * x, axis=-1, keepdims=True) + NORM_EPS) * g


def _dsa_prep_body(cq_ref, ckv_ref, sm_ref, tabh_ref, tabi_ref, gq_ref, gkv_ref,
                   wuq_ref, wuqr_ref, wuk_ref, widx_ref, widxr_ref, lng_ref, lnb_ref,
                   qa_ref, ka_ref, qi_ref, ki_ref, wi_ref, sg_ref):
    tm = cq_ref.shape[0]
    lane = lax.broadcasted_iota(I32, (tm, LANES), 1)
    cqn = _rms(cq_ref[...], gq_ref[...]).astype(BF16)
    ckvn = _rms(ckv_ref[...], gkv_ref[...])
    s0 = sm_ref[:, 0:LANES]
    s1 = sm_ref[:, LANES:2 * LANES]
    ch, sh = tabh_ref[:, 0:LANES], tabh_ref[:, LANES:2 * LANES]
    ci, si = tabi_ref[:, 0:LANES], tabi_ref[:, LANES:2 * LANES]

    k_rope = s0 * ch + pltpu.roll(s0, 32, 1) * sh
    ka_ref[:, 0:A_KV_RANK] = ckvn.astype(BF16)
    ka_ref[:, A_KV_RANK:QA_DIM] = jnp.where(lane < ROPE_DIM, k_rope, 0.0).astype(BF16)

    inside = jnp.where(lane >= 32, jnp.where(lane < 96, 1.0, 0.0), 0.0)
    mu = jnp.sum(s0 * inside, axis=-1, keepdims=True) * (1.0 / IDX_DIM)
    xc = (s0 - mu) * inside
    var = jnp.sum(xc * xc, axis=-1, keepdims=True) * (1.0 / IDX_DIM)
    y = xc * lax.rsqrt(var + NORM_EPS) * lng_ref[...] + lnb_ref[...]
    rot = jnp.where(lane < 40, pltpu.roll(y, LANES - 8, 1), pltpu.roll(y, 8, 1))
    ki_ref[...] = (y * ci + rot * si).astype(BF16)

    scale = HEAD_DIM ** -0.5
    w_scale = IDX_HEADS ** -0.5 * IDX_DIM ** -0.5
    for h in range(A_HEADS):
        qh = _dot(cqn, wuq_ref[h])
        qr = _dot(cqn, wuqr_ref[h])
        roped = qh * ch + qr * sh
        qabs = _dot(roped.astype(BF16), wuk_ref[h])
        qa_ref[h, :, 0:A_KV_RANK] = (qabs * scale).astype(BF16)
        qa_ref[h, :, A_KV_RANK:QA_DIM] = (jnp.where(lane < ROPE_DIM, roped, 0.0) * scale).astype(BF16)
        qih = _dot(cqn, widx_ref[h])
        qir = _dot(cqn, widxr_ref[h])
        qi_ref[h] = (qih * ci + qir * si).astype(BF16)
        wi_ref[h] = jnp.broadcast_to(s1[:, h:h + 1], (tm, LANES)) * w_scale
    sg_ref[...] = jax.nn.sigmoid(s1)


def _dsa_prep(z, tabh, tabi, gq, gkv, wuq, wuqr, wuk, widx, widxr, lng, lnb):
    nb, t, _ = z.shape
    tm = 256
    full = lambda a: pl.BlockSpec(a.shape, lambda b, i: (0,) * a.ndim)
    return pl.pallas_call(
        _dsa_prep_body,
        grid=(nb, t // tm),
        in_specs=[pl.BlockSpec((None, tm, A_Q_RANK), lambda b, i: (b, i, Z_CQ // A_Q_RANK)),
                  pl.BlockSpec((None, tm, A_KV_RANK), lambda b, i: (b, i, Z_CKV // A_KV_RANK)),
                  pl.BlockSpec((None, tm, 256), lambda b, i: (b, i, Z_SMALL // 256)),
                  pl.BlockSpec((None, tm, 256), lambda b, i: (b, i, 0)),
                  pl.BlockSpec((None, tm, 256), lambda b, i: (b, i, 0)),
                  full(gq), full(gkv), full(wuq), full(wuqr), full(wuk), full(widx), full(widxr),
                  full(lng), full(lnb)],
        out_specs=[pl.BlockSpec((None, A_HEADS, tm, QA_DIM), lambda b, i: (b, 0, i, 0)),
                   pl.BlockSpec((None, tm, QA_DIM), lambda b, i: (b, i, 0)),
                   pl.BlockSpec((None, IDX_HEADS, tm, LANES), lambda b, i: (b, 0, i, 0)),
                   pl.BlockSpec((None, tm, LANES), lambda b, i: (b, i, 0)),
                   pl.BlockSpec((None, IDX_HEADS, tm, LANES), lambda b, i: (b, 0, i, 0)),
                   pl.BlockSpec((None, tm, LANES), lambda b, i: (b, i, 0))],
        out_shape=[jax.ShapeDtypeStruct((nb, A_HEADS, t, QA_DIM), BF16),
                   jax.ShapeDtypeStruct((nb, t, QA_DIM), BF16),
                   jax.ShapeDtypeStruct((nb, IDX_HEADS, t, LANES), BF16),
                   jax.ShapeDtypeStruct((nb, t, LANES), BF16),
                   jax.ShapeDtypeStruct((nb, IDX_HEADS, t, LANES), F32),
                   jax.ShapeDtypeStruct((nb, t, LANES), F32)],
        compiler_params=_cparams(("parallel", "parallel")),
    )(z, z, z, tabh, tabi, gq, gkv, wuq, wuqr, wuk, widx, widxr, lng, lnb)


def _fold_lanes(x, op):
    out = x[:, 0:LANES]
    for c in range(1, x.shape[1] // LANES):
        out = op(out, x[:, c * LANES:(c + 1) * LANES])
    return out


def _dsa_attn_body(qi_ref, wi_ref, ki_ref, qa_ref, ka_ref, wuv_ref, o_ref,
                   key_ref, thr_ref, acc_ref, m_ref, l_ref, *, topk, idx_bits):
    qb = DSA_QB
    rows = A_HEADS * qb
    q0 = pl.program_id(1) * qb
    nt = (q0 + qb + KEY_TILE - 1) // KEY_TILE
    tq = q0 + lax.broadcasted_iota(I32, (qb, KEY_TILE), 0)
    lane = lax.broadcasted_iota(I32, (qb, KEY_TILE), 1)

    def score_body(kt, carry):
        smin, smax = carry
        k0 = pl.multiple_of(kt * KEY_TILE, KEY_TILE)
        ki = ki_ref[pl.ds(k0, KEY_TILE), :]
        s = None
        for h in range(IDX_HEADS):
            r = jnp.maximum(_dot_nt(qi_ref[h], ki), 0.0) * _lane_tile(wi_ref[h], KEY_TILE // LANES)
            s = r if s is None else s + r
        s = s + 0.0
        causal = k0 + lane <= tq
        key_ref[kt] = _f32_key(jnp.where(causal, s, -jnp.inf))
        smin = jnp.minimum(smin, _fold_lanes(jnp.where(causal, s, jnp.inf), jnp.minimum))
        smax = jnp.maximum(smax, _fold_lanes(jnp.where(causal, s, -jnp.inf), jnp.maximum))
        return smin, smax

    smin, smax = lax.fori_loop(0, nt, score_body,
                               (jnp.full((qb, LANES), jnp.inf, F32), jnp.full((qb, LANES), -jnp.inf, F32)))
    _topk_select(key_ref, thr_ref, nt, topk, idx_bits,
                 jnp.min(smin, axis=1, keepdims=True), jnp.max(smax, axis=1, keepdims=True))
    thr = thr_ref[:, 0:1]

    m_ref[...] = jnp.full((rows, LANES), NEG_BIG, F32)
    l_ref[...] = jnp.zeros((rows, LANES), F32)
    acc_ref[...] = jnp.zeros((rows, A_KV_RANK), F32)

    def attn_body(kt, carry):
        k0 = pl.multiple_of(kt * KEY_TILE, KEY_TILE)
        ka = ka_ref[pl.ds(k0, KEY_TILE), :]
        val = ka[:, 0:A_KV_RANK]
        sel = jnp.where(key_ref[kt] > thr, k0 + lane, tq + 1) <= tq
        bias = jnp.where(sel, 0.0, NEG_BIG)
        s = _dot_nt(qa_ref[...].reshape(rows, QA_DIM), ka) + _row_tile(bias, A_HEADS)
        m_old = m_ref[...]
        m_new = jnp.maximum(m_old, jnp.max(s, axis=1, keepdims=True))
        p = jnp.exp(s - _lane_tile(m_new, KEY_TILE // LANES))
        alpha = jnp.exp(m_old - m_new)
        l_ref[...] = l_ref[...] * alpha + jnp.sum(p, axis=1, keepdims=True)
        pb = p.astype(BF16)
        half = rows // ATTN_CHUNKS
        for c in range(ATTN_CHUNKS):
            r = slice(c * half, (c + 1) * half)
            acc_ref[r] = acc_ref[r] * _lane_tile(alpha[r], A_KV_RANK // LANES) + _dot(pb[r], val)
        m_ref[...] = m_new
        return carry

    lax.fori_loop(0, nt, attn_body, 0)

    inv = 1.0 / jnp.maximum(l_ref[...], 1e-30)
    o_lat = (acc_ref[...] * _lane_tile(inv, A_KV_RANK // LANES)).astype(BF16)
    for h in range(A_HEADS):
        o_ref[:, h * HEAD_DIM:(h + 1) * HEAD_DIM] = _dot(o_lat[h * qb:(h + 1) * qb], wuv_ref[h]).astype(BF16)


def _dsa_attn(qi, wi, ki, qa, ka, wuv):
    nb, _, t, _ = qa.shape
    topk = min(IDX_TOPK_MAX, t // 4)
    idx_bits = int(np.ceil(np.log2(t)))
    qb = DSA_QB
    body = functools.partial(_dsa_attn_body, topk=topk, idx_bits=idx_bits)
    return pl.pallas_call(
        body,
        grid=(nb, t // qb),
        in_specs=[pl.BlockSpec((None, IDX_HEADS, qb, LANES), lambda b, i: (b, 0, i, 0)),
                  pl.BlockSpec((None, IDX_HEADS, qb, LANES), lambda b, i: (b, 0, i, 0)),
                  pl.BlockSpec((None, t, LANES), lambda b, i: (b, 0, 0)),
                  pl.BlockSpec((None, A_HEADS, qb, QA_DIM), lambda b, i: (b, 0, i, 0)),
                  pl.BlockSpec((None, t, QA_DIM), lambda b, i: (b, 0, 0)),
                  pl.BlockSpec(wuv.shape, lambda b, i: (0, 0, 0))],
        out_specs=pl.BlockSpec((None, qb, A_HEADS * HEAD_DIM), lambda b, i: (b, i, 0)),
        out_shape=jax.ShapeDtypeStruct((nb, t, A_HEADS * HEAD_DIM), BF16),
        scratch_shapes=[pltpu.VMEM((t // KEY_TILE, qb, KEY_TILE), I32),
                        pltpu.VMEM((qb, LANES), I32),
                        pltpu.VMEM((A_HEADS * qb, A_KV_RANK), F32),
                        pltpu.VMEM((A_HEADS * qb, LANES), F32),
                        pltpu.VMEM((A_HEADS * qb, LANES), F32)],
        compiler_params=_cparams(("parallel", "parallel")),
    )(qi, wi, ki, qa, ka, wuv)


def _nsa_prep_body(bq_ref, bkv_ref, tabh_ref, qb_ref, kc_ref, vc_ref, ks_ref, vs_ref, kw_ref, vw_ref):
    tm = bq_ref.shape[0]
    lane = lax.broadcasted_iota(I32, (tm, LANES), 1)
    ch, sh = tabh_ref[:, 0:LANES], tabh_ref[:, LANES:2 * LANES]

    def rope(x):
        rot = jnp.where(lane < ROPE_DIM // 2, pltpu.roll(x, LANES - ROPE_DIM // 2, 1),
                        pltpu.roll(x, ROPE_DIM // 2, 1))
        return x * ch + rot * sh

    scale = HEAD_DIM ** -0.5
    for h in range(B_HEADS):
        sl = slice(h * HEAD_DIM, (h + 1) * HEAD_DIM)
        qb_ref[:, sl] = (rope(bq_ref[:, sl]) * scale).astype(BF16)
    outs = (kc_ref, vc_ref, ks_ref, vs_ref, kw_ref, vw_ref)
    for kind in range(6):
        for g in range(B_GROUPS):
            c0 = (kind * B_GROUPS + g) * HEAD_DIM
            v = bkv_ref[:, c0:c0 + HEAD_DIM]
            if kind % 2 == 0:
                v = rope(v)
            outs[kind][g] = v.astype(BF16)


def _nsa_prep(z, tabh):
    nb, t, _ = z.shape
    tm = 256
    kv_spec = pl.BlockSpec((None, B_GROUPS, tm, HEAD_DIM), lambda b, i: (b, 0, i, 0))
    kv_shape = jax.ShapeDtypeStruct((nb, B_GROUPS, t, HEAD_DIM), BF16)
    return pl.pallas_call(
        _nsa_prep_body,
        grid=(nb, t // tm),
        in_specs=[pl.BlockSpec((None, tm, 1024), lambda b, i: (b, i, Z_BQ // 1024)),
                  pl.BlockSpec((None, tm, 1536), lambda b, i: (b, i, Z_BKV // 1536)),
                  pl.BlockSpec((None, tm, 256), lambda b, i: (b, i, 0))],
        out_specs=[pl.BlockSpec((None, tm, 1024), lambda b, i: (b, i, 0))] + [kv_spec] * 6,
        out_shape=[jax.ShapeDtypeStruct((nb, t, 1024), BF16)] + [kv_shape] * 6,
        compiler_params=_cparams(("parallel", "parallel")),
    )(z, z, tabh)


def _compress_body(ch_ref, pe_ref, w1_ref, w2_ref, o_ref):
    nc = ch_ref.shape[0]
    half = CMP_STRIDE * HEAD_DIM
    x = ch_ref[...]
    ha = _dot(x, w1_ref[0:half, :])
    hb = _dot(x, w1_ref[half:2 * half, :])
    bias = _dot(pe_ref[...], w1_ref[...])[0:1]
    pre = ha + pltpu.roll(hb, nc - 1, 0) + bias
    hid = jax.nn.gelu(pre, approximate=True)
    o_ref[...] = _dot(hid.astype(BF16), w2_ref[...]).astype(BF16)


def _compress(tok, pe, w1, w2):
    nb, ng, t, d = tok.shape
    nc = t // CMP_STRIDE
    chunks = tok.reshape(nb, ng, nc, CMP_STRIDE * d)
    pe8 = jnp.broadcast_to(pe.reshape(1, CMP_LEN * d), (8, CMP_LEN * d)).astype(BF16)
    return pl.pallas_call(
        _compress_body,
        grid=(nb, ng),
        in_specs=[pl.BlockSpec((None, None, nc, CMP_STRIDE * d), lambda b, g: (b, g, 0, 0)),
                  pl.BlockSpec(pe8.shape, lambda b, g: (0, 0)),
                  pl.BlockSpec(w1.shape, lambda b, g: (0, 0)),
                  pl.BlockSpec(w2.shape, lambda b, g: (0, 0))],
        out_specs=pl.BlockSpec((None, None, nc, d), lambda b, g: (b, g, 0, 0)),
        out_shape=jax.ShapeDtypeStruct((nb, ng, nc, d), BF16),
        compiler_params=_cparams(("parallel", "parallel")),
    )(chunks, pe8, w1.astype(BF16), w2.astype(BF16))


def _nsa_attn_body(q_ref, kc_ref, vc_ref, ks_ref, vs_ref, kw_ref, vw_ref, gate_ref, ov_ref, ex_ref,
                   o_ref, key_ref, thr_ref, acc_ref, m_ref, l_ref, *, n_cmp, n_sel):
    qb = NSA_QB
    rows = B_HPG * qb
    nc = kc_ref.shape[0]
    q0 = pl.program_id(2) * qb
    qs = jnp.concatenate([q_ref[:, j * HEAD_DIM:(j + 1) * HEAD_DIM] for j in range(B_HPG)], axis=0)

    tq_c = q0 + lax.broadcasted_iota(I32, (qb, nc), 0)
    n_id = lax.broadcasted_iota(I32, (qb, nc), 1)
    c_ok = jnp.where(n_id < n_cmp, n_id * CMP_STRIDE + (CMP_LEN - 1), tq_c + 1) <= tq_c
    c_bias = _row_tile(jnp.where(c_ok, 0.0, NEG_BIG), B_HPG)
    c_keep = _row_tile(jnp.where(c_ok, 1.0, 0.0), B_HPG)
    sc = _dot_nt(qs, kc_ref[...]) + c_bias
    e = jnp.exp(sc - jnp.max(sc, axis=1, keepdims=True)) * c_keep
    pc = e / jnp.maximum(jnp.sum(e, axis=1, keepdims=True), 1e-30)
    oc = _dot(pc.astype(BF16), vc_ref[...])

    pc4 = pc[0:qb]
    for j in range(1, B_HPG):
        pc4 = pc4 + pc[j * qb:(j + 1) * qb]
    p_hi, p_mid, p_lo = _split3(pc4)
    ov = ov_ref[...]
    imp = _dot(p_hi, ov) + _dot(p_mid, ov) + _dot(p_lo, ov)
    tq_b = q0 + lax.broadcasted_iota(I32, (qb, LANES), 0)
    blk = lax.broadcasted_iota(I32, (qb, LANES), 1)
    cur = tq_b // SLC_LEN
    d_cur = cur - blk
    forced = jnp.where(blk == 0, 1.0, jnp.where(d_cur == 0, 1.0, jnp.where(d_cur == 1, 1.0, 0.0)))
    admissible = blk * SLC_LEN <= tq_b
    free = jnp.where(admissible, jnp.where(forced > 0.5, -1.0, imp), -1.0)
    imp = jnp.where(forced > 0.5, jnp.inf, imp + 0.0)
    key_ref[0] = _f32_key(jnp.where(admissible, imp, -jnp.inf))
    _topk_select(key_ref, thr_ref, 1, n_sel, 7,
                 jnp.full((qb, 1), -1.0, F32), jnp.max(free, axis=1, keepdims=True))
    selm = jnp.where(key_ref[0] > thr_ref[:, 0:1], 1.0, 0.0).astype(BF16)

    nt = (q0 + qb + KEY_TILE - 1) // KEY_TILE
    tq_k = q0 + lax.broadcasted_iota(I32, (qb, KEY_TILE), 0)
    lane_k = lax.broadcasted_iota(I32, (qb, KEY_TILE), 1)
    m_ref[...] = jnp.full((rows, LANES), NEG_BIG, F32)
    l_ref[...] = jnp.zeros((rows, LANES), F32)
    acc_ref[...] = jnp.zeros((rows, HEAD_DIM), F32)

    def sel_body(kt, carry):
        k0 = pl.multiple_of(kt * KEY_TILE, KEY_TILE)
        chosen = _dot(selm, ex_ref[kt])
        ok = jnp.where(chosen > 0.5, k0 + lane_k, tq_k + 1) <= tq_k
        bias = jnp.where(ok, 0.0, NEG_BIG)
        ks = ks_ref[pl.ds(k0, KEY_TILE), :]
        vs = vs_ref[pl.ds(k0, KEY_TILE), :]
        s = _dot_nt(qs, ks) + _row_tile(bias, B_HPG)
        m_old = m_ref[...]
        m_new = jnp.maximum(m_old, jnp.max(s, axis=1, keepdims=True))
        p = jnp.exp(s - _lane_tile(m_new, KEY_TILE // LANES))
        alpha = jnp.exp(m_old - m_new)
        l_ref[...] = l_ref[...] * alpha + jnp.sum(p, axis=1, keepdims=True)
        acc_ref[...] = acc_ref[...] * alpha + _dot(p.astype(BF16), vs)
        m_ref[...] = m_new
        return carry

    lax.fori_loop(0, nt, sel_body, 0)
    os_ = acc_ref[...] / jnp.maximum(l_ref[...], 1e-30)

    span = WINDOW + qb
    start = pl.multiple_of(jnp.maximum(q0 - WINDOW, 0), qb)
    tq_w = q0 + lax.broadcasted_iota(I32, (qb, span), 0)
    wpos = start + lax.broadcasted_iota(I32, (qb, span), 1)
    w_ok = jnp.where(wpos > tq_w - WINDOW, wpos, tq_w + 1) <= tq_w
    w_bias = _row_tile(jnp.where(w_ok, 0.0, NEG_BIG), B_HPG)
    sw = _dot_nt(qs, kw_ref[pl.ds(start, span), :]) + w_bias
    ew = jnp.exp(sw - jnp.max(sw, axis=1, keepdims=True))
    pw = ew / jnp.maximum(jnp.sum(ew, axis=1, keepdims=True), 1e-30)
    ow = _dot(pw.astype(BF16), vw_ref[pl.ds(start, span), :])

    gate = gate_ref[...]
    for j in range(B_HPG):
        r = slice(j * qb, (j + 1) * qb)
        res = (gate[:, 3 * j:3 * j + 1] * oc[r] + gate[:, 3 * j + 1:3 * j + 2] * os_[r]
               + gate[:, 3 * j + 2:3 * j + 3] * ow[r])
        o_ref[:, j * HEAD_DIM:(j + 1) * HEAD_DIM] = res.astype(BF16)


def _nsa_attn(qb_all, kc, vc, ks, vs, kw, vw, gates, ov, ex):
    nb, t, _ = qb_all.shape
    nc = kc.shape[2]
    n_cmp = (t - CMP_LEN) // CMP_STRIDE + 1
    n_sel = min(SLC_TOPN, t // SLC_LEN)
    qb = NSA_QB
    body = functools.partial(_nsa_attn_body, n_cmp=n_cmp, n_sel=n_sel)
    res_c = pl.BlockSpec((None, None, nc, HEAD_DIM), lambda b, g, i: (b, g, 0, 0))
    res_t = pl.BlockSpec((None, None, t, HEAD_DIM), lambda b, g, i: (b, g, 0, 0))
    return pl.pallas_call(
        body,
        grid=(nb, B_GROUPS, t // qb),
        in_specs=[pl.BlockSpec((None, qb, B_HPG * HEAD_DIM), lambda b, g, i: (b, i, g)),
                  res_c, res_c, res_t, res_t, res_t, res_t,
                  pl.BlockSpec((None, None, qb, 3 * B_HPG), lambda b, g, i: (b, g, i, 0)),
                  pl.BlockSpec(ov.shape, lambda b, g, i: (0, 0)),
                  pl.BlockSpec(ex.shape, lambda b, g, i: (0, 0, 0))],
        out_specs=pl.BlockSpec((None, qb, B_HPG * HEAD_DIM), lambda b, g, i: (b, i, g)),
        out_shape=jax.ShapeDtypeStruct((nb, t, B_HEADS * HEAD_DIM), BF16),
        scratch_shapes=[pltpu.VMEM((1, qb, LANES), I32),
                        pltpu.VMEM((qb, LANES), I32),
                        pltpu.VMEM((B_HPG * qb, HEAD_DIM), F32),
                        pltpu.VMEM((B_HPG * qb, LANES), F32),
                        pltpu.VMEM((B_HPG * qb, LANES), F32)],
        compiler_params=_cparams(("parallel", "parallel", "parallel")),
    )(qb_all, kc, vc, ks, vs, kw, vw, gates, ov, ex)


def _merge_body(oa_ref, ob_ref, wa_ref, wb_ref, ga_ref, gb_ref, o_ref):
    a = _dot(oa_ref[...], wa_ref[...])
    b = _dot(ob_ref[...], wb_ref[...])
    o_ref[...] = (jax.nn.sigmoid(ga_ref[...]) * a + jax.nn.sigmoid(gb_ref[...]) * b).astype(BF16)


def _merge(o_a, o_b, w_a, w_b, z):
    nb, t, da = o_a.shape
    d = w_a.shape[1]
    tm, tn = 512, 1024
    return pl.pallas_call(
        _merge_body,
        grid=(nb, t // tm, d // tn),
        in_specs=[pl.BlockSpec((None, tm, da), lambda b, i, j: (b, i, 0)),
                  pl.BlockSpec((None, tm, da), lambda b, i, j: (b, i, 0)),
                  pl.BlockSpec((da, tn), lambda b, i, j: (0, j)),
                  pl.BlockSpec((da, tn), lambda b, i, j: (0, j)),
                  pl.BlockSpec((None, tm, tn), lambda b, i, j: (b, i, j)),
                  pl.BlockSpec((None, tm, tn), lambda b, i, j: (b, i, d // tn + j))],
        out_specs=pl.BlockSpec((None, tm, tn), lambda b, i, j: (b, i, j)),
        out_shape=jax.ShapeDtypeStruct((nb, t, d), BF16),
        compiler_params=_cparams(("parallel", "parallel", "parallel")),
    )(o_a, o_b, w_a, w_b, z, z)


def _outproj_body(mg_ref, wo_ref, x_ref, gpost_ref, gt_ref, gpre_ref, sc_ref, sh_ref,
                  wr_ref, br_ref, x1_ref, h_ref, idx_ref, gate_ref):
    tm = x_ref.shape[0]
    y = _dot(mg_ref[...], wo_ref[...])
    x1 = x_ref[...] + gt_ref[...] * _rms(y, gpost_ref[...])
    x1_ref[...] = x1
    h = _rms(x1, gpre_ref[...]) * (1.0 + sc_ref[...]) + sh_ref[...]
    h_ref[...] = h
    h_hi = h.astype(BF16)
    h_lo = (h - h_hi.astype(F32)).astype(BF16)
    logits = (_dot(h_hi, wr_ref[0]) + (_dot(h_hi, wr_ref[1]) + _dot(h_lo, wr_ref[0]))) + br_ref[...]
    lane = lax.broadcasted_iota(I32, (tm, N_EXPERTS), 1).astype(F32)
    vals, idxs = [], []
    cur = logits
    for _ in range(TOP_K):
        m = jnp.max(cur, axis=1, keepdims=True)
        am = jnp.min(jnp.where(cur == m, lane, float(N_EXPERTS)), axis=1, keepdims=True)
        vals.append(m)
        idxs.append(am)
        cur = jnp.where(lane == am, -jnp.inf, cur)
    es = [jnp.exp(v - vals[0]) for v in vals]
    tot = es[0] + es[1] + es[2] + es[3]
    for k in range(TOP_K):
        idx_ref[:, k:k + 1] = idxs[k].astype(I32)
        gate_ref[:, k:k + 1] = es[k] / tot


def _outproj(merged, w_out, x, g_post, gt1, g_pre, sc2, sh2, wr, br):
    nb, t, d = x.shape
    tm = 256
    vec = lambda: pl.BlockSpec((1, d), lambda b, i: (0, 0))
    mod = lambda: pl.BlockSpec((None, 1, d), lambda b, i: (b, 0, 0))
    row = lambda w: pl.BlockSpec((None, tm, w), lambda b, i: (b, i, 0))
    return pl.pallas_call(
        _outproj_body,
        grid=(nb, t // tm),
        in_specs=[row(d), pl.BlockSpec((d, d), lambda b, i: (0, 0)), row(d), vec(), mod(), vec(), mod(), mod(),
                  pl.BlockSpec(wr.shape, lambda b, i: (0, 0, 0)),
                  pl.BlockSpec((1, N_EXPERTS), lambda b, i: (0, 0))],
        out_specs=[row(d), row(d), row(TOP_K), row(TOP_K)],
        out_shape=[jax.ShapeDtypeStruct((nb, t, d), F32),
                   jax.ShapeDtypeStruct((nb, t, d), F32),
                   jax.ShapeDtypeStruct((nb, t, TOP_K), I32),
                   jax.ShapeDtypeStruct((nb, t, TOP_K), F32)],
        compiler_params=_cparams(("parallel", "parallel")),
    )(merged, w_out, x, g_post[None], gt1, g_pre[None], sc2, sh2, wr, br[None])


def _gmm_body(be_ref, nu_ref, tokc_ref, tokn_ref, h_ref, wg_ref, wu_ref, bg_ref, bu_ref, wd_ref, bd_ref,
              y_ref, xf_ref, xb_ref, sem, *, ff_steps):
    i, j = pl.program_id(0), pl.program_id(1)
    n_used = nu_ref[0]
    slot = i % 2
    last = ff_steps - 1
    rows_per_step = ROW_BLK // ff_steps

    def row_copy(tok_ref, r, s):
        return pltpu.make_async_copy(h_ref.at[pl.ds(tok_ref[0, 0, r], 1), :],
                                     xf_ref.at[s, pl.ds(r, 1), :], sem.at[s])

    def wait_slot(s):
        pltpu.make_async_copy(h_ref.at[pl.ds(0, ROW_BLK), :], xf_ref.at[s], sem.at[s]).wait()

    @pl.when(j == 0)
    def _():
        @pl.when(i == 0)
        def _():
            def body(r, carry):
                row_copy(tokc_ref, r, 0).start()
                return carry

            lax.fori_loop(0, ROW_BLK, body, 0)

        @pl.when(i <= n_used)
        def _():
            wait_slot(slot)

        @pl.when(i < n_used)
        def _():
            xb_ref[...] = xf_ref[slot].astype(BF16)

    @pl.when(i < n_used)
    def _():
        x = xb_ref[...]
        g = jnp.minimum(_dot(x, wg_ref[...].astype(BF16)) + bg_ref[...], SWIGLU_LIMIT)
        u = jnp.clip(_dot(x, wu_ref[...].astype(BF16)) + bu_ref[...], -SWIGLU_LIMIT, SWIGLU_LIMIT)
        act = g * jax.nn.sigmoid(SWIGLU_ALPHA * g) * (u + 1.0)
        part = _dot(act.astype(BF16), wd_ref[...].astype(BF16))
        for r in range(rows_per_step):
            row_copy(tokn_ref, j * rows_per_step + r, 1 - slot).start()

        @pl.when(j == 0)
        def _():
            y_ref[...] = part + bd_ref[...]

        @pl.when(j > 0)
        def _():
            y_ref[...] = y_ref[...] + part

    @pl.when(jnp.logical_and(i >= n_used, j == last))
    def _():
        y_ref[...] = jnp.zeros(y_ref.shape, F32)

    @pl.when(jnp.logical_and(jnp.logical_and(i == pl.num_programs(0) - 1, j == last), i < n_used))
    def _():
        wait_slot(1 - slot)


def _gmm(blk_e, n_used, row_tok, h, wgu, bgu, wd, bd):
    d = h.shape[1]
    ff = wd.shape[1]
    p = row_tok.shape[0]
    nblk = p // ROW_BLK
    nft = ff // FF_TILE
    tok3 = row_tok.reshape(nblk, 1, ROW_BLK)

    def expert(i, be, nu):
        return be[jnp.minimum(i, nu[0] - 1)]

    def tile(i, j, nu):
        return jnp.where(i < nu[0], j, nft - 1)

    grid_spec = pltpu.PrefetchScalarGridSpec(
        num_scalar_prefetch=2,
        grid=(nblk, nft),
        in_specs=[pl.BlockSpec((1, 1, ROW_BLK), lambda i, j, be, nu: (i, 0, 0), memory_space=pltpu.SMEM),
                  pl.BlockSpec((1, 1, ROW_BLK), lambda i, j, be, nu: (jnp.minimum(i + 1, nblk - 1), 0, 0),
                               memory_space=pltpu.SMEM),
                  pl.BlockSpec(memory_space=pl.ANY),
                  pl.BlockSpec((None, d, FF_TILE), lambda i, j, be, nu: (expert(i, be, nu), 0, tile(i, j, nu))),
                  pl.BlockSpec((None, d, FF_TILE),
                               lambda i, j, be, nu: (expert(i, be, nu), 0, nft + tile(i, j, nu))),
                  pl.BlockSpec((None, 1, FF_TILE), lambda i, j, be, nu: (expert(i, be, nu), 0, tile(i, j, nu))),
                  pl.BlockSpec((None, 1, FF_TILE),
                               lambda i, j, be, nu: (expert(i, be, nu), 0, nft + tile(i, j, nu))),
                  pl.BlockSpec((None, FF_TILE, d), lambda i, j, be, nu: (expert(i, be, nu), tile(i, j, nu), 0)),
                  pl.BlockSpec((None, 1, d), lambda i, j, be, nu: (expert(i, be, nu), 0, 0))],
        out_specs=pl.BlockSpec((ROW_BLK, d), lambda i, j, be, nu: (i, 0)),
        scratch_shapes=[pltpu.VMEM((2, ROW_BLK, d), F32),
                        pltpu.VMEM((ROW_BLK, d), BF16),
                        pltpu.SemaphoreType.DMA((2,))],
    )
    assert ROW_BLK % nft == 0
    return pl.pallas_call(
        functools.partial(_gmm_body, ff_steps=nft),
        grid_spec=grid_spec,
        out_shape=jax.ShapeDtypeStruct((p, d), F32),
        compiler_params=_cparams(("arbitrary", "arbitrary")),
    )(blk_e, n_used, tok3, tok3, h, wgu, wgu, bgu, bgu, wd, bd)


def _combine_body(destc_ref, destn_ref, ys_ref, gate_ref, x1_ref, g_ref, gt_ref, o_ref, buf, sem):
    tm = CMB_TOK
    i = pl.program_id(0)
    slot = i % 2

    def issue(dest_ref, s):
        def body(r, carry):
            pltpu.make_async_copy(ys_ref.at[pl.ds(dest_ref[0, 0, r], 1), :],
                                  buf.at[s, pl.ds(r, 1), :], sem.at[s]).start()
            return carry

        lax.fori_loop(0, TOP_K * tm, body, 0, unroll=16)

    @pl.when(i == 0)
    def _():
        issue(destc_ref, 0)

    pltpu.make_async_copy(ys_ref.at[pl.ds(0, TOP_K * tm), :], buf.at[slot], sem.at[slot]).wait()

    @pl.when(i + 1 < pl.num_programs(0))
    def _():
        issue(destn_ref, 1 - slot)

    gate = gate_ref[...]
    y = buf[slot, 0:tm] * gate[:, 0:1]
    for k in range(1, TOP_K):
        y = y + buf[slot, k * tm:(k + 1) * tm] * gate[:, k:k + 1]
    o_ref[...] = x1_ref[...] + gt_ref[...] * _rms(y, g_ref[...])


def _combine(dest, ys, gates, x1, g_post, gt2):
    nb, t, d = x1.shape
    n = nb * t
    tm = CMB_TOK
    steps = n // tm
    dest_t = dest.reshape(steps, tm, TOP_K).transpose(0, 2, 1).reshape(steps, 1, TOP_K * tm)
    out = pl.pallas_call(
        _combine_body,
        grid=(steps,),
        in_specs=[pl.BlockSpec((1, 1, TOP_K * tm), lambda i: (i, 0, 0), memory_space=pltpu.SMEM),
                  pl.BlockSpec((1, 1, TOP_K * tm), lambda i: (jnp.minimum(i + 1, steps - 1), 0, 0),
                               memory_space=pltpu.SMEM),
                  pl.BlockSpec(memory_space=pl.ANY),
                  pl.BlockSpec((tm, TOP_K), lambda i: (i, 0)),
                  pl.BlockSpec((tm, d), lambda i: (i, 0)),
                  pl.BlockSpec((1, d), lambda i: (0, 0)),
                  pl.BlockSpec((None, 1, d), lambda i: ((i * tm) // t, 0, 0))],
        out_specs=pl.BlockSpec((tm, d), lambda i: (i, 0)),
        out_shape=jax.ShapeDtypeStruct((n, d), F32),
        scratch_shapes=[pltpu.VMEM((2, TOP_K * tm, d), F32), pltpu.SemaphoreType.DMA((2,))],
        compiler_params=_cparams(("arbitrary",)),
    )(dest_t, dest_t, ys, gates, x1.reshape(n, d), g_post[None], gt2)
    return out.reshape(nb, t, d)


def _route(top_idx):
    n = top_idx.shape[0]
    nk = n * TOP_K
    e = top_idx.reshape(nk)
    onehot = (e[:, None] == jnp.arange(N_EXPERTS, dtype=I32)[None, :]).astype(I32)
    csum = jnp.cumsum(onehot, axis=0)
    rank = jnp.sum((csum - onehot) * onehot, axis=1)
    counts = csum[-1]
    padded = (counts + ROW_BLK - 1) // ROW_BLK * ROW_BLK
    pend = jnp.cumsum(padded)
    pstart = pend - padded
    dest = (pstart[e] + rank).astype(I32)
    nblk = nk // ROW_BLK + N_EXPERTS
    tok = jnp.arange(nk, dtype=I32) // TOP_K
    row_tok = jnp.zeros((nblk * ROW_BLK,), I32).at[dest].set(tok, unique_indices=True)
    blk_start = jnp.arange(nblk, dtype=I32) * ROW_BLK
    blk_e = jnp.minimum(jnp.sum((pend[None, :] <= blk_start[:, None]).astype(I32), axis=1), N_EXPERTS - 1)
    n_used = (pend[-1:] // ROW_BLK).astype(I32)
    return dest.reshape(n, TOP_K), row_tok, blk_e.astype(I32), n_used


def _rot_cols(w, half):
    z = jnp.zeros_like(w)
    z = z.at[..., 0:half].set(w[..., half:2 * half])
    return z.at[..., half:2 * half].set(w[..., 0:half])


def _prep_w_in(w_in):
    d = w_in.shape[0]
    offs = np.cumsum([0, A_Q_RANK, A_KV_RANK, ROPE_DIM, IDX_DIM, IDX_HEADS, B_HEADS * HEAD_DIM,
                      6 * B_GROUPS * HEAD_DIM, 3 * B_HEADS, 2 * d])
    cq, ckv, krope, kidx, widx, bq, bkv, bgate, mgate = [w_in[:, int(offs[i]):int(offs[i + 1])] for i in range(9)]
    krope_rot = jnp.concatenate([krope[:, ROPE_DIM // 2:], krope[:, :ROPE_DIM // 2]], axis=1)
    small = jnp.concatenate([krope, kidx, krope_rot, widx, bgate,
                             jnp.zeros((d, 256 - 160), w_in.dtype)], axis=1)
    w = jnp.concatenate([mgate, bq, cq, ckv, small, bkv], axis=1)
    assert w.shape[1] == Z_COLS
    return w.astype(BF16)


def _rope_tables(positions):
    def tab(rot_dim, lo, width):
        inv = jnp.power(ROPE_THETA, -jnp.arange(0, rot_dim, 2, dtype=F32) / rot_dim)
        ang = positions.astype(F32)[..., None] * inv
        c, s = jnp.cos(ang), jnp.sin(ang)
        shape = positions.shape
        cos_t = jnp.concatenate([jnp.zeros(shape + (lo,), F32), c, c,
                                 jnp.ones(shape + (width - rot_dim,), F32),
                                 jnp.zeros(shape + (LANES - lo - width,), F32)], axis=-1)
        sin_t = jnp.concatenate([jnp.zeros(shape + (lo,), F32), -s, s,
                                 jnp.zeros(shape + (LANES - lo - rot_dim,), F32)], axis=-1)
        return jnp.concatenate([cos_t, sin_t], axis=-1)

    return tab(ROPE_DIM, 0, LANES), tab(IDX_ROPE, 32, IDX_DIM)


def _select_tables(t):
    n_cmp = (t - CMP_LEN) // CMP_STRIDE + 1
    nc = t // CMP_STRIDE
    n = np.arange(nc)[:, None]
    m = np.arange(LANES)[None, :]
    ov = (n * CMP_STRIDE < m * SLC_LEN + SLC_LEN) & (n * CMP_STRIDE + CMP_LEN - 1 >= m * SLC_LEN) & (n < n_cmp)
    key_blk = (np.arange(t) // SLC_LEN).reshape(t // KEY_TILE, 1, KEY_TILE)
    ex = key_blk == np.arange(LANES).reshape(1, LANES, 1)
    return jnp.asarray(ov, BF16), jnp.asarray(ex, BF16)


def kernel(x, c, positions, w_ada, b_ada, g_pre_mix, g_post_mix, g_pre_ffn, g_post_ffn, w_in, a_cq_norm, a_ckv_norm, a_w_uq, a_w_uk, a_w_uv, idx_w_q, idx_k_norm_g, idx_k_norm_b, cmp_k_pe, cmp_k_w1, cmp_k_w2, cmp_v_pe, cmp_v_w1, cmp_v_w2, w_br_a, w_br_b, w_out, w_router, b_router, w_gate_up, b_gate_up, w_down, b_down):
    nb, t, d = x.shape
    n = nb * t
    assert t % KEY_TILE == 0 and t // SLC_LEN <= LANES and d % LANES == 0

    mod = _adaln(c, w_ada, b_ada)
    sh1, sc1, gt1, sh2, sc2, gt2 = [m[:, None, :] for m in jnp.split(mod, 6, axis=-1)]

    z = _inproj(x, g_pre_mix, sc1, sh1, _prep_w_in(w_in))
    tabh, tabi = _rope_tables(positions)

    wuq = jnp.transpose(a_w_uq, (1, 0, 2))
    wuk = jnp.transpose(a_w_uk, (1, 2, 0))
    wuk = jnp.concatenate([jnp.zeros((A_HEADS, ROPE_DIM, A_KV_RANK), wuk.dtype), wuk], axis=1)
    widx = jnp.transpose(idx_w_q, (1, 0, 2))
    pad_idx = lambda w: jnp.pad(w, ((0, 0), (0, 0), (32, LANES - 32 - IDX_DIM)))
    pad_vec = lambda v: jnp.pad(v, (32, LANES - 32 - IDX_DIM))[None]
    qa, ka, qi, ki, wi, sg = _dsa_prep(
        z, tabh, tabi, a_cq_norm[None], a_ckv_norm[None],
        wuq.astype(BF16), _rot_cols(wuq, ROPE_DIM // 2).astype(BF16), wuk.astype(BF16),
        pad_idx(widx).astype(BF16), pad_idx(_rot_cols(widx, IDX_ROPE // 2)).astype(BF16),
        pad_vec(idx_k_norm_g), pad_vec(idx_k_norm_b))
    o_a = _dsa_attn(qi, wi, ki, qa, ka, a_w_uv.astype(BF16))

    qb_all, kc_tok, vc_tok, ks, vs, kw, vw = _nsa_prep(z, tabh)
    kc = _compress(kc_tok, cmp_k_pe, cmp_k_w1, cmp_k_w2)
    vc = _compress(vc_tok, cmp_v_pe, cmp_v_w1, cmp_v_w2)
    gates_b = sg[:, :, IDX_HEADS:IDX_HEADS + 3 * B_HEADS].reshape(nb, t, B_GROUPS, 3 * B_HPG).transpose(0, 2, 1, 3)
    ov, ex = _select_tables(t)
    o_b = _nsa_attn(qb_all, kc, vc, ks, vs, kw, vw, gates_b, ov, ex)

    merged = _merge(o_a, o_b, w_br_a.astype(BF16), w_br_b.astype(BF16), z)

    wr_hi = w_router.astype(BF16)
    wr = jnp.stack([wr_hi, (w_router - wr_hi.astype(F32)).astype(BF16)])
    x1, h2, top_idx, gates = _outproj(merged, w_out.astype(BF16), x, g_post_mix, gt1, g_pre_ffn, sc2, sh2,
                                      wr, b_router)
    dest, row_tok, blk_e, n_used = _route(top_idx.reshape(n, TOP_K))
    ys = _gmm(blk_e, n_used, row_tok, h2.reshape(n, d), w_gate_up, b_gate_up[:, None, :],
              w_down, b_down[:, None, :])
    return _combine(dest, ys, gates.reshape(n, TOP_K), x1, g_post_ffn, gt2)
```

```python
import functools

import numpy as np
import jax
import jax.numpy as jnp
from jax import lax
from jax.experimental import pallas as pl
from jax.experimental.pallas import tpu as pltpu

F32 = jnp.float32
BF16 = jnp.bfloat16
I32 = jnp.int32

HEAD_DIM = 128
ROPE_DIM = 32
ROPE_THETA = 500000.0
NORM_EPS = 1e-6
Q_BLOCK = 128
A_HEADS = 8
A_Q_RANK = 512
A_KV_RANK = 256
IDX_HEADS = 8
IDX_DIM = 64
IDX_ROPE = 16
IDX_TOPK_MAX = 256
B_HEADS = 8
B_GROUPS = 2
B_HPG = 4
CMP_LEN = 32
CMP_STRIDE = 16
CMP_HIDDEN = 256
SLC_LEN = 64
SLC_TOPN = 16
WINDOW = 512
N_EXPERTS = 32
TOP_K = 4
SWIGLU_ALPHA = 1.702
SWIGLU_LIMIT = 7.0

LANES = 128
VMEM_LIMIT = 58 * 1024 * 1024

KEY_TILE = 512
QA_DIM = A_KV_RANK + LANES
ROW_BLK = 512
FF_TILE = 512
CMB_TOK = 256
ATTN_CHUNKS = 2
NSA_QB = 512
DSA_QB = 128

Z_MGATE, Z_BQ, Z_CQ, Z_CKV, Z_SMALL, Z_BKV, Z_COLS = 0, 4096, 5120, 5632, 5888, 6144, 7680

NEG_BIG = -1e30
INT_MIN = np.int32(-2147483648)
KEY_NEG_INF = np.int32(-2139095041)
BISECT_MAX_ITER = 24


def _cparams(sem):
    return pltpu.CompilerParams(dimension_semantics=sem, vmem_limit_bytes=VMEM_LIMIT)


def _dot(a, b):
    return jnp.dot(a, b, preferred_element_type=F32)


def _dot_nt(a, b):
    return lax.dot_general(a, b, (((1,), (1,)), ((), ())), preferred_element_type=F32)


def _split3(a):
    hi = a.astype(BF16)
    r1 = a - hi.astype(F32)
    mid = r1.astype(BF16)
    lo = (r1 - mid.astype(F32)).astype(BF16)
    return hi, mid, lo


def _lane_tile(a, n):
    return jnp.concatenate([a] * n, axis=1) if n > 1 else a


def _row_tile(a, n):
    return jnp.concatenate([a] * n, axis=0) if n > 1 else a


def _f32_key(s):
    bits = pltpu.bitcast(s, I32)
    return bits ^ (jnp.right_shift(bits, 31) & np.int32(0x7FFFFFFF))


def _count(key_ref, nt, pred):
    rows, width = key_ref.shape[1], key_ref.shape[2]

    def body(kt, acc):
        hit = pred(key_ref[kt], kt)
        for c in range(width // LANES):
            acc = jnp.where(hit[:, c * LANES:(c + 1) * LANES], acc + 1.0, acc)
        return acc

    acc = lax.fori_loop(0, nt, body, jnp.zeros((rows, LANES), F32))
    return jnp.sum(acc, axis=1, keepdims=True)


def _kth_key(key_ref, nt, k):
    rows = key_ref.shape[1]

    def bit_body(b, ans):
        cand = ans | jnp.left_shift(jnp.int32(1), 31 - b)
        cs = cand ^ INT_MIN
        cnt = _count(key_ref, nt, lambda kk, kt: kk >= cs)
        return jnp.where(cnt >= k, cand, ans)

    ans = lax.fori_loop(0, 32, bit_body, jnp.zeros((rows, 1), I32))
    return ans ^ INT_MIN


def _break_ties(key_ref, nt, t, k, idx_bits):
    rows, width = key_ref.shape[1], key_ref.shape[2]
    cnt_gt = _count(key_ref, nt, lambda kk, kt: kk > t)
    cnt_ge = _count(key_ref, nt, lambda kk, kt: kk >= t)
    need = k - cnt_gt

    @pl.when(jnp.max(cnt_ge) > k)
    def _():
        lane = lax.broadcasted_iota(I32, (rows, width), 1)

        def bit_body(b, c0):
            cand = c0 | jnp.left_shift(jnp.int32(1), idx_bits - 1 - b)
            f = _count(key_ref, nt, lambda kk, kt: jnp.where(kk == t, kt * width + lane, cand) < cand)
            return jnp.where(f < need, cand, c0)

        c0 = lax.fori_loop(0, idx_bits, bit_body, jnp.zeros((rows, 1), I32))

        def fix(kt, carry):
            kk = key_ref[kt]
            idx = kt * width + lane
            key_ref[kt] = jnp.where(jnp.where(kk == t, idx, c0) > c0, t - 1, kk)
            return carry

        lax.fori_loop(0, nt, fix, 0)


def _topk_select(key_ref, thr_ref, nt, k, idx_bits, lo0, hi0):
    rows = key_ref.shape[1]
    n_fin = _count(key_ref, nt, lambda kk, kt: kk > KEY_NEG_INF)
    done0 = jnp.where(n_fin <= k, 1.0, 0.0)
    thr0 = jnp.full((rows, 1), KEY_NEG_INF, I32)

    def cond(st):
        it, _, _, _, done, stuck = st
        return jnp.logical_and(it < BISECT_MAX_ITER, jnp.min(jnp.maximum(done, stuck)) < 0.5)

    def body(st):
        it, lo, hi, thr, done, stuck = st
        mid = 0.5 * lo + 0.5 * hi
        midk = _f32_key(mid)
        cnt = _count(key_ref, nt, lambda kk, kt: kk > midk)
        hit = jnp.where(cnt == k, 1.0 - done, 0.0)
        thr = jnp.where(hit > 0.5, midk, thr)
        done = jnp.maximum(done, hit)
        stuck = jnp.maximum(stuck, jnp.where(mid <= lo, 1.0, jnp.where(mid >= hi, 1.0, 0.0)))
        above = cnt >= k
        return it + 1, jnp.where(above, mid, lo), jnp.where(above, hi, mid), thr, done, stuck

    init = (jnp.int32(0), lo0, hi0, thr0, done0, jnp.zeros((rows, 1), F32))
    _, _, _, thr, done, _ = lax.while_loop(cond, lambda st: body(body(st)), init)
    thr_ref[...] = jnp.broadcast_to(thr, thr_ref.shape)

    @pl.when(jnp.min(done) < 0.5)
    def _():
        t = _kth_key(key_ref, nt, k)
        _break_ties(key_ref, nt, t, k, idx_bits)
        thr_ref[...] = jnp.broadcast_to(t - 1, thr_ref.shape)


def _ada_body(c_ref, w_ref, b_ref, o_ref):
    c = c_ref[...]
    s = c * jax.nn.sigmoid(c)
    s_hi, s_mid, s_lo = _split3(s)
    w_hi, w_mid, w_lo = _split3(w_ref[...])
    acc = _dot(s_hi, w_hi) + (_dot(s_hi, w_mid) + _dot(s_mid, w_hi))
    acc = acc + (_dot(s_hi, w_lo) + _dot(s_mid, w_mid) + _dot(s_lo, w_hi))
    o_ref[...] = acc + b_ref[...]


def _adaln(c, w_ada, b_ada):
    nb, d = c.shape
    n = w_ada.shape[1]
    tn = 1024
    cp = jnp.zeros((8, d), F32).at[:nb].set(c)
    out = pl.pallas_call(
        _ada_body,
        grid=(n // tn,),
        in_specs=[pl.BlockSpec((8, d), lambda j: (0, 0)),
                  pl.BlockSpec((d, tn), lambda j: (0, j)),
                  pl.BlockSpec((1, tn), lambda j: (0, j))],
        out_specs=pl.BlockSpec((8, tn), lambda j: (0, j)),
        out_shape=jax.ShapeDtypeStruct((8, n), F32),
        compiler_params=_cparams(("parallel",)),
    )(cp, w_ada, b_ada[None])
    return out[:nb]


def _inproj_body(x_ref, g_ref, sc_ref, sh_ref, w_ref, z_ref, h_ref):
    @pl.when(pl.program_id(2) == 0)
    def _():
        x = x_ref[...]
        y = x * lax.rsqrt(jnp.mean(x * x, axis=-1, keepdims=True) + NORM_EPS) * g_ref[...]
        h_ref[...] = (y * (1.0 + sc_ref[...]) + sh_ref[...]).astype(BF16)

    z_ref[...] = _dot(h_ref[...], w_ref[...])


def _inproj(x, g, sc, sh, w):
    nb, t, d = x.shape
    tm, tn = 512, 1536
    return pl.pallas_call(
        _inproj_body,
        grid=(nb, t // tm, Z_COLS // tn),
        in_specs=[pl.BlockSpec((None, tm, d), lambda b, i, j: (b, i, 0)),
                  pl.BlockSpec((1, d), lambda b, i, j: (0, 0)),
                  pl.BlockSpec((None, 1, d), lambda b, i, j: (b, 0, 0)),
                  pl.BlockSpec((None, 1, d), lambda b, i, j: (b, 0, 0)),
                  pl.BlockSpec((d, tn), lambda b, i, j: (0, j))],
        out_specs=pl.BlockSpec((None, tm, tn), lambda b, i, j: (b, i, j)),
        out_shape=jax.ShapeDtypeStruct((nb, t, Z_COLS), F32),
        scratch_shapes=[pltpu.VMEM((tm, d), BF16)],
        compiler_params=_cparams(("parallel", "parallel", "arbitrary")),
    )(x, g[None], sc, sh, w)


def _rms(x, g):
    return x * lax.rsqrt(jnp.mean(x * x, axis=-1, keepdims=True) + NORM_EPS) * g


def _dsa_prep_body(cq_ref, ckv_ref, sm_ref, tabh_ref, tabi_ref, gq_ref, gkv_ref,
                   wuq_ref, wuqr_ref, wuk_ref, widx_ref, widxr_ref, lng_ref, lnb_ref,
                   qa_ref, ka_ref, qi_ref, ki_ref, wi_ref, sg_ref):
    tm = cq_ref.shape[0]
    lane = lax.broadcasted_iota(I32, (tm, LANES), 1)
    cqn = _rms(cq_ref[...], gq_ref[...]).astype(BF16)
    ckvn = _rms(ckv_ref[...], gkv_ref[...])
    s0 = sm_ref[:, 0:LANES]
    s1 = sm_ref[:, LANES:2 * LANES]
    ch, sh = tabh_ref[:, 0:LANES], tabh_ref[:, LANES:2 * LANES]
    ci, si = tabi_ref[:, 0:LANES], tabi_ref[:, LANES:2 * LANES]

    k_rope = s0 * ch + pltpu.roll(s0, 32, 1) * sh
    ka_ref[:, 0:A_KV_RANK] = ckvn.astype(BF16)
    ka_ref[:, A_KV_RANK:QA_DIM] = jnp.where(lane < ROPE_DIM, k_rope, 0.0).astype(BF16)

    inside = jnp.where(lane >= 32, jnp.where(lane < 96, 1.0, 0.0), 0.0)
    mu = jnp.sum(s0 * inside, axis=-1, keepdims=True) * (1.0 / IDX_DIM)
    xc = (s0 - mu) * inside
    var = jnp.sum(xc * xc, axis=-1, keepdims=True) * (1.0 / IDX_DIM)
    y = xc * lax.rsqrt(var + NORM_EPS) * lng_ref[...] + lnb_ref[...]
    rot = jnp.where(lane < 40, pltpu.roll(y, LANES - 8, 1), pltpu.roll(y, 8, 1))
    ki_ref[...] = (y * ci + rot * si).astype(BF16)

    scale = HEAD_DIM ** -0.5
    w_scale = IDX_HEADS ** -0.5 * IDX_DIM ** -0.5
    q_all, qr_all = _dot(cqn, wuq_ref[...]), _dot(cqn, wuqr_ref[...])
    qi_all, qir_all = _dot(cqn, widx_ref[...]), _dot(cqn, widxr_ref[...])
    for h in range(A_HEADS):
        hs = slice(h * LANES, (h + 1) * LANES)
        roped = q_all[:, hs] * ch + qr_all[:, hs] * sh
        qabs = _dot(roped.astype(BF16), wuk_ref[h])
        qa_ref[h, :, 0:A_KV_RANK] = (qabs * scale).astype(BF16)
        qa_ref[h, :, A_KV_RANK:QA_DIM] = (jnp.where(lane < ROPE_DIM, roped, 0.0) * scale).astype(BF16)
        qi_ref[h] = (qi_all[:, hs] * ci + qir_all[:, hs] * si).astype(BF16)
        wi_ref[h] = jnp.broadcast_to(s1[:, h:h + 1], (tm, LANES)) * w_scale
    sg_ref[...] = jax.nn.sigmoid(s1)


def _dsa_prep(z, tabh, tabi, gq, gkv, wuq, wuqr, wuk, widx, widxr, lng, lnb):
    nb, t, _ = z.shape
    tm = 256
    full = lambda a: pl.BlockSpec(a.shape, lambda b, i: (0,) * a.ndim)
    return pl.pallas_call(
        _dsa_prep_body,
        grid=(nb, t // tm),
        in_specs=[pl.BlockSpec((None, tm, A_Q_RANK), lambda b, i: (b, i, Z_CQ // A_Q_RANK)),
                  pl.BlockSpec((None, tm, A_KV_RANK), lambda b, i: (b, i, Z_CKV // A_KV_RANK)),
                  pl.BlockSpec((None, tm, 256), lambda b, i: (b, i, Z_SMALL // 256)),
                  pl.BlockSpec((None, tm, 256), lambda b, i: (b, i, 0)),
                  pl.BlockSpec((None, tm, 256), lambda b, i: (b, i, 0)),
                  full(gq), full(gkv), full(wuq), full(wuqr), full(wuk), full(widx), full(widxr),
                  full(lng), full(lnb)],
        out_specs=[pl.BlockSpec((None, A_HEADS, tm, QA_DIM), lambda b, i: (b, 0, i, 0)),
                   pl.BlockSpec((None, tm, QA_DIM), lambda b, i: (b, i, 0)),
                   pl.BlockSpec((None, IDX_HEADS, tm, LANES), lambda b, i: (b, 0, i, 0)),
                   pl.BlockSpec((None, tm, LANES), lambda b, i: (b, i, 0)),
                   pl.BlockSpec((None, IDX_HEADS, tm, LANES), lambda b, i: (b, 0, i, 0)),
                   pl.BlockSpec((None, tm, LANES), lambda b, i: (b, i, 0))],
        out_shape=[jax.ShapeDtypeStruct((nb, A_HEADS, t, QA_DIM), BF16),
                   jax.ShapeDtypeStruct((nb, t, QA_DIM), BF16),
                   jax.ShapeDtypeStruct((nb, IDX_HEADS, t, LANES), BF16),
                   jax.ShapeDtypeStruct((nb, t, LANES), BF16),
                   jax.ShapeDtypeStruct((nb, IDX_HEADS, t, LANES), F32),
                   jax.ShapeDtypeStruct((nb, t, LANES), F32)],
        compiler_params=_cparams(("parallel", "parallel")),
    )(z, z, z, tabh, tabi, gq, gkv, wuq, wuqr, wuk, widx, widxr, lng, lnb)


def _fold_lanes(x, op):
    out = x[:, 0:LANES]
    for c in range(1, x.shape[1] // LANES):
        out = op(out, x[:, c * LANES:(c + 1) * LANES])
    return out


def _dsa_attn_body(qi_ref, wi_ref, ki_ref, qa_ref, ka_ref, wuv_ref, o_ref,
                   key_ref, thr_ref, acc_ref, m_ref, l_ref, *, topk, idx_bits):
    qb = DSA_QB
    rows = A_HEADS * qb
    q0 = pl.program_id(1) * qb
    nt = (q0 + qb + KEY_TILE - 1) // KEY_TILE
    tq = q0 + lax.broadcasted_iota(I32, (qb, KEY_TILE), 0)
    lane = lax.broadcasted_iota(I32, (qb, KEY_TILE), 1)

    def score_body(kt, carry):
        smin, smax = carry
        k0 = pl.multiple_of(kt * KEY_TILE, KEY_TILE)
        ki = ki_ref[pl.ds(k0, KEY_TILE), :]
        s = None
        for h in range(IDX_HEADS):
            r = jnp.maximum(_dot_nt(qi_ref[h], ki), 0.0) * _lane_tile(wi_ref[h], KEY_TILE // LANES)
            s = r if s is None else s + r
        s = s + 0.0
        causal = k0 + lane <= tq
        key_ref[kt] = _f32_key(jnp.where(causal, s, -jnp.inf))
        smin = jnp.minimum(smin, _fold_lanes(jnp.where(causal, s, jnp.inf), jnp.minimum))
        smax = jnp.maximum(smax, _fold_lanes(jnp.where(causal, s, -jnp.inf), jnp.maximum))
        return smin, smax

    smin, smax = lax.fori_loop(0, nt, score_body,
                               (jnp.full((qb, LANES), jnp.inf, F32), jnp.full((qb, LANES), -jnp.inf, F32)))
    _topk_select(key_ref, thr_ref, nt, topk, idx_bits,
                 jnp.min(smin, axis=1, keepdims=True), jnp.max(smax, axis=1, keepdims=True))
    thr = thr_ref[:, 0:1]

    m_ref[...] = jnp.full((rows, LANES), NEG_BIG, F32)
    l_ref[...] = jnp.zeros((rows, LANES), F32)
    acc_ref[...] = jnp.zeros((rows, A_KV_RANK), F32)

    def attn_body(kt, carry):
        k0 = pl.multiple_of(kt * KEY_TILE, KEY_TILE)
        ka = ka_ref[pl.ds(k0, KEY_TILE), :]
        val = ka[:, 0:A_KV_RANK]
        sel = jnp.where(key_ref[kt] > thr, k0 + lane, tq + 1) <= tq
        bias = jnp.where(sel, 0.0, NEG_BIG)
        s = _dot_nt(qa_ref[...].reshape(rows, QA_DIM), ka) + _row_tile(bias, A_HEADS)
        m_old = m_ref[...]
        m_new = jnp.maximum(m_old, jnp.max(s, axis=1, keepdims=True))
        p = jnp.exp(s - _lane_tile(m_new, KEY_TILE // LANES))
        alpha = jnp.exp(m_old - m_new)
        l_ref[...] = l_ref[...] * alpha + jnp.sum(p, axis=1, keepdims=True)
        pb = p.astype(BF16)
        half = rows // ATTN_CHUNKS
        for c in range(ATTN_CHUNKS):
            r = slice(c * half, (c + 1) * half)
            acc_ref[r] = acc_ref[r] * _lane_tile(alpha[r], A_KV_RANK // LANES) + _dot(pb[r], val)
        m_ref[...] = m_new
        return carry

    lax.fori_loop(0, nt, attn_body, 0)

    inv = 1.0 / jnp.maximum(l_ref[...], 1e-30)
    o_lat = (acc_ref[...] * _lane_tile(inv, A_KV_RANK // LANES)).astype(BF16)
    for h in range(A_HEADS):
        o_ref[:, h * HEAD_DIM:(h + 1) * HEAD_DIM] = _dot(o_lat[h * qb:(h + 1) * qb], wuv_ref[h]).astype(BF16)


def _dsa_attn(qi, wi, ki, qa, ka, wuv):
    nb, _, t, _ = qa.shape
    topk = min(IDX_TOPK_MAX, t // 4)
    idx_bits = int(np.ceil(np.log2(t)))
    qb = DSA_QB
    body = functools.partial(_dsa_attn_body, topk=topk, idx_bits=idx_bits)
    return pl.pallas_call(
        body,
        grid=(nb, t // qb),
        in_specs=[pl.BlockSpec((None, IDX_HEADS, qb, LANES), lambda b, i: (b, 0, i, 0)),
                  pl.BlockSpec((None, IDX_HEADS, qb, LANES), lambda b, i: (b, 0, i, 0)),
                  pl.BlockSpec((None, t, LANES), lambda b, i: (b, 0, 0)),
                  pl.BlockSpec((None, A_HEADS, qb, QA_DIM), lambda b, i: (b, 0, i, 0)),
                  pl.BlockSpec((None, t, QA_DIM), lambda b, i: (b, 0, 0)),
                  pl.BlockSpec(wuv.shape, lambda b, i: (0, 0, 0))],
        out_specs=pl.BlockSpec((None, qb, A_HEADS * HEAD_DIM), lambda b, i: (b, i, 0)),
        out_shape=jax.ShapeDtypeStruct((nb, t, A_HEADS * HEAD_DIM), BF16),
        scratch_shapes=[pltpu.VMEM((t // KEY_TILE, qb, KEY_TILE), I32),
                        pltpu.VMEM((qb, LANES), I32),
                        pltpu.VMEM((A_HEADS * qb, A_KV_RANK), F32),
                        pltpu.VMEM((A_HEADS * qb, LANES), F32),
                        pltpu.VMEM((A_HEADS * qb, LANES), F32)],
        compiler_params=_cparams(("parallel", "parallel")),
    )(qi, wi, ki, qa, ka, wuv)


def _nsa_prep_body(bq_ref, bkv_ref, tabh_ref, qb_ref, kc_ref, vc_ref, ks_ref, vs_ref, kw_ref, vw_ref):
    tm = bq_ref.shape[0]
    lane = lax.broadcasted_iota(I32, (tm, LANES), 1)
    ch, sh = tabh_ref[:, 0:LANES], tabh_ref[:, LANES:2 * LANES]

    def rope(x):
        rot = jnp.where(lane < ROPE_DIM // 2, pltpu.roll(x, LANES - ROPE_DIM // 2, 1),
                        pltpu.roll(x, ROPE_DIM // 2, 1))
        return x * ch + rot * sh

    scale = HEAD_DIM ** -0.5
    for h in range(B_HEADS):
        sl = slice(h * HEAD_DIM, (h + 1) * HEAD_DIM)
        qb_ref[:, sl] = (rope(bq_ref[:, sl]) * scale).astype(BF16)
    outs = (kc_ref, vc_ref, ks_ref, vs_ref, kw_ref, vw_ref)
    for kind in range(6):
        for g in range(B_GROUPS):
            c0 = (kind * B_GROUPS + g) * HEAD_DIM
            v = bkv_ref[:, c0:c0 + HEAD_DIM]
            if kind % 2 == 0:
                v = rope(v)
            outs[kind][g] = v.astype(BF16)


def _nsa_prep(z, tabh):
    nb, t, _ = z.shape
    tm = 256
    kv_spec = pl.BlockSpec((None, B_GROUPS, tm, HEAD_DIM), lambda b, i: (b, 0, i, 0))
    kv_shape = jax.ShapeDtypeStruct((nb, B_GROUPS, t, HEAD_DIM), BF16)
    return pl.pallas_call(
        _nsa_prep_body,
        grid=(nb, t // tm),
        in_specs=[pl.BlockSpec((None, tm, 1024), lambda b, i: (b, i, Z_BQ // 1024)),
                  pl.BlockSpec((None, tm, 1536), lambda b, i: (b, i, Z_BKV // 1536)),
                  pl.BlockSpec((None, tm, 256), lambda b, i: (b, i, 0))],
        out_specs=[pl.BlockSpec((None, tm, 1024), lambda b, i: (b, i, 0))] + [kv_spec] * 6,
        out_shape=[jax.ShapeDtypeStruct((nb, t, 1024), BF16)] + [kv_shape] * 6,
        compiler_params=_cparams(("parallel", "parallel")),
    )(z, z, tabh)


def _compress_body(ch_ref, pe_ref, w1_ref, w2_ref, o_ref):
    nc = ch_ref.shape[0]
    half = CMP_STRIDE * HEAD_DIM
    x = ch_ref[...]
    ha = _dot(x, w1_ref[0:half, :])
    hb = _dot(x, w1_ref[half:2 * half, :])
    bias = _dot(pe_ref[...], w1_ref[...])[0:1]
    pre = ha + pltpu.roll(hb, nc - 1, 0) + bias
    hid = jax.nn.gelu(pre, approximate=True)
    o_ref[...] = _dot(hid.astype(BF16), w2_ref[...]).astype(BF16)


def _compress(tok, pe, w1, w2):
    nb, ng, t, d = tok.shape
    nc = t // CMP_STRIDE
    chunks = tok.reshape(nb, ng, nc, CMP_STRIDE * d)
    pe8 = jnp.broadcast_to(pe.reshape(1, CMP_LEN * d), (8, CMP_LEN * d)).astype(BF16)
    return pl.pallas_call(
        _compress_body,
        grid=(nb, ng),
        in_specs=[pl.BlockSpec((None, None, nc, CMP_STRIDE * d), lambda b, g: (b, g, 0, 0)),
                  pl.BlockSpec(pe8.shape, lambda b, g: (0, 0)),
                  pl.BlockSpec(w1.shape, lambda b, g: (0, 0)),
                  pl.BlockSpec(w2.shape, lambda b, g: (0, 0))],
        out_specs=pl.BlockSpec((None, None, nc, d), lambda b, g: (b, g, 0, 0)),
        out_shape=jax.ShapeDtypeStruct((nb, ng, nc, d), BF16),
        compiler_params=_cparams(("parallel", "parallel")),
    )(chunks, pe8, w1.astype(BF16), w2.astype(BF16))


def _nsa_attn_body(q_ref, kc_ref, vc_ref, ks_ref, vs_ref, kw_ref, vw_ref, gate_ref, ov_ref, ex_ref,
                   o_ref, key_ref, thr_ref, acc_ref, m_ref, l_ref, *, n_cmp, n_sel):
    qb = NSA_QB
    rows = B_HPG * qb
    nc = kc_ref.shape[0]
    q0 = pl.program_id(2) * qb
    qs = jnp.concatenate([q_ref[:, j * HEAD_DIM:(j + 1) * HEAD_DIM] for j in range(B_HPG)], axis=0)

    tq_c = q0 + lax.broadcasted_iota(I32, (qb, nc), 0)
    n_id = lax.broadcasted_iota(I32, (qb, nc), 1)
    c_ok = jnp.where(n_id < n_cmp, n_id * CMP_STRIDE + (CMP_LEN - 1), tq_c + 1) <= tq_c
    c_bias = _row_tile(jnp.where(c_ok, 0.0, NEG_BIG), B_HPG)
    c_keep = _row_tile(jnp.where(c_ok, 1.0, 0.0), B_HPG)
    sc = _dot_nt(qs, kc_ref[...]) + c_bias
    e = jnp.exp(sc - jnp.max(sc, axis=1, keepdims=True)) * c_keep
    pc = e * (1.0 / jnp.maximum(jnp.sum(e, axis=1, keepdims=True), 1e-30))
    oc = _dot(pc.astype(BF16), vc_ref[...])

    pc4 = pc[0:qb]
    for j in range(1, B_HPG):
        pc4 = pc4 + pc[j * qb:(j + 1) * qb]
    p_hi, p_mid, p_lo = _split3(pc4)
    ov = ov_ref[...]
    imp = _dot(p_hi, ov) + _dot(p_mid, ov) + _dot(p_lo, ov)
    tq_b = q0 + lax.broadcasted_iota(I32, (qb, LANES), 0)
    blk = lax.broadcasted_iota(I32, (qb, LANES), 1)
    cur = tq_b // SLC_LEN
    d_cur = cur - blk
    forced = jnp.where(blk == 0, 1.0, jnp.where(d_cur == 0, 1.0, jnp.where(d_cur == 1, 1.0, 0.0)))
    admissible = blk * SLC_LEN <= tq_b
    free = jnp.where(admissible, jnp.where(forced > 0.5, -1.0, imp), -1.0)
    imp = jnp.where(forced > 0.5, jnp.inf, imp + 0.0)
    key_ref[0] = _f32_key(jnp.where(admissible, imp, -jnp.inf))
    _topk_select(key_ref, thr_ref, 1, n_sel, 7,
                 jnp.full((qb, 1), -1.0, F32), jnp.max(free, axis=1, keepdims=True))
    selm = jnp.where(key_ref[0] > thr_ref[:, 0:1], 1.0, 0.0).astype(BF16)

    nt = (q0 + qb + KEY_TILE - 1) // KEY_TILE
    tq_k = q0 + lax.broadcasted_iota(I32, (qb, KEY_TILE), 0)
    lane_k = lax.broadcasted_iota(I32, (qb, KEY_TILE), 1)
    m_ref[...] = jnp.full((rows, LANES), NEG_BIG, F32)
    l_ref[...] = jnp.zeros((rows, LANES), F32)
    acc_ref[...] = jnp.zeros((rows, HEAD_DIM), F32)

    def sel_body(kt, carry):
        k0 = pl.multiple_of(kt * KEY_TILE, KEY_TILE)
        chosen = _dot(selm, ex_ref[kt])
        ok = jnp.where(chosen > 0.5, k0 + lane_k, tq_k + 1) <= tq_k
        bias = jnp.where(ok, 0.0, NEG_BIG)
        ks = ks_ref[pl.ds(k0, KEY_TILE), :]
        vs = vs_ref[pl.ds(k0, KEY_TILE), :]
        s = _dot_nt(qs, ks) + _row_tile(bias, B_HPG)
        m_old = m_ref[...]
        m_new = jnp.maximum(m_old, jnp.max(s, axis=1, keepdims=True))
        p = jnp.exp(s - _lane_tile(m_new, KEY_TILE // LANES))
        alpha = jnp.exp(m_old - m_new)
        l_ref[...] = l_ref[...] * alpha + jnp.sum(p, axis=1, keepdims=True)
        acc_ref[...] = acc_ref[...] * alpha + _dot(p.astype(BF16), vs)
        m_ref[...] = m_new
        return carry

    lax.fori_loop(0, nt, sel_body, 0)
    os_ = acc_ref[...] / jnp.maximum(l_ref[...], 1e-30)

    span = WINDOW + qb
    start = pl.multiple_of(jnp.maximum(q0 - WINDOW, 0), qb)
    tq_w = q0 + lax.broadcasted_iota(I32, (qb, span), 0)
    wpos = start + lax.broadcasted_iota(I32, (qb, span), 1)
    w_ok = jnp.where(wpos > tq_w - WINDOW, wpos, tq_w + 1) <= tq_w
    w_bias = _row_tile(jnp.where(w_ok, 0.0, NEG_BIG), B_HPG)
    sw = _dot_nt(qs, kw_ref[pl.ds(start, span), :]) + w_bias
    ew = jnp.exp(sw - jnp.max(sw, axis=1, keepdims=True))
    inv_w = 1.0 / jnp.maximum(jnp.sum(ew, axis=1, keepdims=True), 1e-30)
    ow = _dot(ew.astype(BF16), vw_ref[pl.ds(start, span), :]) * inv_w

    gate = gate_ref[...]
    for j in range(B_HPG):
        r = slice(j * qb, (j + 1) * qb)
        res = (gate[:, 3 * j:3 * j + 1] * oc[r] + gate[:, 3 * j + 1:3 * j + 2] * os_[r]
               + gate[:, 3 * j + 2:3 * j + 3] * ow[r])
        o_ref[:, j * HEAD_DIM:(j + 1) * HEAD_DIM] = res.astype(BF16)


def _nsa_attn(qb_all, kc, vc, ks, vs, kw, vw, gates, ov, ex):
    nb, t, _ = qb_all.shape
    nc = kc.shape[2]
    n_cmp = (t - CMP_LEN) // CMP_STRIDE + 1
    n_sel = min(SLC_TOPN, t // SLC_LEN)
    qb = NSA_QB
    body = functools.partial(_nsa_attn_body, n_cmp=n_cmp, n_sel=n_sel)
    res_c = pl.BlockSpec((None, None, nc, HEAD_DIM), lambda b, g, i: (b, g, 0, 0))
    res_t = pl.BlockSpec((None, None, t, HEAD_DIM), lambda b, g, i: (b, g, 0, 0))
    return pl.pallas_call(
        body,
        grid=(nb, B_GROUPS, t // qb),
        in_specs=[pl.BlockSpec((None, qb, B_HPG * HEAD_DIM), lambda b, g, i: (b, i, g)),
                  res_c, res_c, res_t, res_t, res_t, res_t,
                  pl.BlockSpec((None, None, qb, 3 * B_HPG), lambda b, g, i: (b, g, i, 0)),
                  pl.BlockSpec(ov.shape, lambda b, g, i: (0, 0)),
                  pl.BlockSpec(ex.shape, lambda b, g, i: (0, 0, 0))],
        out_specs=pl.BlockSpec((None, qb, B_HPG * HEAD_DIM), lambda b, g, i: (b, i, g)),
        out_shape=jax.ShapeDtypeStruct((nb, t, B_HEADS * HEAD_DIM), BF16),
        scratch_shapes=[pltpu.VMEM((1, qb, LANES), I32),
                        pltpu.VMEM((qb, LANES), I32),
                        pltpu.VMEM((B_HPG * qb, HEAD_DIM), F32),
                        pltpu.VMEM((B_HPG * qb, LANES), F32),
                        pltpu.VMEM((B_HPG * qb, LANES), F32)],
        compiler_params=_cparams(("parallel", "parallel", "parallel")),
    )(qb_all, kc, vc, ks, vs, kw, vw, gates, ov, ex)


def _merge_body(oa_ref, ob_ref, wa_ref, wb_ref, ga_ref, gb_ref, o_ref):
    a = _dot(oa_ref[...], wa_ref[...])
    b = _dot(ob_ref[...], wb_ref[...])
    o_ref[...] = (jax.nn.sigmoid(ga_ref[...]) * a + jax.nn.sigmoid(gb_ref[...]) * b).astype(BF16)


def _merge(o_a, o_b, w_a, w_b, z):
    nb, t, da = o_a.shape
    d = w_a.shape[1]
    tm, tn = 1024, 1024
    return pl.pallas_call(
        _merge_body,
        grid=(nb, t // tm, d // tn),
        in_specs=[pl.BlockSpec((None, tm, da), lambda b, i, j: (b, i, 0)),
                  pl.BlockSpec((None, tm, da), lambda b, i, j: (b, i, 0)),
                  pl.BlockSpec((da, tn), lambda b, i, j: (0, j)),
                  pl.BlockSpec((da, tn), lambda b, i, j: (0, j)),
                  pl.BlockSpec((None, tm, tn), lambda b, i, j: (b, i, j)),
                  pl.BlockSpec((None, tm, tn), lambda b, i, j: (b, i, d // tn + j))],
        out_specs=pl.BlockSpec((None, tm, tn), lambda b, i, j: (b, i, j)),
        out_shape=jax.ShapeDtypeStruct((nb, t, d), BF16),
        compiler_params=_cparams(("parallel", "parallel", "parallel")),
    )(o_a, o_b, w_a, w_b, z, z)


def _outproj_body(mg_ref, wo_ref, x_ref, gpost_ref, gt_ref, gpre_ref, sc_ref, sh_ref,
                  wr_ref, br_ref, x1_ref, h_ref, idx_ref, gate_ref):
    tm = x_ref.shape[0]
    y = _dot(mg_ref[...], wo_ref[...])
    x1 = x_ref[...] + gt_ref[...] * _rms(y, gpost_ref[...])
    x1_ref[...] = x1
    h = _rms(x1, gpre_ref[...]) * (1.0 + sc_ref[...]) + sh_ref[...]
    h_ref[...] = h
    h_hi = h.astype(BF16)
    h_lo = (h - h_hi.astype(F32)).astype(BF16)
    logits = (_dot(h_hi, wr_ref[0]) + (_dot(h_hi, wr_ref[1]) + _dot(h_lo, wr_ref[0]))) + br_ref[...]
    lane = lax.broadcasted_iota(I32, (tm, N_EXPERTS), 1).astype(F32)
    vals, idxs = [], []
    cur = logits
    for _ in range(TOP_K):
        m = jnp.max(cur, axis=1, keepdims=True)
        am = jnp.min(jnp.where(cur == m, lane, float(N_EXPERTS)), axis=1, keepdims=True)
        vals.append(m)
        idxs.append(am)
        cur = jnp.where(lane == am, -jnp.inf, cur)
    es = [jnp.exp(v - vals[0]) for v in vals]
    tot = es[0] + es[1] + es[2] + es[3]
    for k in range(TOP_K):
        idx_ref[:, k:k + 1] = idxs[k].astype(I32)
        gate_ref[:, k:k + 1] = es[k] / tot


def _outproj(merged, w_out, x, g_post, gt1, g_pre, sc2, sh2, wr, br):
    nb, t, d = x.shape
    tm = 512
    vec = lambda: pl.BlockSpec((1, d), lambda b, i: (0, 0))
    mod = lambda: pl.BlockSpec((None, 1, d), lambda b, i: (b, 0, 0))
    row = lambda w: pl.BlockSpec((None, tm, w), lambda b, i: (b, i, 0))
    return pl.pallas_call(
        _outproj_body,
        grid=(nb, t // tm),
        in_specs=[row(d), pl.BlockSpec((d, d), lambda b, i: (0, 0)), row(d), vec(), mod(), vec(), mod(), mod(),
                  pl.BlockSpec(wr.shape, lambda b, i: (0, 0, 0)),
                  pl.BlockSpec((1, N_EXPERTS), lambda b, i: (0, 0))],
        out_specs=[row(d), row(d), row(TOP_K), row(TOP_K)],
        out_shape=[jax.ShapeDtypeStruct((nb, t, d), F32),
                   jax.ShapeDtypeStruct((nb, t, d), F32),
                   jax.ShapeDtypeStruct((nb, t, TOP_K), I32),
                   jax.ShapeDtypeStruct((nb, t, TOP_K), F32)],
        compiler_params=_cparams(("parallel", "parallel")),
    )(merged, w_out, x, g_post[None], gt1, g_pre[None], sc2, sh2, wr, br[None])


def _gmm_body(be_ref, nu_ref, tokc_ref, tokn_ref, h_ref, wg_ref, wu_ref, bg_ref, bu_ref, wd_ref, bd_ref,
              y_ref, xf_ref, xb_ref, sem, *, ff_steps):
    i, j = pl.program_id(0), pl.program_id(1)
    n_used = nu_ref[0]
    slot = i % 2
    last = ff_steps - 1
    rows_per_step = ROW_BLK // ff_steps

    def row_copy(tok_ref, r, s):
        return pltpu.make_async_copy(h_ref.at[pl.ds(tok_ref[0, 0, r], 1), :],
                                     xf_ref.at[s, pl.ds(r, 1), :], sem.at[s])

    def wait_slot(s):
        pltpu.make_async_copy(h_ref.at[pl.ds(0, ROW_BLK), :], xf_ref.at[s], sem.at[s]).wait()

    @pl.when(j == 0)
    def _():
        @pl.when(i == 0)
        def _():
            def body(r, carry):
                row_copy(tokc_ref, r, 0).start()
                return carry

            lax.fori_loop(0, ROW_BLK, body, 0)

        @pl.when(i <= n_used)
        def _():
            wait_slot(slot)

        @pl.when(i < n_used)
        def _():
            xb_ref[...] = xf_ref[slot].astype(BF16)

    @pl.when(i < n_used)
    def _():
        x = xb_ref[...]
        g = jnp.minimum(_dot(x, wg_ref[...].astype(BF16)) + bg_ref[...], SWIGLU_LIMIT)
        u = jnp.clip(_dot(x, wu_ref[...].astype(BF16)) + bu_ref[...], -SWIGLU_LIMIT, SWIGLU_LIMIT)
        act = g * jax.nn.sigmoid(SWIGLU_ALPHA * g) * (u + 1.0)
        part = _dot(act.astype(BF16), wd_ref[...].astype(BF16))
        for r in range(rows_per_step):
            row_copy(tokn_ref, j * rows_per_step + r, 1 - slot).start()

        @pl.when(j == 0)
        def _():
            y_ref[...] = part + bd_ref[...]

        @pl.when(j > 0)
        def _():
            y_ref[...] = y_ref[...] + part

    @pl.when(jnp.logical_and(i >= n_used, j == last))
    def _():
        y_ref[...] = jnp.zeros(y_ref.shape, F32)

    @pl.when(jnp.logical_and(jnp.logical_and(i == pl.num_programs(0) - 1, j == last), i < n_used))
    def _():
        wait_slot(1 - slot)


def _gmm(blk_e, n_used, row_tok, h, wgu, bgu, wd, bd):
    d = h.shape[1]
    ff = wd.shape[1]
    p = row_tok.shape[0]
    nblk = p // ROW_BLK
    nft = ff // FF_TILE
    tok3 = row_tok.reshape(nblk, 1, ROW_BLK)

    def expert(i, be, nu):
        return be[jnp.minimum(i, nu[0] - 1)]

    def tile(i, j, nu):
        return jnp.where(i < nu[0], j, nft - 1)

    grid_spec = pltpu.PrefetchScalarGridSpec(
        num_scalar_prefetch=2,
        grid=(nblk, nft),
        in_specs=[pl.BlockSpec((1, 1, ROW_BLK), lambda i, j, be, nu: (i, 0, 0), memory_space=pltpu.SMEM),
                  pl.BlockSpec((1, 1, ROW_BLK), lambda i, j, be, nu: (jnp.minimum(i + 1, nblk - 1), 0, 0),
                               memory_space=pltpu.SMEM),
                  pl.BlockSpec(memory_space=pl.ANY),
                  pl.BlockSpec((None, d, FF_TILE), lambda i, j, be, nu: (expert(i, be, nu), 0, tile(i, j, nu))),
                  pl.BlockSpec((None, d, FF_TILE),
                               lambda i, j, be, nu: (expert(i, be, nu), 0, nft + tile(i, j, nu))),
                  pl.BlockSpec((None, 1, FF_TILE), lambda i, j, be, nu: (expert(i, be, nu), 0, tile(i, j, nu))),
                  pl.BlockSpec((None, 1, FF_TILE),
                               lambda i, j, be, nu: (expert(i, be, nu), 0, nft + tile(i, j, nu))),
                  pl.BlockSpec((None, FF_TILE, d), lambda i, j, be, nu: (expert(i, be, nu), tile(i, j, nu), 0)),
                  pl.BlockSpec((None, 1, d), lambda i, j, be, nu: (expert(i, be, nu), 0, 0))],
        out_specs=pl.BlockSpec((ROW_BLK, d), lambda i, j, be, nu: (i, 0)),
        scratch_shapes=[pltpu.VMEM((2, ROW_BLK, d), F32),
                        pltpu.VMEM((ROW_BLK, d), BF16),
                        pltpu.SemaphoreType.DMA((2,))],
    )
    assert ROW_BLK % nft == 0
    return pl.pallas_call(
        functools.partial(_gmm_body, ff_steps=nft),
        grid_spec=grid_spec,
        out_shape=jax.ShapeDtypeStruct((p, d), F32),
        compiler_params=_cparams(("arbitrary", "arbitrary")),
    )(blk_e, n_used, tok3, tok3, h, wgu, wgu, bgu, bgu, wd, bd)


def _combine_body(destc_ref, destn_ref, ys_ref, gate_ref, x1_ref, g_ref, gt_ref, o_ref, buf, sem):
    tm = CMB_TOK
    i = pl.program_id(0)
    slot = i % 2

    def issue(dest_ref, s):
        def body(r, carry):
            pltpu.make_async_copy(ys_ref.at[pl.ds(dest_ref[0, 0, r], 1), :],
                                  buf.at[s, pl.ds(r, 1), :], sem.at[s]).start()
            return carry

        lax.fori_loop(0, TOP_K * tm, body, 0, unroll=16)

    @pl.when(i == 0)
    def _():
        issue(destc_ref, 0)

    pltpu.make_async_copy(ys_ref.at[pl.ds(0, TOP_K * tm), :], buf.at[slot], sem.at[slot]).wait()

    @pl.when(i + 1 < pl.num_programs(0))
    def _():
        issue(destn_ref, 1 - slot)

    gate = gate_ref[...]
    y = buf[slot, 0:tm] * gate[:, 0:1]
    for k in range(1, TOP_K):
        y = y + buf[slot, k * tm:(k + 1) * tm] * gate[:, k:k + 1]
    o_ref[...] = x1_ref[...] + gt_ref[...] * _rms(y, g_ref[...])


def _combine(dest, ys, gates, x1, g_post, gt2):
    nb, t, d = x1.shape
    n = nb * t
    tm = CMB_TOK
    steps = n // tm
    dest_t = dest.reshape(steps, tm, TOP_K).transpose(0, 2, 1).reshape(steps, 1, TOP_K * tm)
    out = pl.pallas_call(
        _combine_body,
        grid=(steps,),
        in_specs=[pl.BlockSpec((1, 1, TOP_K * tm), lambda i: (i, 0, 0), memory_space=pltpu.SMEM),
                  pl.BlockSpec((1, 1, TOP_K * tm), lambda i: (jnp.minimum(i + 1, steps - 1), 0, 0),
                               memory_space=pltpu.SMEM),
                  pl.BlockSpec(memory_space=pl.ANY),
                  pl.BlockSpec((tm, TOP_K), lambda i: (i, 0)),
                  pl.BlockSpec((tm, d), lambda i: (i, 0)),
                  pl.BlockSpec((1, d), lambda i: (0, 0)),
                  pl.BlockSpec((None, 1, d), lambda i: ((i * tm) // t, 0, 0))],
        out_specs=pl.BlockSpec((tm, d), lambda i: (i, 0)),
        out_shape=jax.ShapeDtypeStruct((n, d), F32),
        scratch_shapes=[pltpu.VMEM((2, TOP_K * tm, d), F32), pltpu.SemaphoreType.DMA((2,))],
        compiler_params=_cparams(("arbitrary",)),
    )(dest_t, dest_t, ys, gates, x1.reshape(n, d), g_post[None], gt2)
    return out.reshape(nb, t, d)


def _route(top_idx):
    n = top_idx.shape[0]
    nk = n * TOP_K
    e = top_idx.reshape(nk)
    onehot = (e[:, None] == jnp.arange(N_EXPERTS, dtype=I32)[None, :]).astype(I32)
    csum = jnp.cumsum(onehot, axis=0)
    rank = jnp.sum((csum - onehot) * onehot, axis=1)
    counts = csum[-1]
    padded = (counts + ROW_BLK - 1) // ROW_BLK * ROW_BLK
    pend = jnp.cumsum(padded)
    pstart = pend - padded
    dest = (pstart[e] + rank).astype(I32)
    nblk = nk // ROW_BLK + N_EXPERTS
    tok = jnp.arange(nk, dtype=I32) // TOP_K
    row_tok = jnp.zeros((nblk * ROW_BLK,), I32).at[dest].set(tok, unique_indices=True)
    blk_start = jnp.arange(nblk, dtype=I32) * ROW_BLK
    blk_e = jnp.minimum(jnp.sum((pend[None, :] <= blk_start[:, None]).astype(I32), axis=1), N_EXPERTS - 1)
    n_used = (pend[-1:] // ROW_BLK).astype(I32)
    return dest.reshape(n, TOP_K), row_tok, blk_e.astype(I32), n_used


def _rot_cols(w, half):
    rest = jnp.zeros(w.shape[:-1] + (w.shape[-1] - 2 * half,), w.dtype)
    return jnp.concatenate([w[..., half:2 * half], w[..., 0:half], rest], axis=-1)


def _prep_w_in(w_in):
    d = w_in.shape[0]
    offs = np.cumsum([0, A_Q_RANK, A_KV_RANK, ROPE_DIM, IDX_DIM, IDX_HEADS, B_HEADS * HEAD_DIM,
                      6 * B_GROUPS * HEAD_DIM, 3 * B_HEADS, 2 * d])
    cq, ckv, krope, kidx, widx, bq, bkv, bgate, mgate = [w_in[:, int(offs[i]):int(offs[i + 1])] for i in range(9)]
    krope_rot = jnp.concatenate([krope[:, ROPE_DIM // 2:], krope[:, :ROPE_DIM // 2]], axis=1)
    small = jnp.concatenate([krope, kidx, krope_rot, widx, bgate,
                             jnp.zeros((d, 256 - 160), w_in.dtype)], axis=1)
    w = jnp.concatenate([mgate, bq, cq, ckv, small, bkv], axis=1)
    assert w.shape[1] == Z_COLS
    return w.astype(BF16)


def _rope_tables(positions):
    def tab(rot_dim, lo, width):
        inv = jnp.power(ROPE_THETA, -jnp.arange(0, rot_dim, 2, dtype=F32) / rot_dim)
        ang = positions.astype(F32)[..., None] * inv
        c, s = jnp.cos(ang), jnp.sin(ang)
        shape = positions.shape
        cos_t = jnp.concatenate([jnp.zeros(shape + (lo,), F32), c, c,
                                 jnp.ones(shape + (width - rot_dim,), F32),
                                 jnp.zeros(shape + (LANES - lo - width,), F32)], axis=-1)
        sin_t = jnp.concatenate([jnp.zeros(shape + (lo,), F32), -s, s,
                                 jnp.zeros(shape + (LANES - lo - rot_dim,), F32)], axis=-1)
        return jnp.concatenate([cos_t, sin_t], axis=-1)

    return tab(ROPE_DIM, 0, LANES), tab(IDX_ROPE, 32, IDX_DIM)


def _select_tables(t):
    n_cmp = (t - CMP_LEN) // CMP_STRIDE + 1
    nc = t // CMP_STRIDE
    n = np.arange(nc)[:, None]
    m = np.arange(LANES)[None, :]
    ov = (n * CMP_STRIDE < m * SLC_LEN + SLC_LEN) & (n * CMP_STRIDE + CMP_LEN - 1 >= m * SLC_LEN) & (n < n_cmp)
    key_blk = (np.arange(t) // SLC_LEN).reshape(t // KEY_TILE, 1, KEY_TILE)
    ex = key_blk == np.arange(LANES).reshape(1, LANES, 1)
    return jnp.asarray(ov, BF16), jnp.asarray(ex, BF16)


def kernel(x, c, positions, w_ada, b_ada, g_pre_mix, g_post_mix, g_pre_ffn, g_post_ffn, w_in, a_cq_norm, a_ckv_norm, a_w_uq, a_w_uk, a_w_uv, idx_w_q, idx_k_norm_g, idx_k_norm_b, cmp_k_pe, cmp_k_w1, cmp_k_w2, cmp_v_pe, cmp_v_w1, cmp_v_w2, w_br_a, w_br_b, w_out, w_router, b_router, w_gate_up, b_gate_up, w_down, b_down):
    nb, t, d = x.shape
    n = nb * t
    assert t % KEY_TILE == 0 and t // SLC_LEN <= LANES and d % LANES == 0

    mod = _adaln(c, w_ada, b_ada)
    sh1, sc1, gt1, sh2, sc2, gt2 = [m[:, None, :] for m in jnp.split(mod, 6, axis=-1)]

    z = _inproj(x, g_pre_mix, sc1, sh1, _prep_w_in(w_in))
    tabh, tabi = _rope_tables(positions)

    wuk = jnp.transpose(a_w_uk, (1, 2, 0))
    wuk = jnp.concatenate([jnp.zeros((A_HEADS, ROPE_DIM, A_KV_RANK), wuk.dtype), wuk], axis=1)
    pad_idx = lambda w: jnp.pad(w, ((0, 0), (0, 0), (32, LANES - 32 - IDX_DIM)))
    pad_vec = lambda v: jnp.pad(v, (32, LANES - 32 - IDX_DIM))[None]
    heads_wide = lambda w: w.reshape(A_Q_RANK, -1).astype(BF16)
    qa, ka, qi, ki, wi, sg = _dsa_prep(
        z, tabh, tabi, a_cq_norm[None], a_ckv_norm[None],
        heads_wide(a_w_uq), heads_wide(_rot_cols(a_w_uq, ROPE_DIM // 2)), wuk.astype(BF16),
        heads_wide(pad_idx(idx_w_q)), heads_wide(pad_idx(_rot_cols(idx_w_q, IDX_ROPE // 2))),
        pad_vec(idx_k_norm_g), pad_vec(idx_k_norm_b))
    o_a = _dsa_attn(qi, wi, ki, qa, ka, a_w_uv.astype(BF16))

    qb_all, kc_tok, vc_tok, ks, vs, kw, vw = _nsa_prep(z, tabh)
    kc = _compress(kc_tok, cmp_k_pe, cmp_k_w1, cmp_k_w2)
    vc = _compress(vc_tok, cmp_v_pe, cmp_v_w1, cmp_v_w2)
    gates_b = sg[:, :, IDX_HEADS:IDX_HEADS + 3 * B_HEADS].reshape(nb, t, B_GROUPS, 3 * B_HPG).transpose(0, 2, 1, 3)
    ov, ex = _select_tables(t)
    o_b = _nsa_attn(qb_all, kc, vc, ks, vs, kw, vw, gates_b, ov, ex)

    merged = _merge(o_a, o_b, w_br_a.astype(BF16), w_br_b.astype(BF16), z)

    wr_hi = w_router.astype(BF16)
    wr = jnp.stack([wr_hi, (w_router - wr_hi.astype(F32)).astype(BF16)])
    x1, h2, top_idx, gates = _outproj(merged, w_out.astype(BF16), x, g_post_mix, gt1, g_pre_ffn, sc2, sh2,
                                      wr, b_router)
    dest, row_tok, blk_e, n_used = _route(top_idx.reshape(n, TOP_K))
    ys = _gmm(blk_e, n_used, row_tok, h2.reshape(n, d), w_gate_up, b_gate_up[:, None, :],
              w_down, b_down[:, None, :])
    return _combine(dest, ys, gates.reshape(n, TOP_K), x1, g_post_ffn, gt2)
```

```python
import functools

import numpy as np
import jax
import jax.numpy as jnp
from jax import lax
from jax.experimental import pallas as pl
from jax.experimental.pallas import tpu as pltpu

F32 = jnp.float32
BF16 = jnp.bfloat16
I32 = jnp.int32

HEAD_DIM = 128
ROPE_DIM = 32
ROPE_THETA = 500000.0
NORM_EPS = 1e-6
Q_BLOCK = 128
A_HEADS = 8
A_Q_RANK = 512
A_KV_RANK = 256
IDX_HEADS = 8
IDX_DIM = 64
IDX_ROPE = 16
IDX_TOPK_MAX = 256
B_HEADS = 8
B_GROUPS = 2
B_HPG = 4
CMP_LEN = 32
CMP_STRIDE = 16
CMP_HIDDEN = 256
SLC_LEN = 64
SLC_TOPN = 16
WINDOW = 512
N_EXPERTS = 32
TOP_K = 4
SWIGLU_ALPHA = 1.702
SWIGLU_LIMIT = 7.0

LANES = 128
VMEM_LIMIT = 58 * 1024 * 1024

KEY_TILE = 512
QA_DIM = A_KV_RANK + LANES
ROW_BLK = 512
FF_TILE = 512
CMB_TOK = 256
ATTN_CHUNKS = 2
NSA_QB = 512
DSA_QB = 128

Z_MGATE, Z_BQ, Z_CQ, Z_CKV, Z_SMALL, Z_BKV, Z_COLS = 0, 4096, 5120, 5632, 5888, 6144, 7680

NEG_BIG = -1e30
INT_MIN = np.int32(-2147483648)
KEY_NEG_INF = np.int32(-2139095041)
BISECT_MAX_ITER = 24


def _cparams(sem):
    return pltpu.CompilerParams(dimension_semantics=sem, vmem_limit_bytes=VMEM_LIMIT)


def _dot(a, b):
    return jnp.dot(a, b, preferred_element_type=F32)


def _dot_nt(a, b):
    return lax.dot_general(a, b, (((1,), (1,)), ((), ())), preferred_element_type=F32)


def _split3(a):
    hi = a.astype(BF16)
    r1 = a - hi.astype(F32)
    mid = r1.astype(BF16)
    lo = (r1 - mid.astype(F32)).astype(BF16)
    return hi, mid, lo


def _lane_tile(a, n):
    return jnp.concatenate([a] * n, axis=1) if n > 1 else a


def _row_tile(a, n):
    return jnp.concatenate([a] * n, axis=0) if n > 1 else a


def _f32_key(s):
    bits = pltpu.bitcast(s, I32)
    return bits ^ (jnp.right_shift(bits, 31) & np.int32(0x7FFFFFFF))


def _count(key_ref, nt, pred):
    rows, width = key_ref.shape[1], key_ref.shape[2]

    def body(kt, acc):
        hit = pred(key_ref[kt], kt)
        for c in range(width // LANES):
            acc = jnp.where(hit[:, c * LANES:(c + 1) * LANES], acc + 1.0, acc)
        return acc

    acc = lax.fori_loop(0, nt, body, jnp.zeros((rows, LANES), F32))
    return jnp.sum(acc, axis=1, keepdims=True)


def _kth_key(key_ref, nt, k):
    rows = key_ref.shape[1]

    def bit_body(b, ans):
        cand = ans | jnp.left_shift(jnp.int32(1), 31 - b)
        cs = cand ^ INT_MIN
        cnt = _count(key_ref, nt, lambda kk, kt: kk >= cs)
        return jnp.where(cnt >= k, cand, ans)

    ans = lax.fori_loop(0, 32, bit_body, jnp.zeros((rows, 1), I32))
    return ans ^ INT_MIN


def _break_ties(key_ref, nt, t, k, idx_bits):
    rows, width = key_ref.shape[1], key_ref.shape[2]
    cnt_gt = _count(key_ref, nt, lambda kk, kt: kk > t)
    cnt_ge = _count(key_ref, nt, lambda kk, kt: kk >= t)
    need = k - cnt_gt

    @pl.when(jnp.max(cnt_ge) > k)
    def _():
        lane = lax.broadcasted_iota(I32, (rows, width), 1)

        def bit_body(b, c0):
            cand = c0 | jnp.left_shift(jnp.int32(1), idx_bits - 1 - b)
            f = _count(key_ref, nt, lambda kk, kt: jnp.where(kk == t, kt * width + lane, cand) < cand)
            return jnp.where(f < need, cand, c0)

        c0 = lax.fori_loop(0, idx_bits, bit_body, jnp.zeros((rows, 1), I32))

        def fix(kt, carry):
            kk = key_ref[kt]
            idx = kt * width + lane
            key_ref[kt] = jnp.where(jnp.where(kk == t, idx, c0) > c0, t - 1, kk)
            return carry

        lax.fori_loop(0, nt, fix, 0)


def _topk_select(key_ref, thr_ref, nt, k, idx_bits, lo0, hi0):
    rows = key_ref.shape[1]
    n_fin = _count(key_ref, nt, lambda kk, kt: kk > KEY_NEG_INF)
    done0 = jnp.where(n_fin <= k, 1.0, 0.0)
    thr0 = jnp.full((rows, 1), KEY_NEG_INF, I32)

    def cond(st):
        it, _, _, _, done, stuck = st
        return jnp.logical_and(it < BISECT_MAX_ITER, jnp.min(jnp.maximum(done, stuck)) < 0.5)

    def body(st):
        it, lo, hi, thr, done, stuck = st
        mid = 0.5 * lo + 0.5 * hi
        midk = _f32_key(mid)
        cnt = _count(key_ref, nt, lambda kk, kt: kk > midk)
        hit = jnp.where(cnt == k, 1.0 - done, 0.0)
        thr = jnp.where(hit > 0.5, midk, thr)
        done = jnp.maximum(done, hit)
        stuck = jnp.maximum(stuck, jnp.where(mid <= lo, 1.0, jnp.where(mid >= hi, 1.0, 0.0)))
        above = cnt >= k
        return it + 1, jnp.where(above, mid, lo), jnp.where(above, hi, mid), thr, done, stuck

    init = (jnp.int32(0), lo0, hi0, thr0, done0, jnp.zeros((rows, 1), F32))
    _, _, _, thr, done, _ = lax.while_loop(cond, lambda st: body(body(st)), init)
    thr_ref[...] = jnp.broadcast_to(thr, thr_ref.shape)

    @pl.when(jnp.min(done) < 0.5)
    def _():
        t = _kth_key(key_ref, nt, k)
        _break_ties(key_ref, nt, t, k, idx_bits)
        thr_ref[...] = jnp.broadcast_to(t - 1, thr_ref.shape)


def _ada_body(c_ref, w_ref, b_ref, o_ref):
    c = c_ref[...]
    s = c * jax.nn.sigmoid(c)
    s_hi, s_mid, s_lo = _split3(s)
    w_hi, w_mid, w_lo = _split3(w_ref[...])
    acc = _dot(s_hi, w_hi) + (_dot(s_hi, w_mid) + _dot(s_mid, w_hi))
    acc = acc + (_dot(s_hi, w_lo) + _dot(s_mid, w_mid) + _dot(s_lo, w_hi))
    o_ref[...] = acc + b_ref[...]


def _adaln(c, w_ada, b_ada):
    nb, d = c.shape
    n = w_ada.shape[1]
    tn = 1024
    cp = jnp.zeros((8, d), F32).at[:nb].set(c)
    out = pl.pallas_call(
        _ada_body,
        grid=(n // tn,),
        in_specs=[pl.BlockSpec((8, d), lambda j: (0, 0)),
                  pl.BlockSpec((d, tn), lambda j: (0, j)),
                  pl.BlockSpec((1, tn), lambda j: (0, j))],
        out_specs=pl.BlockSpec((8, tn), lambda j: (0, j)),
        out_shape=jax.ShapeDtypeStruct((8, n), F32),
        compiler_params=_cparams(("parallel",)),
    )(cp, w_ada, b_ada[None])
    return out[:nb]


def _inproj_body(x_ref, g_ref, sc_ref, sh_ref, w_ref, z_ref, h_ref):
    @pl.when(pl.program_id(2) == 0)
    def _():
        x = x_ref[...]
        y = x * lax.rsqrt(jnp.mean(x * x, axis=-1, keepdims=True) + NORM_EPS) * g_ref[...]
        h_ref[...] = (y * (1.0 + sc_ref[...]) + sh_ref[...]).astype(BF16)

    z_ref[...] = _dot(h_ref[...], w_ref[...])


def _inproj(x, g, sc, sh, w):
    nb, t, d = x.shape
    tm, tn = 512, 1536
    return pl.pallas_call(
        _inproj_body,
        grid=(nb, t // tm, Z_COLS // tn),
        in_specs=[pl.BlockSpec((None, tm, d), lambda b, i, j: (b, i, 0)),
                  pl.BlockSpec((1, d), lambda b, i, j: (0, 0)),
                  pl.BlockSpec((None, 1, d), lambda b, i, j: (b, 0, 0)),
                  pl.BlockSpec((None, 1, d), lambda b, i, j: (b, 0, 0)),
                  pl.BlockSpec((d, tn), lambda b, i, j: (0, j))],
        out_specs=pl.BlockSpec((None, tm, tn), lambda b, i, j: (b, i, j)),
        out_shape=jax.ShapeDtypeStruct((nb, t, Z_COLS), F32),
        scratch_shapes=[pltpu.VMEM((tm, d), BF16)],
        compiler_params=_cparams(("parallel", "parallel", "arbitrary")),
    )(x, g[None], sc, sh, w)


def _rms(x, g):
    return x * lax.rsqrt(jnp.mean(x * x, axis=-1, keepdims=True) + NORM_EPS) * g


def _dsa_prep_body(cq_ref, ckv_ref, sm_ref, tabh_ref, tabi_ref, gq_ref, gkv_ref,
                   wuq_ref, wuqr_ref, wuk_ref, widx_ref, widxr_ref, lng_ref, lnb_ref,
                   qa_ref, ka_ref, qi_ref, ki_ref, wi_ref, sg_ref):
    tm = cq_ref.shape[0]
    lane = lax.broadcasted_iota(I32, (tm, LANES), 1)
    cqn = _rms(cq_ref[...], gq_ref[...]).astype(BF16)
    ckvn = _rms(ckv_ref[...], gkv_ref[...])
    s0 = sm_ref[:, 0:LANES]
    s1 = sm_ref[:, LANES:2 * LANES]
    ch, sh = tabh_ref[:, 0:LANES], tabh_ref[:, LANES:2 * LANES]
    ci, si = tabi_ref[:, 0:LANES], tabi_ref[:, LANES:2 * LANES]

    k_rope = s0 * ch + pltpu.roll(s0, 32, 1) * sh
    ka_ref[:, 0:A_KV_RANK] = ckvn.astype(BF16)
    ka_ref[:, A_KV_RANK:QA_DIM] = jnp.where(lane < ROPE_DIM, k_rope, 0.0).astype(BF16)

    inside = jnp.where(lane >= 32, jnp.where(lane < 96, 1.0, 0.0), 0.0)
    mu = jnp.sum(s0 * inside, axis=-1, keepdims=True) * (1.0 / IDX_DIM)
    xc = (s0 - mu) * inside
    var = jnp.sum(xc * xc, axis=-1, keepdims=True) * (1.0 / IDX_DIM)
    y = xc * lax.rsqrt(var + NORM_EPS) * lng_ref[...] + lnb_ref[...]
    rot = jnp.where(lane < 40, pltpu.roll(y, LANES - 8, 1), pltpu.roll(y, 8, 1))
    ki_ref[...] = (y * ci + rot * si).astype(BF16)

    scale = HEAD_DIM ** -0.5
    w_scale = IDX_HEADS ** -0.5 * IDX_DIM ** -0.5
    q_all, qr_all = _dot(cqn, wuq_ref[...]), _dot(cqn, wuqr_ref[...])
    qi_all, qir_all = _dot(cqn, widx_ref[...]), _dot(cqn, widxr_ref[...])
    for h in range(A_HEADS):
        hs = slice(h * LANES, (h + 1) * LANES)
        roped = q_all[:, hs] * ch + qr_all[:, hs] * sh
        qabs = _dot(roped.astype(BF16), wuk_ref[h])
        qa_ref[h, :, 0:A_KV_RANK] = (qabs * scale).astype(BF16)
        qa_ref[h, :, A_KV_RANK:QA_DIM] = (jnp.where(lane < ROPE_DIM, roped, 0.0) * scale).astype(BF16)
        qi_ref[h] = (qi_all[:, hs] * ci + qir_all[:, hs] * si).astype(BF16)
        wi_ref[h] = jnp.broadcast_to(s1[:, h:h + 1], (tm, LANES)) * w_scale
    sg_ref[...] = jax.nn.sigmoid(s1)


def _dsa_prep(z, tabh, tabi, gq, gkv, wuq, wuqr, wuk, widx, widxr, lng, lnb):
    nb, t, _ = z.shape
    tm = 256
    full = lambda a: pl.BlockSpec(a.shape, lambda b, i: (0,) * a.ndim)
    return pl.pallas_call(
        _dsa_prep_body,
        grid=(nb, t // tm),
        in_specs=[pl.BlockSpec((None, tm, A_Q_RANK), lambda b, i: (b, i, Z_CQ // A_Q_RANK)),
                  pl.BlockSpec((None, tm, A_KV_RANK), lambda b, i: (b, i, Z_CKV // A_KV_RANK)),
                  pl.BlockSpec((None, tm, 256), lambda b, i: (b, i, Z_SMALL // 256)),
                  pl.BlockSpec((None, tm, 256), lambda b, i: (b, i, 0)),
                  pl.BlockSpec((None, tm, 256), lambda b, i: (b, i, 0)),
                  full(gq), full(gkv), full(wuq), full(wuqr), full(wuk), full(widx), full(widxr),
                  full(lng), full(lnb)],
        out_specs=[pl.BlockSpec((None, A_HEADS, tm, QA_DIM), lambda b, i: (b, 0, i, 0)),
                   pl.BlockSpec((None, tm, QA_DIM), lambda b, i: (b, i, 0)),
                   pl.BlockSpec((None, IDX_HEADS, tm, LANES), lambda b, i: (b, 0, i, 0)),
                   pl.BlockSpec((None, tm, LANES), lambda b, i: (b, i, 0)),
                   pl.BlockSpec((None, IDX_HEADS, tm, LANES), lambda b, i: (b, 0, i, 0)),
                   pl.BlockSpec((None, tm, LANES), lambda b, i: (b, i, 0))],
        out_shape=[jax.ShapeDtypeStruct((nb, A_HEADS, t, QA_DIM), BF16),
                   jax.ShapeDtypeStruct((nb, t, QA_DIM), BF16),
                   jax.ShapeDtypeStruct((nb, IDX_HEADS, t, LANES), BF16),
                   jax.ShapeDtypeStruct((nb, t, LANES), BF16),
                   jax.ShapeDtypeStruct((nb, IDX_HEADS, t, LANES), F32),
                   jax.ShapeDtypeStruct((nb, t, LANES), F32)],
        compiler_params=_cparams(("parallel", "parallel")),
    )(z, z, z, tabh, tabi, gq, gkv, wuq, wuqr, wuk, widx, widxr, lng, lnb)


def _fold_lanes(x, op):
    out = x[:, 0:LANES]
    for c in range(1, x.shape[1] // LANES):
        out = op(out, x[:, c * LANES:(c + 1) * LANES])
    return out


def _dsa_attn_body(qi_ref, wi_ref, ki_ref, qa_ref, ka_ref, wuv_ref, o_ref,
                   key_ref, thr_ref, acc_ref, m_ref, l_ref, *, topk, idx_bits):
    qb = DSA_QB
    rows = A_HEADS * qb
    q0 = pl.program_id(1) * qb
    nt = (q0 + qb + KEY_TILE - 1) // KEY_TILE
    tq = q0 + lax.broadcasted_iota(I32, (qb, KEY_TILE), 0)
    lane = lax.broadcasted_iota(I32, (qb, KEY_TILE), 1)

    def score_body(kt, carry):
        smin, smax = carry
        k0 = pl.multiple_of(kt * KEY_TILE, KEY_TILE)
        ki = ki_ref[pl.ds(k0, KEY_TILE), :]
        s = None
        for h in range(IDX_HEADS):
            r = jnp.maximum(_dot_nt(qi_ref[h], ki), 0.0) * _lane_tile(wi_ref[h], KEY_TILE // LANES)
            s = r if s is None else s + r
        s = s + 0.0
        causal = k0 + lane <= tq
        key_ref[kt] = _f32_key(jnp.where(causal, s, -jnp.inf))
        smin = jnp.minimum(smin, _fold_lanes(jnp.where(causal, s, jnp.inf), jnp.minimum))
        smax = jnp.maximum(smax, _fold_lanes(jnp.where(causal, s, -jnp.inf), jnp.maximum))
        return smin, smax

    smin, smax = lax.fori_loop(0, nt, score_body,
                               (jnp.full((qb, LANES), jnp.inf, F32), jnp.full((qb, LANES), -jnp.inf, F32)))
    _topk_select(key_ref, thr_ref, nt, topk, idx_bits,
                 jnp.min(smin, axis=1, keepdims=True), jnp.max(smax, axis=1, keepdims=True))
    thr = thr_ref[:, 0:1]

    m_ref[...] = jnp.full((rows, LANES), NEG_BIG, F32)
    l_ref[...] = jnp.zeros((rows, LANES), F32)
    acc_ref[...] = jnp.zeros((rows, A_KV_RANK), F32)

    def attn_body(kt, carry):
        k0 = pl.multiple_of(kt * KEY_TILE, KEY_TILE)
        ka = ka_ref[pl.ds(k0, KEY_TILE), :]
        val = ka[:, 0:A_KV_RANK]
        sel = jnp.where(key_ref[kt] > thr, k0 + lane, tq + 1) <= tq
        bias = jnp.where(sel, 0.0, NEG_BIG)
        s = _dot_nt(qa_ref[...].reshape(rows, QA_DIM), ka) + _row_tile(bias, A_HEADS)
        m_old = m_ref[...]
        m_new = jnp.maximum(m_old, jnp.max(s, axis=1, keepdims=True))
        p = jnp.exp(s - _lane_tile(m_new, KEY_TILE // LANES))
        alpha = jnp.exp(m_old - m_new)
        l_ref[...] = l_ref[...] * alpha + jnp.sum(p, axis=1, keepdims=True)
        pb = p.astype(BF16)
        half = rows // ATTN_CHUNKS
        for c in range(ATTN_CHUNKS):
            r = slice(c * half, (c + 1) * half)
            acc_ref[r] = acc_ref[r] * _lane_tile(alpha[r], A_KV_RANK // LANES) + _dot(pb[r], val)
        m_ref[...] = m_new
        return carry

    lax.fori_loop(0, nt, attn_body, 0)

    inv = 1.0 / jnp.maximum(l_ref[...], 1e-30)
    o_lat = (acc_ref[...] * _lane_tile(inv, A_KV_RANK // LANES)).astype(BF16)
    for h in range(A_HEADS):
        o_ref[:, h * HEAD_DIM:(h + 1) * HEAD_DIM] = _dot(o_lat[h * qb:(h + 1) * qb], wuv_ref[h]).astype(BF16)


def _dsa_attn(qi, wi, ki, qa, ka, wuv):
    nb, _, t, _ = qa.shape
    topk = min(IDX_TOPK_MAX, t // 4)
    idx_bits = int(np.ceil(np.log2(t)))
    qb = DSA_QB
    body = functools.partial(_dsa_attn_body, topk=topk, idx_bits=idx_bits)
    return pl.pallas_call(
        body,
        grid=(nb, t // qb),
        in_specs=[pl.BlockSpec((None, IDX_HEADS, qb, LANES), lambda b, i: (b, 0, i, 0)),
                  pl.BlockSpec((None, IDX_HEADS, qb, LANES), lambda b, i: (b, 0, i, 0)),
                  pl.BlockSpec((None, t, LANES), lambda b, i: (b, 0, 0)),
                  pl.BlockSpec((None, A_HEADS, qb, QA_DIM), lambda b, i: (b, 0, i, 0)),
                  pl.BlockSpec((None, t, QA_DIM), lambda b, i: (b, 0, 0)),
                  pl.BlockSpec(wuv.shape, lambda b, i: (0, 0, 0))],
        out_specs=pl.BlockSpec((None, qb, A_HEADS * HEAD_DIM), lambda b, i: (b, i, 0)),
        out_shape=jax.ShapeDtypeStruct((nb, t, A_HEADS * HEAD_DIM), BF16),
        scratch_shapes=[pltpu.VMEM((t // KEY_TILE, qb, KEY_TILE), I32),
                        pltpu.VMEM((qb, LANES), I32),
                        pltpu.VMEM((A_HEADS * qb, A_KV_RANK), F32),
                        pltpu.VMEM((A_HEADS * qb, LANES), F32),
                        pltpu.VMEM((A_HEADS * qb, LANES), F32)],
        compiler_params=_cparams(("parallel", "parallel")),
    )(qi, wi, ki, qa, ka, wuv)


def _nsa_prep_body(bq_ref, bkv_ref, tabh_ref, qb_ref, kc_ref, vc_ref, ks_ref, vs_ref, kw_ref, vw_ref):
    tm = bq_ref.shape[0]
    lane = lax.broadcasted_iota(I32, (tm, LANES), 1)
    ch, sh = tabh_ref[:, 0:LANES], tabh_ref[:, LANES:2 * LANES]

    def rope(x):
        rot = jnp.where(lane < ROPE_DIM // 2, pltpu.roll(x, LANES - ROPE_DIM // 2, 1),
                        pltpu.roll(x, ROPE_DIM // 2, 1))
        return x * ch + rot * sh

    scale = HEAD_DIM ** -0.5
    for h in range(B_HEADS):
        sl = slice(h * HEAD_DIM, (h + 1) * HEAD_DIM)
        qb_ref[:, sl] = (rope(bq_ref[:, sl]) * scale).astype(BF16)
    outs = (kc_ref, vc_ref, ks_ref, vs_ref, kw_ref, vw_ref)
    for kind in range(6):
        for g in range(B_GROUPS):
            c0 = (kind * B_GROUPS + g) * HEAD_DIM
            v = bkv_ref[:, c0:c0 + HEAD_DIM]
            if kind % 2 == 0:
                v = rope(v)
            outs[kind][g] = v.astype(BF16)


def _nsa_prep(z, tabh):
    nb, t, _ = z.shape
    tm = 256
    kv_spec = pl.BlockSpec((None, B_GROUPS, tm, HEAD_DIM), lambda b, i: (b, 0, i, 0))
    kv_shape = jax.ShapeDtypeStruct((nb, B_GROUPS, t, HEAD_DIM), BF16)
    return pl.pallas_call(
        _nsa_prep_body,
        grid=(nb, t // tm),
        in_specs=[pl.BlockSpec((None, tm, 1024), lambda b, i: (b, i, Z_BQ // 1024)),
                  pl.BlockSpec((None, tm, 1536), lambda b, i: (b, i, Z_BKV // 1536)),
                  pl.BlockSpec((None, tm, 256), lambda b, i: (b, i, 0))],
        out_specs=[pl.BlockSpec((None, tm, 1024), lambda b, i: (b, i, 0))] + [kv_spec] * 6,
        out_shape=[jax.ShapeDtypeStruct((nb, t, 1024), BF16)] + [kv_shape] * 6,
        compiler_params=_cparams(("parallel", "parallel")),
    )(z, z, tabh)


def _compress_body(ch_ref, pe_ref, w1_ref, w2_ref, o_ref):
    nc = ch_ref.shape[0]
    half = CMP_STRIDE * HEAD_DIM
    x = ch_ref[...]
    ha = _dot(x, w1_ref[0:half, :])
    hb = _dot(x, w1_ref[half:2 * half, :])
    bias = _dot(pe_ref[...], w1_ref[...])[0:1]
    pre = ha + pltpu.roll(hb, nc - 1, 0) + bias
    hid = jax.nn.gelu(pre, approximate=True)
    o_ref[...] = _dot(hid.astype(BF16), w2_ref[...]).astype(BF16)


def _compress(tok, pe, w1, w2):
    nb, ng, t, d = tok.shape
    nc = t // CMP_STRIDE
    chunks = tok.reshape(nb, ng, nc, CMP_STRIDE * d)
    pe8 = jnp.broadcast_to(pe.reshape(1, CMP_LEN * d), (8, CMP_LEN * d)).astype(BF16)
    return pl.pallas_call(
        _compress_body,
        grid=(nb, ng),
        in_specs=[pl.BlockSpec((None, None, nc, CMP_STRIDE * d), lambda b, g: (b, g, 0, 0)),
                  pl.BlockSpec(pe8.shape, lambda b, g: (0, 0)),
                  pl.BlockSpec(w1.shape, lambda b, g: (0, 0)),
                  pl.BlockSpec(w2.shape, lambda b, g: (0, 0))],
        out_specs=pl.BlockSpec((None, None, nc, d), lambda b, g: (b, g, 0, 0)),
        out_shape=jax.ShapeDtypeStruct((nb, ng, nc, d), BF16),
        compiler_params=_cparams(("parallel", "parallel")),
    )(chunks, pe8, w1.astype(BF16), w2.astype(BF16))


def _nsa_attn_body(q_ref, kc_ref, vc_ref, ks_ref, vs_ref, kw_ref, vw_ref, gate_ref, ov_ref, ex_ref,
                   o_ref, key_ref, thr_ref, acc_ref, m_ref, l_ref, *, n_cmp, n_sel):
    qb = NSA_QB
    rows = B_HPG * qb
    nc = kc_ref.shape[0]
    q0 = pl.program_id(2) * qb
    qs = jnp.concatenate([q_ref[:, j * HEAD_DIM:(j + 1) * HEAD_DIM] for j in range(B_HPG)], axis=0)

    tq_c = q0 + lax.broadcasted_iota(I32, (qb, nc), 0)
    n_id = lax.broadcasted_iota(I32, (qb, nc), 1)
    c_ok = jnp.where(n_id < n_cmp, n_id * CMP_STRIDE + (CMP_LEN - 1), tq_c + 1) <= tq_c
    c_bias = _row_tile(jnp.where(c_ok, 0.0, NEG_BIG), B_HPG)
    c_keep = _row_tile(jnp.where(c_ok, 1.0, 0.0), B_HPG)
    sc = _dot_nt(qs, kc_ref[...]) + c_bias
    e = jnp.exp(sc - jnp.max(sc, axis=1, keepdims=True)) * c_keep
    pc = e * (1.0 / jnp.maximum(jnp.sum(e, axis=1, keepdims=True), 1e-30))
    oc = _dot(pc.astype(BF16), vc_ref[...])

    pc4 = pc[0:qb]
    for j in range(1, B_HPG):
        pc4 = pc4 + pc[j * qb:(j + 1) * qb]
    p_hi, p_mid, p_lo = _split3(pc4)
    ov = ov_ref[...]
    imp = _dot(p_hi, ov) + _dot(p_mid, ov) + _dot(p_lo, ov)
    tq_b = q0 + lax.broadcasted_iota(I32, (qb, LANES), 0)
    blk = lax.broadcasted_iota(I32, (qb, LANES), 1)
    cur = tq_b // SLC_LEN
    d_cur = cur - blk
    forced = jnp.where(blk == 0, 1.0, jnp.where(d_cur == 0, 1.0, jnp.where(d_cur == 1, 1.0, 0.0)))
    admissible = blk * SLC_LEN <= tq_b
    free = jnp.where(admissible, jnp.where(forced > 0.5, -1.0, imp), -1.0)
    imp = jnp.where(forced > 0.5, jnp.inf, imp + 0.0)
    key_ref[0] = _f32_key(jnp.where(admissible, imp, -jnp.inf))
    _topk_select(key_ref, thr_ref, 1, n_sel, 7,
                 jnp.full((qb, 1), -1.0, F32), jnp.max(free, axis=1, keepdims=True))
    selm = jnp.where(key_ref[0] > thr_ref[:, 0:1], 1.0, 0.0).astype(BF16)

    nt = (q0 + qb + KEY_TILE - 1) // KEY_TILE
    tq_k = q0 + lax.broadcasted_iota(I32, (qb, KEY_TILE), 0)
    lane_k = lax.broadcasted_iota(I32, (qb, KEY_TILE), 1)
    m_ref[...] = jnp.full((rows, LANES), NEG_BIG, F32)
    l_ref[...] = jnp.zeros((rows, LANES), F32)
    acc_ref[...] = jnp.zeros((rows, HEAD_DIM), F32)

    def sel_body(kt, carry):
        k0 = pl.multiple_of(kt * KEY_TILE, KEY_TILE)
        chosen = _dot(selm, ex_ref[kt])
        ok = jnp.where(chosen > 0.5, k0 + lane_k, tq_k + 1) <= tq_k
        bias = jnp.where(ok, 0.0, NEG_BIG)
        ks = ks_ref[pl.ds(k0, KEY_TILE), :]
        vs = vs_ref[pl.ds(k0, KEY_TILE), :]
        s = _dot_nt(qs, ks) + _row_tile(bias, B_HPG)
        m_old = m_ref[...]
        m_new = jnp.maximum(m_old, jnp.max(s, axis=1, keepdims=True))
        p = jnp.exp(s - _lane_tile(m_new, KEY_TILE // LANES))
        alpha = jnp.exp(m_old - m_new)
        l_ref[...] = l_ref[...] * alpha + jnp.sum(p, axis=1, keepdims=True)
        acc_ref[...] = acc_ref[...] * alpha + _dot(p.astype(BF16), vs)
        m_ref[...] = m_new
        return carry

    lax.fori_loop(0, nt, sel_body, 0)
    os_ = acc_ref[...] / jnp.maximum(l_ref[...], 1e-30)

    span = WINDOW + qb
    start = pl.multiple_of(jnp.maximum(q0 - WINDOW, 0), qb)
    tq_w = q0 + lax.broadcasted_iota(I32, (qb, span), 0)
    wpos = start + lax.broadcasted_iota(I32, (qb, span), 1)
    w_ok = jnp.where(wpos > tq_w - WINDOW, wpos, tq_w + 1) <= tq_w
    w_bias = _row_tile(jnp.where(w_ok, 0.0, NEG_BIG), B_HPG)
    sw = _dot_nt(qs, kw_ref[pl.ds(start, span), :]) + w_bias
    ew = jnp.exp(sw - jnp.max(sw, axis=1, keepdims=True))
    inv_w = 1.0 / jnp.maximum(jnp.sum(ew, axis=1, keepdims=True), 1e-30)
    ow = _dot(ew.astype(BF16), vw_ref[pl.ds(start, span), :]) * inv_w

    gate = gate_ref[...]
    for j in range(B_HPG):
        r = slice(j * qb, (j + 1) * qb)
        res = (gate[:, 3 * j:3 * j + 1] * oc[r] + gate[:, 3 * j + 1:3 * j + 2] * os_[r]
               + gate[:, 3 * j + 2:3 * j + 3] * ow[r])
        o_ref[:, j * HEAD_DIM:(j + 1) * HEAD_DIM] = res.astype(BF16)


def _nsa_attn(qb_all, kc, vc, ks, vs, kw, vw, gates, ov, ex):
    nb, t, _ = qb_all.shape
    nc = kc.shape[2]
    n_cmp = (t - CMP_LEN) // CMP_STRIDE + 1
    n_sel = min(SLC_TOPN, t // SLC_LEN)
    qb = NSA_QB
    body = functools.partial(_nsa_attn_body, n_cmp=n_cmp, n_sel=n_sel)
    res_c = pl.BlockSpec((None, None, nc, HEAD_DIM), lambda b, g, i: (b, g, 0, 0))
    res_t = pl.BlockSpec((None, None, t, HEAD_DIM), lambda b, g, i: (b, g, 0, 0))
    return pl.pallas_call(
        body,
        grid=(nb, B_GROUPS, t // qb),
        in_specs=[pl.BlockSpec((None, qb, B_HPG * HEAD_DIM), lambda b, g, i: (b, i, g)),
                  res_c, res_c, res_t, res_t, res_t, res_t,
                  pl.BlockSpec((None, None, qb, 3 * B_HPG), lambda b, g, i: (b, g, i, 0)),
                  pl.BlockSpec(ov.shape, lambda b, g, i: (0, 0)),
                  pl.BlockSpec(ex.shape, lambda b, g, i: (0, 0, 0))],
        out_specs=pl.BlockSpec((None, qb, B_HPG * HEAD_DIM), lambda b, g, i: (b, i, g)),
        out_shape=jax.ShapeDtypeStruct((nb, t, B_HEADS * HEAD_DIM), BF16),
        scratch_shapes=[pltpu.VMEM((1, qb, LANES), I32),
                        pltpu.VMEM((qb, LANES), I32),
                        pltpu.VMEM((B_HPG * qb, HEAD_DIM), F32),
                        pltpu.VMEM((B_HPG * qb, LANES), F32),
                        pltpu.VMEM((B_HPG * qb, LANES), F32)],
        compiler_params=_cparams(("parallel", "parallel", "parallel")),
    )(qb_all, kc, vc, ks, vs, kw, vw, gates, ov, ex)


def _merge_body(oa_ref, ob_ref, wa_ref, wb_ref, ga_ref, gb_ref, o_ref):
    a = _dot(oa_ref[...], wa_ref[...])
    b = _dot(ob_ref[...], wb_ref[...])
    o_ref[...] = (jax.nn.sigmoid(ga_ref[...]) * a + jax.nn.sigmoid(gb_ref[...]) * b).astype(BF16)


def _merge(o_a, o_b, w_a, w_b, z):
    nb, t, da = o_a.shape
    d = w_a.shape[1]
    tm, tn = 1024, 1024
    return pl.pallas_call(
        _merge_body,
        grid=(nb, t // tm, d // tn),
        in_specs=[pl.BlockSpec((None, tm, da), lambda b, i, j: (b, i, 0)),
                  pl.BlockSpec((None, tm, da), lambda b, i, j: (b, i, 0)),
                  pl.BlockSpec((da, tn), lambda b, i, j: (0, j)),
                  pl.BlockSpec((da, tn), lambda b, i, j: (0, j)),
                  pl.BlockSpec((None, tm, tn), lambda b, i, j: (b, i, j)),
                  pl.BlockSpec((None, tm, tn), lambda b, i, j: (b, i, d // tn + j))],
        out_specs=pl.BlockSpec((None, tm, tn), lambda b, i, j: (b, i, j)),
        out_shape=jax.ShapeDtypeStruct((nb, t, d), BF16),
        compiler_params=_cparams(("parallel", "parallel", "parallel")),
    )(o_a, o_b, w_a, w_b, z, z)


def _outproj_body(mg_ref, wo_ref, x_ref, gpost_ref, gt_ref, gpre_ref, sc_ref, sh_ref,
                  wr_ref, br_ref, x1_ref, h_ref, idx_ref, gate_ref):
    tm = x_ref.shape[0]
    y = _dot(mg_ref[...], wo_ref[...])
    x1 = x_ref[...] + gt_ref[...] * _rms(y, gpost_ref[...])
    x1_ref[...] = x1
    h = _rms(x1, gpre_ref[...]) * (1.0 + sc_ref[...]) + sh_ref[...]
    h_ref[...] = h
    h_hi = h.astype(BF16)
    h_lo = (h - h_hi.astype(F32)).astype(BF16)
    logits = (_dot(h_hi, wr_ref[0]) + (_dot(h_hi, wr_ref[1]) + _dot(h_lo, wr_ref[0]))) + br_ref[...]
    lane = lax.broadcasted_iota(I32, (tm, N_EXPERTS), 1).astype(F32)
    vals, idxs = [], []
    cur = logits
    for _ in range(TOP_K):
        m = jnp.max(cur, axis=1, keepdims=True)
        am = jnp.min(jnp.where(cur == m, lane, float(N_EXPERTS)), axis=1, keepdims=True)
        vals.append(m)
        idxs.append(am)
        cur = jnp.where(lane == am, -jnp.inf, cur)
    es = [jnp.exp(v - vals[0]) for v in vals]
    tot = es[0] + es[1] + es[2] + es[3]
    for k in range(TOP_K):
        idx_ref[:, k:k + 1] = idxs[k].astype(I32)
        gate_ref[:, k:k + 1] = es[k] / tot


def _outproj(merged, w_out, x, g_post, gt1, g_pre, sc2, sh2, wr, br):
    nb, t, d = x.shape
    tm = 512
    vec = lambda: pl.BlockSpec((1, d), lambda b, i: (0, 0))
    mod = lambda: pl.BlockSpec((None, 1, d), lambda b, i: (b, 0, 0))
    row = lambda w: pl.BlockSpec((None, tm, w), lambda b, i: (b, i, 0))
    return pl.pallas_call(
        _outproj_body,
        grid=(nb, t // tm),
        in_specs=[row(d), pl.BlockSpec((d, d), lambda b, i: (0, 0)), row(d), vec(), mod(), vec(), mod(), mod(),
                  pl.BlockSpec(wr.shape, lambda b, i: (0, 0, 0)),
                  pl.BlockSpec((1, N_EXPERTS), lambda b, i: (0, 0))],
        out_specs=[row(d), row(d), row(TOP_K), row(TOP_K)],
        out_shape=[jax.ShapeDtypeStruct((nb, t, d), F32),
                   jax.ShapeDtypeStruct((nb, t, d), F32),
                   jax.ShapeDtypeStruct((nb, t, TOP_K), I32),
                   jax.ShapeDtypeStruct((nb, t, TOP_K), F32)],
        compiler_params=_cparams(("parallel", "parallel")),
    )(merged, w_out, x, g_post[None], gt1, g_pre[None], sc2, sh2, wr, br[None])


def _gmm_body(be_ref, nu_ref, tokc_ref, tokn_ref, h_ref, wg_ref, wu_ref, bg_ref, bu_ref, wd_ref, bd_ref,
              y_ref, xf_ref, xb_ref, sem, *, ff_steps):
    i, j = pl.program_id(0), pl.program_id(1)
    n_used = nu_ref[0]
    slot = i % 2
    last = ff_steps - 1
    rows_per_step = ROW_BLK // ff_steps

    def row_copy(tok_ref, r, s):
        return pltpu.make_async_copy(h_ref.at[pl.ds(tok_ref[0, 0, r], 1), :],
                                     xf_ref.at[s, pl.ds(r, 1), :], sem.at[s])

    def wait_slot(s):
        pltpu.make_async_copy(h_ref.at[pl.ds(0, ROW_BLK), :], xf_ref.at[s], sem.at[s]).wait()

    @pl.when(j == 0)
    def _():
        @pl.when(i == 0)
        def _():
            def body(r, carry):
                row_copy(tokc_ref, r, 0).start()
                return carry

            lax.fori_loop(0, ROW_BLK, body, 0)

        @pl.when(i <= n_used)
        def _():
            wait_slot(slot)

        @pl.when(i < n_used)
        def _():
            xb_ref[...] = xf_ref[slot].astype(BF16)

    @pl.when(i < n_used)
    def _():
        x = xb_ref[...]
        g = jnp.minimum(_dot(x, wg_ref[...].astype(BF16)) + bg_ref[...], SWIGLU_LIMIT)
        u = jnp.clip(_dot(x, wu_ref[...].astype(BF16)) + bu_ref[...], -SWIGLU_LIMIT, SWIGLU_LIMIT)
        act = g * jax.nn.sigmoid(SWIGLU_ALPHA * g) * (u + 1.0)
        part = _dot(act.astype(BF16), wd_ref[...].astype(BF16))
        for r in range(rows_per_step):
            row_copy(tokn_ref, j * rows_per_step + r, 1 - slot).start()

        @pl.when(j == 0)
        def _():
            y_ref[...] = part + bd_ref[...]

        @pl.when(j > 0)
        def _():
            y_ref[...] = y_ref[...] + part

    @pl.when(jnp.logical_and(i >= n_used, j == last))
    def _():
        y_ref[...] = jnp.zeros(y_ref.shape, F32)

    @pl.when(jnp.logical_and(jnp.logical_and(i == pl.num_programs(0) - 1, j == last), i < n_used))
    def _():
        wait_slot(1 - slot)


def _gmm(blk_e, n_used, row_tok, h, wgu, bgu, wd, bd):
    d = h.shape[1]
    ff = wd.shape[1]
    p = row_tok.shape[0]
    nblk = p // ROW_BLK
    nft = ff // FF_TILE
    tok3 = row_tok.reshape(nblk, 1, ROW_BLK)

    def expert(i, be, nu):
        return be[jnp.minimum(i, nu[0] - 1)]

    def tile(i, j, nu):
        return jnp.where(i < nu[0], j, nft - 1)

    grid_spec = pltpu.PrefetchScalarGridSpec(
        num_scalar_prefetch=2,
        grid=(nblk, nft),
        in_specs=[pl.BlockSpec((1, 1, ROW_BLK), lambda i, j, be, nu: (i, 0, 0), memory_space=pltpu.SMEM),
                  pl.BlockSpec((1, 1, ROW_BLK), lambda i, j, be, nu: (jnp.minimum(i + 1, nblk - 1), 0, 0),
                               memory_space=pltpu.SMEM),
                  pl.BlockSpec(memory_space=pl.ANY),
                  pl.BlockSpec((None, d, FF_TILE), lambda i, j, be, nu: (expert(i, be, nu), 0, tile(i, j, nu))),
                  pl.BlockSpec((None, d, FF_TILE),
                               lambda i, j, be, nu: (expert(i, be, nu), 0, nft + tile(i, j, nu))),
                  pl.BlockSpec((None, 1, FF_TILE), lambda i, j, be, nu: (expert(i, be, nu), 0, tile(i, j, nu))),
                  pl.BlockSpec((None, 1, FF_TILE),
                               lambda i, j, be, nu: (expert(i, be, nu), 0, nft + tile(i, j, nu))),
                  pl.BlockSpec((None, FF_TILE, d), lambda i, j, be, nu: (expert(i, be, nu), tile(i, j, nu), 0)),
                  pl.BlockSpec((None, 1, d), lambda i, j, be, nu: (expert(i, be, nu), 0, 0))],
        out_specs=pl.BlockSpec((ROW_BLK, d), lambda i, j, be, nu: (i, 0)),
        scratch_shapes=[pltpu.VMEM((2, ROW_BLK, d), F32),
                        pltpu.VMEM((ROW_BLK, d), BF16),
                        pltpu.SemaphoreType.DMA((2,))],
    )
    assert ROW_BLK % nft == 0
    return pl.pallas_call(
        functools.partial(_gmm_body, ff_steps=nft),
        grid_spec=grid_spec,
        out_shape=jax.ShapeDtypeStruct((p, d), F32),
        compiler_params=_cparams(("arbitrary", "arbitrary")),
    )(blk_e, n_used, tok3, tok3, h, wgu, wgu, bgu, bgu, wd, bd)


def _combine_body(destc_ref, destn_ref, ys_ref, gate_ref, x1_ref, g_ref, gt_ref, o_ref, buf, sem):
    tm = CMB_TOK
    i = pl.program_id(0)
    slot = i % 2

    def issue(dest_ref, s):
        def body(r, carry):
            pltpu.make_async_copy(ys_ref.at[pl.ds(dest_ref[0, 0, r], 1), :],
                                  buf.at[s, pl.ds(r, 1), :], sem.at[s]).start()
            return carry

        lax.fori_loop(0, TOP_K * tm, body, 0, unroll=16)

    @pl.when(i == 0)
    def _():
        issue(destc_ref, 0)

    pltpu.make_async_copy(ys_ref.at[pl.ds(0, TOP_K * tm), :], buf.at[slot], sem.at[slot]).wait()

    @pl.when(i + 1 < pl.num_programs(0))
    def _():
        issue(destn_ref, 1 - slot)

    gate = gate_ref[...]
    y = buf[slot, 0:tm] * gate[:, 0:1]
    for k in range(1, TOP_K):
        y = y + buf[slot, k * tm:(k + 1) * tm] * gate[:, k:k + 1]
    o_ref[...] = x1_ref[...] + gt_ref[...] * _rms(y, g_ref[...])


def _combine(dest, ys, gates, x1, g_post, gt2):
    nb, t, d = x1.shape
    n = nb * t
    tm = CMB_TOK
    steps = n // tm
    dest_t = dest.reshape(steps, tm, TOP_K).transpose(0, 2, 1).reshape(steps, 1, TOP_K * tm)
    out = pl.pallas_call(
        _combine_body,
        grid=(steps,),
        in_specs=[pl.BlockSpec((1, 1, TOP_K * tm), lambda i: (i, 0, 0), memory_space=pltpu.SMEM),
                  pl.BlockSpec((1, 1, TOP_K * tm), lambda i: (jnp.minimum(i + 1, steps - 1), 0, 0),
                               memory_space=pltpu.SMEM),
                  pl.BlockSpec(memory_space=pl.ANY),
                  pl.BlockSpec((tm, TOP_K), lambda i: (i, 0)),
                  pl.BlockSpec((tm, d), lambda i: (i, 0)),
                  pl.BlockSpec((1, d), lambda i: (0, 0)),
                  pl.BlockSpec((None, 1, d), lambda i: ((i * tm) // t, 0, 0))],
        out_specs=pl.BlockSpec((tm, d), lambda i: (i, 0)),
        out_shape=jax.ShapeDtypeStruct((n, d), F32),
        scratch_shapes=[pltpu.VMEM((2, TOP_K * tm, d), F32), pltpu.SemaphoreType.DMA((2,))],
        compiler_params=_cparams(("arbitrary",)),
    )(dest_t, dest_t, ys, gates, x1.reshape(n, d), g_post[None], gt2)
    return out.reshape(nb, t, d)


def _route(top_idx):
    n = top_idx.shape[0]
    nk = n * TOP_K
    e = top_idx.reshape(nk)
    onehot = (e[:, None] == jnp.arange(N_EXPERTS, dtype=I32)[None, :]).astype(I32)
    csum = jnp.cumsum(onehot, axis=0)
    rank = jnp.sum((csum - onehot) * onehot, axis=1)
    counts = csum[-1]
    padded = (counts + ROW_BLK - 1) // ROW_BLK * ROW_BLK
    pend = jnp.cumsum(padded)
    pstart = pend - padded
    dest = (pstart[e] + rank).astype(I32)
    nblk = nk // ROW_BLK + N_EXPERTS
    tok = jnp.arange(nk, dtype=I32) // TOP_K
    row_tok = jnp.zeros((nblk * ROW_BLK,), I32).at[dest].set(tok, unique_indices=True)
    blk_start = jnp.arange(nblk, dtype=I32) * ROW_BLK
    blk_e = jnp.minimum(jnp.sum((pend[None, :] <= blk_start[:, None]).astype(I32), axis=1), N_EXPERTS - 1)
    n_used = (pend[-1:] // ROW_BLK).astype(I32)
    return dest.reshape(n, TOP_K), row_tok, blk_e.astype(I32), n_used


def _rot_cols(w, half):
    rest = jnp.zeros(w.shape[:-1] + (w.shape[-1] - 2 * half,), w.dtype)
    return jnp.concatenate([w[..., half:2 * half], w[..., 0:half], rest], axis=-1)


def _prep_w_in(w_in):
    d = w_in.shape[0]
    offs = np.cumsum([0, A_Q_RANK, A_KV_RANK, ROPE_DIM, IDX_DIM, IDX_HEADS, B_HEADS * HEAD_DIM,
                      6 * B_GROUPS * HEAD_DIM, 3 * B_HEADS, 2 * d])
    cq, ckv, krope, kidx, widx, bq, bkv, bgate, mgate = [w_in[:, int(offs[i]):int(offs[i + 1])] for i in range(9)]
    krope_rot = jnp.concatenate([krope[:, ROPE_DIM // 2:], krope[:, :ROPE_DIM // 2]], axis=1)
    small = jnp.concatenate([krope, kidx, krope_rot, widx, bgate,
                             jnp.zeros((d, 256 - 160), w_in.dtype)], axis=1)
    w = jnp.concatenate([mgate, bq, cq, ckv, small, bkv], axis=1)
    assert w.shape[1] == Z_COLS
    return w.astype(BF16)


def _rope_tables(positions):
    def tab(rot_dim, lo, width):
        inv = jnp.power(ROPE_THETA, -jnp.arange(0, rot_dim, 2, dtype=F32) / rot_dim)
        ang = positions.astype(F32)[..., None] * inv
        half = rot_dim // 2
        cs = jnp.concatenate([jnp.cos(ang), jnp.sin(ang)], axis=-1)
        place = np.zeros((rot_dim, 2 * LANES), np.float32)
        const = np.zeros((2 * LANES,), np.float32)
        for i in range(half):
            place[i, lo + i] = place[i, lo + half + i] = 1.0
            place[half + i, LANES + lo + i] = -1.0
            place[half + i, LANES + lo + half + i] = 1.0
        const[lo + rot_dim:lo + width] = 1.0
        return jnp.einsum("btk,kn->btn", cs, jnp.asarray(place), precision=lax.Precision.HIGHEST) + const

    return tab(ROPE_DIM, 0, LANES), tab(IDX_ROPE, 32, IDX_DIM)


def _select_tables(t):
    n_cmp = (t - CMP_LEN) // CMP_STRIDE + 1
    nc = t // CMP_STRIDE
    n = np.arange(nc)[:, None]
    m = np.arange(LANES)[None, :]
    ov = (n * CMP_STRIDE < m * SLC_LEN + SLC_LEN) & (n * CMP_STRIDE + CMP_LEN - 1 >= m * SLC_LEN) & (n < n_cmp)
    key_blk = (np.arange(t) // SLC_LEN).reshape(t // KEY_TILE, 1, KEY_TILE)
    ex = key_blk == np.arange(LANES).reshape(1, LANES, 1)
    return jnp.asarray(ov, BF16), jnp.asarray(ex, BF16)


def kernel(x, c, positions, w_ada, b_ada, g_pre_mix, g_post_mix, g_pre_ffn, g_post_ffn, w_in, a_cq_norm, a_ckv_norm, a_w_uq, a_w_uk, a_w_uv, idx_w_q, idx_k_norm_g, idx_k_norm_b, cmp_k_pe, cmp_k_w1, cmp_k_w2, cmp_v_pe, cmp_v_w1, cmp_v_w2, w_br_a, w_br_b, w_out, w_router, b_router, w_gate_up, b_gate_up, w_down, b_down):
    nb, t, d = x.shape
    n = nb * t
    assert t % KEY_TILE == 0 and t // SLC_LEN <= LANES and d % LANES == 0

    mod = _adaln(c, w_ada, b_ada)
    sh1, sc1, gt1, sh2, sc2, gt2 = [m[:, None, :] for m in jnp.split(mod, 6, axis=-1)]

    z = _inproj(x, g_pre_mix, sc1, sh1, _prep_w_in(w_in))
    tabh, tabi = _rope_tables(positions)

    wuk = jnp.transpose(a_w_uk, (1, 2, 0))
    wuk = jnp.concatenate([jnp.zeros((A_HEADS, ROPE_DIM, A_KV_RANK), wuk.dtype), wuk], axis=1)
    pad_idx = lambda w: jnp.pad(w, ((0, 0), (0, 0), (32, LANES - 32 - IDX_DIM)))
    pad_vec = lambda v: jnp.pad(v, (32, LANES - 32 - IDX_DIM))[None]
    heads_wide = lambda w: w.reshape(A_Q_RANK, -1).astype(BF16)
    qa, ka, qi, ki, wi, sg = _dsa_prep(
        z, tabh, tabi, a_cq_norm[None], a_ckv_norm[None],
        heads_wide(a_w_uq), heads_wide(_rot_cols(a_w_uq, ROPE_DIM // 2)), wuk.astype(BF16),
        heads_wide(pad_idx(idx_w_q)), heads_wide(pad_idx(_rot_cols(idx_w_q, IDX_ROPE // 2))),
        pad_vec(idx_k_norm_g), pad_vec(idx_k_norm_b))
    o_a = _dsa_attn(qi, wi, ki, qa, ka, a_w_uv.astype(BF16))

    qb_all, kc_tok, vc_tok, ks, vs, kw, vw = _nsa_prep(z, tabh)
    kc = _compress(kc_tok, cmp_k_pe, cmp_k_w1, cmp_k_w2)
    vc = _compress(vc_tok, cmp_v_pe, cmp_v_w1, cmp_v_w2)
    gates_b = sg[:, :, IDX_HEADS:IDX_HEADS + 3 * B_HEADS].reshape(nb, t, B_GROUPS, 3 * B_HPG).transpose(0, 2, 1, 3)
    ov, ex = _select_tables(t)
    o_b = _nsa_attn(qb_all, kc, vc, ks, vs, kw, vw, gates_b, ov, ex)

    merged = _merge(o_a, o_b, w_br_a.astype(BF16), w_br_b.astype(BF16), z)

    wr_hi = w_router.astype(BF16)
    wr = jnp.stack([wr_hi, (w_router - wr_hi.astype(F32)).astype(BF16)])
    x1, h2, top_idx, gates = _outproj(merged, w_out.astype(BF16), x, g_post_mix, gt1, g_pre_ffn, sc2, sh2,
                                      wr, b_router)
    dest, row_tok, blk_e, n_used = _route(top_idx.reshape(n, TOP_K))
    ys = _gmm(blk_e, n_used, row_tok, h2.reshape(n, d), w_gate_up, b_gate_up[:, None, :],
              w_down, b_down[:, None, :])
    return _combine(dest, ys, gates.reshape(n, TOP_K), x1, g_post_ffn, gt2)
```
